```python
import math
import jax
import jax.numpy as jnp
from jax import lax
import numpy as np

D_MODEL = 4096
BATCH = 2
SEQ = 8192
DEPTH = 4

GRID_W = 64
CTX_LEN = 256
N_MOD = 6
ADA_RANK = D_MODEL // 8
EPS = 1e-6
NEG_INF = -1e30
HEAD_DIM = 128
ROPE_BASE = 10000.0
BLOCK = 128
WINDOW = 128
HY_CH = D_MODEL // 2
HY_ORDER = 2
HY_SHORT = 3
HY_EMB = 33
HY_BANDS = (HY_EMB - 1) // 2
HY_FILTER_HID = 64
HY_FILTER_HIDDEN_LAYERS = 2
HY_DIRS = 2
HY_FAST_DECAY = 0.3
HY_SLOW_DECAY = 1.5
HY_DECAY_TARGET = 1e-2
HY_WIDTH = (HY_ORDER + 1) * HY_CH
WIN_HEADS = (D_MODEL - HY_CH) // HEAD_DIM
WIN_KV_HEADS = WIN_HEADS // 4
WIN_Q = WIN_HEADS * HEAD_DIM
WIN_KV = WIN_KV_HEADS * HEAD_DIM
EV_IN = HY_WIDTH + WIN_Q + 2 * WIN_KV
EV_MIX = HY_CH + WIN_Q
DIFF_DIM = 128
DIFF_HEADS = D_MODEL // (2 * DIFF_DIM)
DIFF_Q = DIFF_HEADS * 2 * DIFF_DIM
D_FF = 2 * D_MODEL
N_EXPERTS = 8
TOP_K = 2
D_FF_EXPERT = 3 * D_MODEL // 8
N_EVEN = (DEPTH + 1) // 2
N_ODD = DEPTH // 2

kernel_name = "hybrid_hyena_swa_diffattn_moe_dit"


def rms_norm(x, g):
    xf = x.astype(jnp.float32)
    y = xf * lax.rsqrt(jnp.mean(xf * xf, axis=-1, keepdims=True) + EPS)
    return (y * g.astype(jnp.float32)).astype(x.dtype)


def modulate(h, shift, scale):
    return h * (1 + scale) + shift


def ada_modulation(cv, down, up, b):
    m = (jax.nn.silu(cv) @ down) @ up + b
    return m.reshape(m.shape[:-1] + (N_MOD, m.shape[-1] // N_MOD))


def axial_rope_tables(rows, cols, dim):
    quarter = dim // 4
    inv = ROPE_BASE ** (-jnp.arange(quarter, dtype=jnp.float32) / quarter)
    ar = rows.astype(jnp.float32)[:, None] * inv
    ac = cols.astype(jnp.float32)[:, None] * inv
    ang = jnp.concatenate([ar, ar, ac, ac], axis=-1)
    return jnp.cos(ang), jnp.sin(ang)


def apply_axial_rope(x, cos, sin):
    x1, x2, x3, x4 = jnp.split(x, 4, axis=-1)
    rot = jnp.concatenate([-x2, x1, -x4, x3], axis=-1)
    shp = (1, x.shape[1]) + (1,) * (x.ndim - 3) + (x.shape[-1],)
    return (x * cos.reshape(shp) + rot * sin.reshape(shp)).astype(x.dtype)


def heads(t, dim):
    return t.reshape(t.shape[:-1] + (t.shape[-1] // dim, dim))


def short_conv(u, w, b):
    L = u.shape[1]
    pad = HY_SHORT // 2
    up = jnp.pad(u, ((0, 0), (pad, pad), (0, 0)))
    y = b + up[:, 0:L] * w[0]
    for j in range(1, HY_SHORT):
        y = y + up[:, j:j + L] * w[j]
    return y


def implicit_filters(L, w_in, w_hid, b, freq, w_out):
    f32 = jnp.float32
    t = jnp.linspace(0.0, 1.0, L, dtype=f32)[:, None]
    w = (2.0 * math.pi / L) * jnp.arange(L, dtype=f32)[:, None]
    f = jnp.linspace(1e-4, HY_BANDS - 1, HY_BANDS, dtype=f32)[None, :]
    z = jnp.concatenate([t, jnp.cos(f * w), -jnp.sin(f * w)], axis=-1)
    b = b.astype(f32)
    freq = freq.astype(f32)
    h = jnp.sin(freq[0] * (z @ w_in.astype(f32) + b[0]))
    for n in range(HY_FILTER_HIDDEN_LAYERS):
        h = jnp.sin(freq[n + 1] * (h @ w_hid[n].astype(f32) + b[n + 1]))
    h = (h @ w_out.astype(f32)).reshape(L, HY_DIRS, HY_ORDER, HY_CH)
    max_decay = math.log(HY_DECAY_TARGET) / HY_FAST_DECAY
    min_decay = math.log(HY_DECAY_TARGET) / HY_SLOW_DECAY
    deltas = jnp.abs(jnp.linspace(min_decay, max_decay, HY_ORDER * HY_CH, dtype=f32)).reshape(HY_ORDER, HY_CH)
    decay = jnp.exp(-t[:, :, None] * deltas)
    h = h * decay[:, None]
    fwd = h[:, 0]
    bwd = h[:0:-1, 1]
    return jnp.concatenate([fwd, jnp.zeros_like(fwd[:1]), bwd], axis=0)


def hyena(u, conv_w, conv_b, f_w_in, f_w_hid, f_b, f_freq, f_w_out, skip):
    L = u.shape[1]
    u = short_conv(u, conv_w, conv_b)
    v, *gates = jnp.split(u, HY_ORDER + 1, axis=-1)
    filt_f = jnp.fft.rfft(implicit_filters(L, f_w_in, f_w_hid, f_b, f_freq, f_w_out), axis=0)
    skip = skip.astype(jnp.float32)
    z = v.astype(jnp.float32)
    for n, g in enumerate(gates):
        conv = jnp.fft.irfft(jnp.fft.rfft(z, n=2 * L, axis=1) * filt_f[:, n], n=2 * L, axis=1)[:, :L]
        z = g.astype(jnp.float32) * (conv + z * skip[n])
    return z.astype(u.dtype)


def band_blocks(t, nb):
    B = t.shape[0]
    tb = t.reshape((B, nb, BLOCK) + t.shape[2:])
    tp = jnp.pad(tb, ((0, 0), (1, 1)) + ((0, 0),) * (tb.ndim - 2))
    return jnp.concatenate([tp[:, :-2], tp[:, 1:-1], tp[:, 2:]], axis=2)


def window_attention_latent(q, k, v, k_c, v_c, sink):
    B, L, H, d = q.shape
    Hkv = k.shape[2]
    G = H // Hkv
    Lc = k_c.shape[1]
    nb = L // BLOCK
    scale = d ** -0.5
    qb = q.reshape(B, nb, BLOCK, Hkv, G, d)
    kb = band_blocks(k, nb)
    vb = band_blocks(v, nb)
    s_band = jnp.einsum('bnqhgd,bnkhd->bnhgqk', qb, kb, preferred_element_type=jnp.float32) * scale
    s_ctx = jnp.einsum('bnqhgd,bchd->bnhgqc', qb, k_c, preferred_element_type=jnp.float32) * scale
    blk = jnp.arange(nb)[:, None, None]
    qpos = blk * BLOCK + jnp.arange(BLOCK)[None, :, None]
    kpos = (blk - 1) * BLOCK + jnp.arange(3 * BLOCK)[None, None, :]
    valid = (jnp.abs(qpos - kpos) <= WINDOW) & (kpos >= 0) & (kpos < L)
    s_band = jnp.where(valid[None, :, None, None], s_band, NEG_INF)
    sink_b = jnp.broadcast_to(sink.astype(jnp.float32).reshape(Hkv, G, 1, 1), (B, nb, Hkv, G, BLOCK, 1))
    p = jax.nn.softmax(jnp.concatenate([sink_b, s_ctx, s_band], axis=-1), axis=-1)
    o = (jnp.einsum('bnhgqc,bchd->bnqhgd', p[..., 1:1 + Lc].astype(v.dtype), v_c)
         + jnp.einsum('bnhgqk,bnkhd->bnqhgd', p[..., 1 + Lc:].astype(v.dtype), vb))
    return o.reshape(B, L, H * d)


def context_sink_attention(q, k, v, sink):
    B, Lc, H, d = q.shape
    Hkv = k.shape[2]
    G = H // Hkv
    qg = q.reshape(B, Lc, Hkv, G, d)
    s = jnp.einsum('bqhgd,bkhd->bhgqk', qg, k, preferred_element_type=jnp.float32) * d ** -0.5
    sink_b = jnp.broadcast_to(sink.astype(jnp.float32).reshape(Hkv, G, 1, 1), (B, Hkv, G, Lc, 1))
    p = jax.nn.softmax(jnp.concatenate([sink_b, s], axis=-1), axis=-1)[..., 1:]
    o = jnp.einsum('bhgqk,bkhd->bqhgd', p.astype(v.dtype), v)
    return o.reshape(B, Lc, H * d)


def hyena_window_mixer(hc, hl, cos, sin, w_in, conv_w, conv_b, f_w_in, f_w_hid, f_b, f_freq, f_w_out,
                       hy_skip, qk_g, sink, w_out, ctx_out):
    cuts = [HY_WIDTH, HY_WIDTH + WIN_Q, HY_WIDTH + WIN_Q + WIN_KV]
    u_l, q_l, k_l, v_l = jnp.split(hl @ w_in, cuts, axis=-1)
    q_l = apply_axial_rope(rms_norm(heads(q_l, HEAD_DIM), qk_g[0]), cos, sin)
    k_l = apply_axial_rope(rms_norm(heads(k_l, HEAD_DIM), qk_g[1]), cos, sin)
    v_l = heads(v_l, HEAD_DIM)
    if ctx_out:
        u_c, q_c, k_c, v_c = jnp.split(hc @ w_in, cuts, axis=-1)
    else:
        k_c, v_c = jnp.split(hc @ w_in[:, cuts[1]:], 2, axis=-1)
    k_c = rms_norm(heads(k_c, HEAD_DIM), qk_g[1])
    v_c = heads(v_c, HEAD_DIM)
    y_l = jnp.concatenate(
        [hyena(u_l, conv_w, conv_b, f_w_in, f_w_hid, f_b, f_freq, f_w_out, hy_skip),
         window_attention_latent(q_l, k_l, v_l, k_c, v_c, sink)], axis=-1) @ w_out
    if not ctx_out:
        return None, y_l
    q_c = rms_norm(heads(q_c, HEAD_DIM), qk_g[0])
    y_c = jnp.concatenate(
        [hyena(u_c, conv_w, conv_b, f_w_in, f_w_hid, f_b, f_freq, f_w_out, hy_skip),
         context_sink_attention(q_c, k_c, v_c, sink)], axis=-1) @ w_out
    return y_c, y_l


def diff_attend(q, k, v, lam):
    s = jnp.einsum('bqhmd,bkhmd->bhmqk', q, k, preferred_element_type=jnp.float32) * q.shape[-1] ** -0.5
    p = jax.nn.softmax(s, axis=-1)
    w = p[:, :, 0] - lam * p[:, :, 1]
    return jnp.einsum('bhqk,bkhe->bqhe', w.astype(v.dtype), v)


def diff_attention_mixer(hc, hl, cos, sin, w_qkv, qk_g, lam_p, subln_g, w_out, lam_init, ctx_out):
    B, L, _ = hl.shape

    def qk(t, g):
        return rms_norm(t.reshape(t.shape[:-1] + (DIFF_HEADS, 2, DIFF_DIM)), g)

    def vv(t):
        return t.reshape(t.shape[:-1] + (DIFF_HEADS, 2 * DIFF_DIM))

    q_l, k_l, v_l = jnp.split(hl @ w_qkv, 3, axis=-1)
    q_l = apply_axial_rope(qk(q_l, qk_g[0]), cos, sin)
    k_l = apply_axial_rope(qk(k_l, qk_g[1]), cos, sin)
    v_l = vv(v_l)
    if ctx_out:
        q_c, k_c, v_c = jnp.split(hc @ w_qkv, 3, axis=-1)
    else:
        k_c, v_c = jnp.split(hc @ w_qkv[:, DIFF_Q:], 2, axis=-1)
    k_c = qk(k_c, qk_g[1])
    v_c = vv(v_c)
    lp = lam_p.astype(jnp.float32)
    lam = jnp.exp(jnp.sum(lp[0] * lp[1])) - jnp.exp(jnp.sum(lp[2] * lp[3])) + lam_init
    k_all = jnp.concatenate([k_c, k_l], axis=1)
    v_all = jnp.concatenate([v_c, v_l], axis=1)
    nb = L // BLOCK
    qb = jnp.moveaxis(q_l.reshape(B, nb, BLOCK, DIFF_HEADS, 2, DIFF_DIM), 1, 0)
    ob = lax.map(lambda qq: diff_attend(qq, k_all, v_all, lam), qb)
    o_l = jnp.moveaxis(ob, 0, 1).reshape(B, L, DIFF_HEADS, 2 * DIFF_DIM)

    def out(o):
        o = rms_norm(o, subln_g) * (1.0 - lam_init)
        return o.reshape(o.shape[:2] + (DIFF_Q,)) @ w_out

    y_l = out(o_l)
    if not ctx_out:
        return None, y_l
    q_c = qk(q_c, qk_g[0])
    return out(diff_attend(q_c, k_c, v_c, lam)), y_l


def swiglu(h, w1, w3, w2):
    return (jax.nn.silu(h @ w1) * (h @ w3)) @ w2


def moe_swiglu(h, router_w, router_b, w1, w3, w2):
    logits = jnp.dot(h, router_w, preferred_element_type=jnp.float32) + router_b.astype(jnp.float32)
    top_v, top_i = lax.top_k(logits, TOP_K)
    gates = jax.nn.softmax(top_v, axis=-1)
    combine = jnp.einsum('tk,tke->te', gates, jax.nn.one_hot(top_i, N_EXPERTS, dtype=jnp.float32)).astype(h.dtype)
    out = jnp.zeros_like(h)
    for e in range(N_EXPERTS):
        out = out + combine[:, e, None] * swiglu(h, w1[e], w3[e], w2[e])
    return out


def setup_inputs(seed: int = 0) -> dict:
    key = jax.random.key(seed)
    keys = iter(jax.random.split(key, 40))

    def nrm(shape, scale):
        return jax.random.normal(next(keys), shape, jnp.float32) * scale

    def gain(shape):
        return 1.0 + nrm(shape, 0.02)

    D, NE, NO = D_MODEL, N_EVEN, N_ODD
    return {
        'x': nrm((BATCH, SEQ, D), 1.0),
        'c': nrm((BATCH, D), 1.0),
        'ctx': nrm((BATCH, CTX_LEN, D), 1.0),
        'c_ctx': nrm((D,), 1.0),
        'norm_g': gain((DEPTH, 2, D)),
        'ada_down': nrm((DEPTH, D, ADA_RANK), D ** -0.5),
        'ada_up': nrm((DEPTH, ADA_RANK, N_MOD * D), 0.5 * ADA_RANK ** -0.5),
        'ada_b': nrm((DEPTH, N_MOD * D), 0.02),
        'ev_w_in': nrm((NE, D, EV_IN), D ** -0.5),
        'ev_conv_w': nrm((NE, HY_SHORT, HY_WIDTH), HY_SHORT ** -0.5),
        'ev_conv_b': nrm((NE, HY_WIDTH), 0.02),
        'ev_filt_w_in': nrm((NE, HY_EMB, HY_FILTER_HID), HY_EMB ** -0.5),
        'ev_filt_w_hid': nrm((NE, HY_FILTER_HIDDEN_LAYERS, HY_FILTER_HID, HY_FILTER_HID), HY_FILTER_HID ** -0.5),
        'ev_filt_b': nrm((NE, HY_FILTER_HIDDEN_LAYERS + 1, HY_FILTER_HID), 0.02),
        'ev_filt_freq': gain((NE, HY_FILTER_HIDDEN_LAYERS + 1, HY_FILTER_HID)),
        'ev_filt_w_out': nrm((NE, HY_FILTER_HID, HY_DIRS * HY_ORDER * HY_CH), 0.02 * HY_FILTER_HID ** -0.5),
        'ev_hy_skip': nrm((NE, HY_ORDER, HY_CH), 0.1),
        'ev_qk_g': gain((NE, 2, HEAD_DIM)),
        'ev_sink': nrm((NE, WIN_HEADS), 0.5),
        'ev_w_out': nrm((NE, EV_MIX, D), EV_MIX ** -0.5),
        'ev_ffn_w1': nrm((NE, D, D_FF), D ** -0.5),
        'ev_ffn_w3': nrm((NE, D, D_FF), D ** -0.5),
        'ev_ffn_w2': nrm((NE, D_FF, D), D_FF ** -0.5),
        'od_w_qkv': nrm((NO, D, 3 * DIFF_Q), D ** -0.5),
        'od_qk_g': gain((NO, 2, DIFF_DIM)),
        'od_lambda': nrm((NO, 4, DIFF_DIM), 0.1),
        'od_subln_g': gain((NO, 2 * DIFF_DIM)),
        'od_w_out': nrm((NO, DIFF_Q, D), DIFF_Q ** -0.5),
        'od_router_w': nrm((NO, D, N_EXPERTS), D ** -0.5),
        'od_router_b': nrm((NO, N_EXPERTS), 0.01),
        'od_moe_w1': nrm((NO, N_EXPERTS, D, D_FF_EXPERT), D ** -0.5),
        'od_moe_w3': nrm((NO, N_EXPERTS, D, D_FF_EXPERT), D ** -0.5),
        'od_moe_w2': nrm((NO, N_EXPERTS, D_FF_EXPERT, D), D_FF_EXPERT ** -0.5),
    }


def reference(x, c, ctx, c_ctx, norm_g, ada_down, ada_up, ada_b,
              ev_w_in, ev_conv_w, ev_conv_b, ev_filt_w_in, ev_filt_w_hid, ev_filt_b, ev_filt_freq,
              ev_filt_w_out, ev_hy_skip, ev_qk_g, ev_sink, ev_w_out, ev_ffn_w1, ev_ffn_w3, ev_ffn_w2,
              od_w_qkv, od_qk_g, od_lambda, od_subln_g, od_w_out, od_router_w, od_router_b,
              od_moe_w1, od_moe_w3, od_moe_w2):
    B, L, D = x.shape
    Lc = ctx.shape[1]
    ROWS = L // GRID_W
    rows = jnp.repeat(jnp.arange(ROWS), GRID_W)
    cols = jnp.tile(jnp.arange(GRID_W), ROWS)
    cos, sin = axial_rope_tables(rows, cols, HEAD_DIM)
    xl, xc = x, ctx
    for i in range(DEPTH):
        last = i == DEPTH - 1
        j = i // 2
        m_l = ada_modulation(c, ada_down[i], ada_up[i], ada_b[i]).astype(xl.dtype)
        m_c = ada_modulation(c_ctx, ada_down[i], ada_up[i], ada_b[i]).astype(xc.dtype)
        hl = modulate(rms_norm(xl, norm_g[i, 0]), m_l[:, 0, None], m_l[:, 1, None])
        hc = modulate(rms_norm(xc, norm_g[i, 0]), m_c[0], m_c[1])
        if i % 2 == 0:
            yc, yl = hyena_window_mixer(hc, hl, cos, sin, ev_w_in[j], ev_conv_w[j], ev_conv_b[j],
                                        ev_filt_w_in[j], ev_filt_w_hid[j], ev_filt_b[j], ev_filt_freq[j],
                                        ev_filt_w_out[j], ev_hy_skip[j], ev_qk_g[j], ev_sink[j], ev_w_out[j],
                                        not last)
        else:
            lam_init = 0.8 - 0.6 * math.exp(-0.3 * i)
            yc, yl = diff_attention_mixer(hc, hl, cos, sin, od_w_qkv[j], od_qk_g[j], od_lambda[j],
                                          od_subln_g[j], od_w_out[j], lam_init, not last)
        xl = xl + m_l[:, 2, None] * yl
        hl = modulate(rms_norm(xl, norm_g[i, 1]), m_l[:, 3, None], m_l[:, 4, None])
        if last:
            h2 = hl
        else:
            xc = xc + m_c[2] * yc
            hc = modulate(rms_norm(xc, norm_g[i, 1]), m_c[3], m_c[4])
            h2 = jnp.concatenate([hc, hl], axis=1)
        T = h2.shape[1]
        flat = h2.reshape(B * T, D)
        if i % 2 == 0:
            f = swiglu(flat, ev_ffn_w1[j], ev_ffn_w3[j], ev_ffn_w2[j])
        else:
            f = moe_swiglu(flat, od_router_w[j], od_router_b[j], od_moe_w1[j], od_moe_w3[j], od_moe_w2[j])
        f = f.reshape(B, T, D)
        if last:
            xl = xl + m_l[:, 5, None] * f
        else:
            xl = xl + m_l[:, 5, None] * f[:, Lc:]
            xc = xc + m_c[5] * f[:, :Lc]
    return xl
```

```python
import functools
import math

import jax
import jax.numpy as jnp
from jax import lax
from jax.experimental import pallas as pl
from jax.experimental.pallas import tpu as pltpu

D_MODEL = 4096
DEPTH = 4
GRID_W = 64
N_MOD = 6
EPS = 1e-6
NEG_INF = -1e30
HEAD_DIM = 128
ROPE_BASE = 10000.0
BLOCK = 128
WINDOW = 128
HY_CH = D_MODEL // 2
HY_ORDER = 2
HY_SHORT = 3
HY_EMB = 33
HY_BANDS = (HY_EMB - 1) // 2
HY_FILTER_HIDDEN_LAYERS = 2
HY_DIRS = 2
HY_FAST_DECAY = 0.3
HY_SLOW_DECAY = 1.5
HY_DECAY_TARGET = 1e-2
HY_WIDTH = (HY_ORDER + 1) * HY_CH
WIN_HEADS = (D_MODEL - HY_CH) // HEAD_DIM
WIN_KV_HEADS = WIN_HEADS // 4
WIN_GROUP = WIN_HEADS // WIN_KV_HEADS
WIN_Q = WIN_HEADS * HEAD_DIM
WIN_KV = WIN_KV_HEADS * HEAD_DIM
DIFF_DIM = 128
DIFF_HEADS = D_MODEL // (2 * DIFF_DIM)
DIFF_Q = DIFF_HEADS * 2 * DIFF_DIM
N_EXPERTS = 8
TOP_K = 2

VMEM_LIMIT_BYTES = 56 * 1024 * 1024
ROW_TILE = 768
NORM_TILE = 256
MOE_TILE = 512

HI = lax.Precision.HIGHEST


def _cparams(sem):
    return pltpu.CompilerParams(dimension_semantics=sem, vmem_limit_bytes=VMEM_LIMIT_BYTES)


def _norm_mod_kernel(x_ref, g_ref, shift_ref, scale_ref, o_ref):
    x = x_ref[...]
    y = x * lax.rsqrt(jnp.mean(x * x, axis=-1, keepdims=True) + EPS)
    y = y * g_ref[...]
    o_ref[...] = (y * (1.0 + scale_ref[...]) + shift_ref[...]).astype(o_ref.dtype)


def norm_mod(x, g, shift, scale):
    B, T, D = x.shape
    nt = T // NORM_TILE
    mod_spec = pl.BlockSpec((None, None, 1, D), lambda b, t: (b, jnp.minimum(t, 1), 0, 0))
    return pl.pallas_call(
        _norm_mod_kernel,
        grid=(B, nt),
        in_specs=[
            pl.BlockSpec((None, NORM_TILE, D), lambda b, t: (b, t, 0)),
            pl.BlockSpec((1, D), lambda b, t: (0, 0)),
            mod_spec, mod_spec,
        ],
        out_specs=pl.BlockSpec((None, NORM_TILE, D), lambda b, t: (b, t, 0)),
        out_shape=jax.ShapeDtypeStruct((B, T, D), jnp.bfloat16),
        compiler_params=_cparams(("parallel", "parallel")),
        name="norm_mod",
    )(x, g.reshape(1, D), shift.reshape(B, 2, 1, D), scale.reshape(B, 2, 1, D))


def _mm_kernel(a_ref, b_ref, o_ref, acc_ref, *, nk):
    k = pl.program_id(2)
    part = jnp.dot(a_ref[...], b_ref[...], preferred_element_type=jnp.float32)
    if nk == 1:
        o_ref[...] = part.astype(o_ref.dtype)
        return

    @pl.when(k == 0)
    def _():
        acc_ref[...] = part

    @pl.when(k > 0)
    def _():
        acc_ref[...] += part

    @pl.when(k == nk - 1)
    def _():
        o_ref[...] = acc_ref[...].astype(o_ref.dtype)


def matmul(a, b, *, out_dtype=jnp.bfloat16, tm=ROW_TILE, tn=512, tk=None):
    M, K = a.shape
    _, N = b.shape
    tk = K if tk is None else tk
    nk = K // tk
    return pl.pallas_call(
        functools.partial(_mm_kernel, nk=nk),
        grid=(M // tm, N // tn, nk),
        in_specs=[pl.BlockSpec((tm, tk), lambda i, j, k: (i, k)),
                  pl.BlockSpec((tk, tn), lambda i, j, k: (k, j))],
        out_specs=pl.BlockSpec((tm, tn), lambda i, j, k: (i, j)),
        out_shape=jax.ShapeDtypeStruct((M, N), out_dtype),
        scratch_shapes=[pltpu.VMEM((tm, tn), jnp.float32)],
        compiler_params=_cparams(("parallel", "parallel", "arbitrary")),
        name="matmul",
    )(a, b)


def _mm_swiglu_kernel(a_ref, w1_ref, w3_ref, o_ref):
    a = a_ref[...]
    h1 = jnp.dot(a, w1_ref[...], preferred_element_type=jnp.float32)
    h3 = jnp.dot(a, w3_ref[...], preferred_element_type=jnp.float32)
    o_ref[...] = (h1 * jax.nn.sigmoid(h1) * h3).astype(o_ref.dtype)


def matmul_swiglu(a, w1, w3, *, tm=ROW_TILE, tn=512):
    M, K = a.shape
    _, N = w1.shape
    return pl.pallas_call(
        _mm_swiglu_kernel,
        grid=(M // tm, N // tn),
        in_specs=[pl.BlockSpec((tm, K), lambda i, j: (i, 0)),
                  pl.BlockSpec((K, tn), lambda i, j: (0, j)),
                  pl.BlockSpec((K, tn), lambda i, j: (0, j))],
        out_specs=pl.BlockSpec((tm, tn), lambda i, j: (i, j)),
        out_shape=jax.ShapeDtypeStruct((M, N), jnp.bfloat16),
        compiler_params=_cparams(("parallel", "parallel")),
        name="matmul_swiglu",
    )(a, w1, w3)


def _mm_resgate_kernel(a_ref, b_ref, res_ref, gate_ref, o_ref, acc_ref, *, nk, tm, tiles_per_batch, n_ctx):
    i = pl.program_id(0)
    k = pl.program_id(2)
    part = jnp.dot(a_ref[...], b_ref[...], preferred_element_type=jnp.float32)

    def finish(acc):
        row = (i % tiles_per_batch) * tm + lax.broadcasted_iota(jnp.int32, (tm, 1), 0)
        gate = jnp.where(row < n_ctx, gate_ref[0:1, :], gate_ref[1:2, :])
        o_ref[...] = res_ref[...] + gate * acc

    if nk == 1:
        finish(part)
        return

    @pl.when(k == 0)
    def _():
        acc_ref[...] = part

    @pl.when(k > 0)
    def _():
        acc_ref[...] += part

    @pl.when(k == nk - 1)
    def _():
        finish(acc_ref[...])


def matmul_resgate(a, b, res, gate, *, n_ctx, tm=ROW_TILE, tn=512, tk=None):
    B, T, N = res.shape
    M, K = a.shape
    tk = K if tk is None else tk
    nk = K // tk
    tpb = T // tm
    kern = functools.partial(_mm_resgate_kernel, nk=nk, tm=tm, tiles_per_batch=tpb, n_ctx=n_ctx)
    out = pl.pallas_call(
        kern,
        grid=(M // tm, N // tn, nk),
        in_specs=[pl.BlockSpec((tm, tk), lambda i, j, k: (i, k)),
                  pl.BlockSpec((tk, tn), lambda i, j, k: (k, j)),
                  pl.BlockSpec((tm, tn), lambda i, j, k: (i, j)),
                  pl.BlockSpec((None, 2, tn), lambda i, j, k: (i // tpb, 0, j))],
        out_specs=pl.BlockSpec((tm, tn), lambda i, j, k: (i, j)),
        out_shape=jax.ShapeDtypeStruct((M, N), jnp.float32),
        scratch_shapes=[pltpu.VMEM((tm, tn), jnp.float32)],
        compiler_params=_cparams(("parallel", "parallel", "arbitrary")),
        name="matmul_resgate",
    )(a, b, res.reshape(M, N), gate)
    return out.reshape(B, T, N)


def _window_attn_kernel(sink_ref, q_ref, kc_ref, kp_ref, ko_ref, kn_ref, vc_ref, vp_ref, vo_ref, vn_ref, o_ref,
                        *, n_lat, n_ctx):
    n = pl.program_id(1)
    h = pl.program_id(2)
    G = WIN_GROUP
    q = q_ref[...]
    qs = jnp.concatenate([q[:, g * HEAD_DIM:(g + 1) * HEAD_DIM] for g in range(G)], axis=0)
    k = jnp.concatenate([kc_ref[...], kp_ref[...], ko_ref[...], kn_ref[...]], axis=0)
    v = jnp.concatenate([vc_ref[...], vp_ref[...], vo_ref[...], vn_ref[...]], axis=0)
    s = lax.dot_general(qs, k, (((1,), (1,)), ((), ())), preferred_element_type=jnp.float32)
    nk = n_ctx + 3 * BLOCK
    col = lax.broadcasted_iota(jnp.int32, (G * BLOCK, nk), 1)
    qpos = n * BLOCK + lax.broadcasted_iota(jnp.int32, (G * BLOCK, nk), 0) % BLOCK
    kpos = (n - 1) * BLOCK + (col - n_ctx)
    valid = (col < n_ctx) | ((jnp.abs(qpos - kpos) <= WINDOW) & (kpos >= 0) & (kpos < n_lat))
    s = jnp.where(valid, s, NEG_INF)
    sink = jnp.concatenate(
        [jnp.full((BLOCK, 1), sink_ref[h * G + g], jnp.float32) for g in range(G)], axis=0)
    m = jnp.maximum(jnp.max(s, axis=-1, keepdims=True), sink)
    e = jnp.exp(s - m)
    denom = jnp.sum(e, axis=-1, keepdims=True) + jnp.exp(sink - m)
    p = (e / denom).astype(v.dtype)
    o = jnp.dot(p, v, preferred_element_type=jnp.float32)
    for g in range(G):
        o_ref[:, g * HEAD_DIM:(g + 1) * HEAD_DIM] = o[g * BLOCK:(g + 1) * BLOCK, :].astype(o_ref.dtype)


def window_attention(q, k, v, sink, *, n_ctx):
    B, T, _ = q.shape
    L = T - n_ctx
    nb = L // BLOCK
    off = n_ctx // BLOCK
    kv_blk = lambda f: pl.BlockSpec((None, BLOCK, HEAD_DIM), f)
    ctx_spec = pl.BlockSpec((None, n_ctx, HEAD_DIM), lambda b, n, h: (b, 0, h))
    prev_spec = kv_blk(lambda b, n, h: (b, jnp.maximum(n - 1, 0) + off, h))
    own_spec = kv_blk(lambda b, n, h: (b, n + off, h))
    next_spec = kv_blk(lambda b, n, h: (b, jnp.minimum(n + 1, nb - 1) + off, h))
    return pl.pallas_call(
        functools.partial(_window_attn_kernel, n_lat=L, n_ctx=n_ctx),
        grid=(B, nb, WIN_KV_HEADS),
        in_specs=[pl.BlockSpec(memory_space=pltpu.SMEM),
                  pl.BlockSpec((None, BLOCK, WIN_GROUP * HEAD_DIM), lambda b, n, h: (b, n + off, h)),
                  ctx_spec, prev_spec, own_spec, next_spec,
                  ctx_spec, prev_spec, own_spec, next_spec],
        out_specs=pl.BlockSpec((None, BLOCK, WIN_GROUP * HEAD_DIM), lambda b, n, h: (b, n, h)),
        out_shape=jax.ShapeDtypeStruct((B, L, WIN_Q), jnp.bfloat16),
        compiler_params=_cparams(("parallel", "parallel", "parallel")),
        name="window_attention",
    )(sink.astype(jnp.float32), q, k, k, k, k, v, v, v, v)


def _diff_attn_kernel(lam_ref, q_ref, k_ref, v_ref, g_ref, o_ref, m_ref, l_ref, acc_ref,
                      *, n_keys, tk, out_scale):
    m_ref[...] = jnp.full(m_ref.shape, NEG_INF, jnp.float32)
    l_ref[...] = jnp.zeros(l_ref.shape, jnp.float32)
    acc_ref[...] = jnp.zeros(acc_ref.shape, jnp.float32)

    def step(j, carry):
        ks = pl.multiple_of(j * tk, tk)
        v = v_ref[pl.ds(ks, tk), :]
        for sub in range(2):
            q = q_ref[:, sub * DIFF_DIM:(sub + 1) * DIFF_DIM]
            k = k_ref[pl.ds(ks, tk), sub * DIFF_DIM:(sub + 1) * DIFF_DIM]
            s = lax.dot_general(q, k, (((1,), (1,)), ((), ())), preferred_element_type=jnp.float32)
            m_old = m_ref[sub]
            m_new = jnp.maximum(m_old, jnp.max(s, axis=-1, keepdims=True))
            alpha = jnp.exp(m_old - m_new)
            p = jnp.exp(s - m_new)
            l_ref[sub] = alpha * l_ref[sub] + jnp.sum(p, axis=-1, keepdims=True)
            acc_ref[sub] = alpha * acc_ref[sub] + jnp.dot(p.astype(v.dtype), v, preferred_element_type=jnp.float32)
            m_ref[sub] = m_new
        return carry

    lax.fori_loop(0, n_keys // tk, step, 0)
    lam = lam_ref[0]
    o = acc_ref[0] / l_ref[0] - lam * (acc_ref[1] / l_ref[1])
    o = o * lax.rsqrt(jnp.mean(o * o, axis=-1, keepdims=True) + EPS)
    o_ref[...] = (o * g_ref[...] * out_scale).astype(o_ref.dtype)


def diff_attention(q, k, v, lam, subln_g, *, n_keys, tq, tk, out_scale):
    B, Lq, _ = q.shape
    W = 2 * DIFF_DIM
    return pl.pallas_call(
        functools.partial(_diff_attn_kernel, n_keys=n_keys, tk=tk, out_scale=out_scale),
        grid=(B, DIFF_HEADS, Lq // tq),
        in_specs=[pl.BlockSpec(memory_space=pltpu.SMEM),
                  pl.BlockSpec((None, tq, W), lambda b, h, i: (b, i, h)),
                  pl.BlockSpec((None, n_keys, W), lambda b, h, i: (b, 0, h)),
                  pl.BlockSpec((None, n_keys, W), lambda b, h, i: (b, 0, h)),
                  pl.BlockSpec((1, W), lambda b, h, i: (0, 0))],
        out_specs=pl.BlockSpec((None, tq, W), lambda b, h, i: (b, i, h)),
        scratch_shapes=[pltpu.VMEM((2, tq, 1), jnp.float32),
                        pltpu.VMEM((2, tq, 1), jnp.float32),
                        pltpu.VMEM((2, tq, W), jnp.float32)],
        out_shape=jax.ShapeDtypeStruct((B, Lq, DIFF_Q), jnp.bfloat16),
        compiler_params=_cparams(("parallel", "parallel", "arbitrary")),
        name="diff_attention",
    )(lam.reshape(1).astype(jnp.float32), q, k, v, subln_g.reshape(1, W).astype(jnp.float32))


def _moe_up_kernel(te_ref, tv_ref, x_ref, w1_ref, w3_ref, o_ref):
    i = pl.program_id(1)

    @pl.when(tv_ref[i] > 0)
    def _():
        x = x_ref[...]
        h1 = jnp.dot(x, w1_ref[...], preferred_element_type=jnp.float32)
        h3 = jnp.dot(x, w3_ref[...], preferred_element_type=jnp.float32)
        o_ref[...] = (h1 * jax.nn.sigmoid(h1) * h3).astype(o_ref.dtype)

    @pl.when(tv_ref[i] == 0)
    def _():
        o_ref[...] = jnp.zeros(o_ref.shape, o_ref.dtype)


def _moe_down_kernel(te_ref, tv_ref, g_ref, w2_ref, rg_ref, o_ref):
    i = pl.program_id(1)

    @pl.when(tv_ref[i] > 0)
    def _():
        y = jnp.dot(g_ref[...], w2_ref[...], preferred_element_type=jnp.float32)
        o_ref[...] = (rg_ref[...] * y).astype(o_ref.dtype)

    @pl.when(tv_ref[i] == 0)
    def _():
        o_ref[...] = jnp.zeros(o_ref.shape, o_ref.dtype)


def moe_experts(xs, w1, w3, w2, tile_expert, tile_valid, row_gate, *, tn_up=768, tn_down=1024):
    P, D = xs.shape
    F = w1.shape[2]
    tm = MOE_TILE
    up = pl.pallas_call(
        _moe_up_kernel,
        grid_spec=pltpu.PrefetchScalarGridSpec(
            num_scalar_prefetch=2,
            grid=(F // tn_up, P // tm),
            in_specs=[pl.BlockSpec((tm, D), lambda j, i, te, tv: (i, 0)),
                      pl.BlockSpec((None, D, tn_up), lambda j, i, te, tv: (te[i], 0, j)),
                      pl.BlockSpec((None, D, tn_up), lambda j, i, te, tv: (te[i], 0, j))],
            out_specs=pl.BlockSpec((tm, tn_up), lambda j, i, te, tv: (i, j)),
        ),
        out_shape=jax.ShapeDtypeStruct((P, F), jnp.bfloat16),
        compiler_params=_cparams(("parallel", "arbitrary")),
        name="moe_up",
    )(tile_expert, tile_valid, xs, w1, w3)
    return pl.pallas_call(
        _moe_down_kernel,
        grid_spec=pltpu.PrefetchScalarGridSpec(
            num_scalar_prefetch=2,
            grid=(D // tn_down, P // tm),
            in_specs=[pl.BlockSpec((tm, F), lambda j, i, te, tv: (i, 0)),
                      pl.BlockSpec((None, F, tn_down), lambda j, i, te, tv: (te[i], 0, j)),
                      pl.BlockSpec((tm, 1), lambda j, i, te, tv: (i, 0))],
            out_specs=pl.BlockSpec((tm, tn_down), lambda j, i, te, tv: (i, j)),
        ),
        out_shape=jax.ShapeDtypeStruct((P, D), jnp.bfloat16),
        compiler_params=_cparams(("parallel", "arbitrary")),
        name="moe_down",
    )(tile_expert, tile_valid, up, w2, row_gate)


def moe_layer(h2, router_w, router_b, w1, w3, w2):
    N, D = h2.shape
    tm = MOE_TILE
    logits = jnp.dot(h2.astype(jnp.float32), router_w, precision=HI) + router_b
    top_v, top_i = lax.top_k(logits, TOP_K)
    gates = jax.nn.softmax(top_v, axis=-1)
    A = N * TOP_K
    e_flat = top_i.reshape(A).astype(jnp.int32)
    order = jnp.argsort(e_flat, stable=True)
    e_sorted = e_flat[order]
    counts = jnp.sum(e_flat[:, None] == jnp.arange(N_EXPERTS, dtype=jnp.int32)[None, :], axis=0).astype(jnp.int32)
    padded = ((counts + tm - 1) // tm) * tm
    start_unpadded = jnp.cumsum(counts) - counts
    start_padded = jnp.cumsum(padded) - padded
    dest_sorted = start_padded[e_sorted] + (jnp.arange(A, dtype=jnp.int32) - start_unpadded[e_sorted])
    P = A + N_EXPERTS * tm
    row_token = jnp.zeros((P,), jnp.int32).at[dest_sorted].set((order // TOP_K).astype(jnp.int32))
    row_gate = jnp.zeros((P,), jnp.float32).at[dest_sorted].set(gates.reshape(A)[order])
    pos = jnp.zeros((A,), jnp.int32).at[order].set(dest_sorted).reshape(N, TOP_K)
    tile_start = jnp.arange(P // tm, dtype=jnp.int32) * tm
    ends = jnp.cumsum(padded)
    tile_expert = jnp.minimum(jnp.sum(tile_start[:, None] >= ends[None, :], axis=1), N_EXPERTS - 1).astype(jnp.int32)
    tile_valid = (tile_start < ends[-1]).astype(jnp.int32)
    xs = jnp.take(h2, row_token, axis=0)
    y = moe_experts(xs, w1, w3, w2, tile_expert, tile_valid, row_gate.reshape(P, 1))
    return (jnp.take(y, pos[:, 0], axis=0).astype(jnp.float32)
            + jnp.take(y, pos[:, 1], axis=0).astype(jnp.float32))


def _rope_tables(L, n_ctx):
    rows = jnp.repeat(jnp.arange(L // GRID_W), GRID_W)
    cols = jnp.tile(jnp.arange(GRID_W), L // GRID_W)
    quarter = HEAD_DIM // 4
    inv = ROPE_BASE ** (-jnp.arange(quarter, dtype=jnp.float32) / quarter)
    ar = rows.astype(jnp.float32)[:, None] * inv
    ac = cols.astype(jnp.float32)[:, None] * inv
    ang = jnp.concatenate([ar, ar, ac, ac], axis=-1)
    cos = jnp.concatenate([jnp.ones((n_ctx, HEAD_DIM), jnp.float32), jnp.cos(ang)], axis=0)
    sin = jnp.concatenate([jnp.zeros((n_ctx, HEAD_DIM), jnp.float32), jnp.sin(ang)], axis=0)
    return cos, sin


def _head_norm_rope(t, g, cos, sin, scale):
    B, T, W = t.shape
    x = t.astype(jnp.float32).reshape(B, T, W // HEAD_DIM, HEAD_DIM)
    x = x * lax.rsqrt(jnp.mean(x * x, axis=-1, keepdims=True) + EPS) * g.astype(jnp.float32)
    x1, x2, x3, x4 = jnp.split(x, 4, axis=-1)
    rot = jnp.concatenate([-x2, x1, -x4, x3], axis=-1)
    x = x * cos[None, :, None, :] + rot * sin[None, :, None, :]
    return (x * scale).astype(jnp.bfloat16).reshape(B, T, W)


def _short_conv(u, w, b):
    L = u.shape[1]
    pad = HY_SHORT // 2
    up = jnp.pad(u, ((0, 0), (pad, pad), (0, 0)))
    y = b + up[:, 0:L] * w[0]
    for j in range(1, HY_SHORT):
        y = y + up[:, j:j + L] * w[j]
    return y


def _implicit_filters(L, w_in, w_hid, b, freq, w_out):
    f32 = jnp.float32
    t = jnp.linspace(0.0, 1.0, L, dtype=f32)[:, None]
    w = (2.0 * math.pi / L) * jnp.arange(L, dtype=f32)[:, None]
    f = jnp.linspace(1e-4, HY_BANDS - 1, HY_BANDS, dtype=f32)[None, :]
    z = jnp.concatenate([t, jnp.cos(f * w), -jnp.sin(f * w)], axis=-1)
    h = jnp.sin(freq[0] * (jnp.dot(z, w_in, precision=HI) + b[0]))
    for n in range(HY_FILTER_HIDDEN_LAYERS):
        h = jnp.sin(freq[n + 1] * (jnp.dot(h, w_hid[n], precision=HI) + b[n + 1]))
    h = jnp.dot(h, w_out, precision=HI).reshape(L, HY_DIRS, HY_ORDER, HY_CH)
    max_decay = math.log(HY_DECAY_TARGET) / HY_FAST_DECAY
    min_decay = math.log(HY_DECAY_TARGET) / HY_SLOW_DECAY
    deltas = jnp.abs(jnp.linspace(min_decay, max_decay, HY_ORDER * HY_CH, dtype=f32)).reshape(HY_ORDER, HY_CH)
    decay = jnp.exp(-t[:, :, None] * deltas)
    h = h * decay[:, None]
    fwd = h[:, 0]
    bwd = h[:0:-1, 1]
    return jnp.concatenate([fwd, jnp.zeros_like(fwd[:1]), bwd], axis=0)


def _hyena(u, conv_w, conv_b, f_w_in, f_w_hid, f_b, f_freq, f_w_out, skip):
    L = u.shape[1]
    u = _short_conv(u, conv_w, conv_b)
    v, *gates = jnp.split(u, HY_ORDER + 1, axis=-1)
    filt_f = jnp.fft.rfft(_implicit_filters(L, f_w_in, f_w_hid, f_b, f_freq, f_w_out), axis=0)
    z = v
    for n, g in enumerate(gates):
        conv = jnp.fft.irfft(jnp.fft.rfft(z, n=2 * L, axis=1) * filt_f[:, n], n=2 * L, axis=1)[:, :L]
        z = g * (conv + z * skip[n])
    return z


def _context_sink_attention(q, k, v, sink):
    B, Lc, _ = q.shape
    qg = q.astype(jnp.float32).reshape(B, Lc, WIN_KV_HEADS, WIN_GROUP, HEAD_DIM)
    kh = k.astype(jnp.float32).reshape(B, Lc, WIN_KV_HEADS, HEAD_DIM)
    vh = v.astype(jnp.float32).reshape(B, Lc, WIN_KV_HEADS, HEAD_DIM)
    s = jnp.einsum('bqhgd,bkhd->bhgqk', qg, kh, precision=HI)
    sink_b = jnp.broadcast_to(sink.astype(jnp.float32).reshape(WIN_KV_HEADS, WIN_GROUP, 1, 1),
                              (B, WIN_KV_HEADS, WIN_GROUP, Lc, 1))
    p = jax.nn.softmax(jnp.concatenate([sink_b, s], axis=-1), axis=-1)[..., 1:]
    o = jnp.einsum('bhgqk,bkhd->bqhgd', p, vh, precision=HI)
    return o.reshape(B, Lc, WIN_Q)


def _ada_modulation(cv, down, up, b):
    m = jnp.dot(jnp.dot(jax.nn.silu(cv), down, precision=HI), up, precision=HI) + b
    return m.reshape(m.shape[:-1] + (N_MOD, m.shape[-1] // N_MOD))


def kernel(x, c, ctx, c_ctx, norm_g, ada_down, ada_up, ada_b, ev_w_in, ev_conv_w, ev_conv_b, ev_filt_w_in, ev_filt_w_hid, ev_filt_b, ev_filt_freq, ev_filt_w_out, ev_hy_skip, ev_qk_g, ev_sink, ev_w_out, ev_ffn_w1, ev_ffn_w3, ev_ffn_w2, od_w_qkv, od_qk_g, od_lambda, od_subln_g, od_w_out, od_router_w, od_router_b, od_moe_w1, od_moe_w3, od_moe_w2):
    B, L, D = x.shape
    Lc = ctx.shape[1]
    T = Lc + L
    bf16 = jnp.bfloat16
    cos, sin = _rope_tables(L, Lc)
    X = jnp.concatenate([ctx, x], axis=1)
    qk_scale = HEAD_DIM ** -0.5

    for i in range(DEPTH):
        j = i // 2
        m_l = _ada_modulation(c, ada_down[i], ada_up[i], ada_b[i])
        m_c = _ada_modulation(c_ctx, ada_down[i], ada_up[i], ada_b[i])
        mods = jnp.stack([jnp.broadcast_to(m_c[None], (B, N_MOD, D)), m_l], axis=1)

        h = norm_mod(X, norm_g[i, 0], mods[:, :, 0], mods[:, :, 1])
        hf = h.reshape(B * T, D)
        if i % 2 == 0:
            proj = matmul(hf, ev_w_in[j].astype(bf16)).reshape(B, T, -1)
            u = proj[..., :HY_WIDTH].astype(jnp.float32)
            q = proj[..., HY_WIDTH:HY_WIDTH + WIN_Q]
            k = proj[..., HY_WIDTH + WIN_Q:HY_WIDTH + WIN_Q + WIN_KV]
            v = proj[..., HY_WIDTH + WIN_Q + WIN_KV:]
            q = _head_norm_rope(q, ev_qk_g[j, 0], cos, sin, qk_scale)
            k = _head_norm_rope(k, ev_qk_g[j, 1], cos, sin, 1.0)
            hy_args = (ev_conv_w[j], ev_conv_b[j], ev_filt_w_in[j], ev_filt_w_hid[j], ev_filt_b[j],
                       ev_filt_freq[j], ev_filt_w_out[j], ev_hy_skip[j])
            hy_l = _hyena(u[:, Lc:], *hy_args)
            hy_c = _hyena(u[:, :Lc], *hy_args)
            at_l = window_attention(q, k, v, ev_sink[j], n_ctx=Lc)
            at_c = _context_sink_attention(q[:, :Lc], k[:, :Lc], v[:, :Lc], ev_sink[j])
            mix = jnp.concatenate(
                [jnp.concatenate([hy_c.astype(bf16), at_c.astype(bf16)], axis=-1),
                 jnp.concatenate([hy_l.astype(bf16), at_l], axis=-1)], axis=1)
            w_out = ev_w_out[j]
        else:
            lam_init = 0.8 - 0.6 * math.exp(-0.3 * i)
            qkv = matmul(hf, od_w_qkv[j].astype(bf16)).reshape(B, T, -1)
            q = _head_norm_rope(qkv[..., :DIFF_Q], od_qk_g[j, 0], cos, sin, DIFF_DIM ** -0.5)
            k = _head_norm_rope(qkv[..., DIFF_Q:2 * DIFF_Q], od_qk_g[j, 1], cos, sin, 1.0)
            v = qkv[..., 2 * DIFF_Q:]
            lp = od_lambda[j].astype(jnp.float32)
            lam = jnp.exp(jnp.sum(lp[0] * lp[1])) - jnp.exp(jnp.sum(lp[2] * lp[3])) + lam_init
            o_l = diff_attention(q[:, Lc:], k, v, lam, od_subln_g[j], n_keys=T, tq=512, tk=768,
                                 out_scale=1.0 - lam_init)
            o_c = diff_attention(q[:, :Lc], k, v, lam, od_subln_g[j], n_keys=Lc, tq=Lc, tk=Lc,
                                 out_scale=1.0 - lam_init)
            mix = jnp.concatenate([o_c, o_l], axis=1)
            w_out = od_w_out[j]

        X = matmul_resgate(mix.reshape(B * T, D), w_out.astype(bf16), X, mods[:, :, 2], n_ctx=Lc)
        h2 = norm_mod(X, norm_g[i, 1], mods[:, :, 3], mods[:, :, 4]).reshape(B * T, D)
        if i % 2 == 0:
            gact = matmul_swiglu(h2, ev_ffn_w1[j].astype(bf16), ev_ffn_w3[j].astype(bf16))
            X = matmul_resgate(gact, ev_ffn_w2[j].astype(bf16), X, mods[:, :, 5], n_ctx=Lc, tk=2048)
        else:
            f = moe_layer(h2, od_router_w[j], od_router_b[j], od_moe_w1[j].astype(bf16),
                          od_moe_w3[j].astype(bf16), od_moe_w2[j].astype(bf16))
            X = X + mods[:, :, 5][:, jnp.where(jnp.arange(T) < Lc, 0, 1)] * f.reshape(B, T, D)
    return X[:, Lc:]
```

```python
import functools
import math

import jax
import jax.numpy as jnp
from jax import lax
from jax.experimental import pallas as pl
from jax.experimental.pallas import tpu as pltpu

D_MODEL = 4096
DEPTH = 4
GRID_W = 64
N_MOD = 6
EPS = 1e-6
NEG_INF = -1e30
HEAD_DIM = 128
ROPE_BASE = 10000.0
BLOCK = 128
WINDOW = 128
HY_CH = D_MODEL // 2
HY_ORDER = 2
HY_SHORT = 3
HY_EMB = 33
HY_BANDS = (HY_EMB - 1) // 2
HY_FILTER_HIDDEN_LAYERS = 2
HY_DIRS = 2
HY_FAST_DECAY = 0.3
HY_SLOW_DECAY = 1.5
HY_DECAY_TARGET = 1e-2
HY_WIDTH = (HY_ORDER + 1) * HY_CH
WIN_HEADS = (D_MODEL - HY_CH) // HEAD_DIM
WIN_KV_HEADS = WIN_HEADS // 4
WIN_GROUP = WIN_HEADS // WIN_KV_HEADS
WIN_Q = WIN_HEADS * HEAD_DIM
WIN_KV = WIN_KV_HEADS * HEAD_DIM
DIFF_DIM = 128
DIFF_HEADS = D_MODEL // (2 * DIFF_DIM)
DIFF_Q = DIFF_HEADS * 2 * DIFF_DIM
N_EXPERTS = 8
TOP_K = 2

VMEM_LIMIT_BYTES = 56 * 1024 * 1024
ROW_TILE = 768
NORM_TILE = 256
MOE_TILE = 512

HI = lax.Precision.HIGHEST


def _cparams(sem):
    return pltpu.CompilerParams(dimension_semantics=sem, vmem_limit_bytes=VMEM_LIMIT_BYTES)


def _norm_mod_kernel(x_ref, g_ref, shift_ref, scale_ref, o_ref):
    x = x_ref[...]
    y = x * lax.rsqrt(jnp.mean(x * x, axis=-1, keepdims=True) + EPS)
    y = y * g_ref[...]
    o_ref[...] = (y * (1.0 + scale_ref[...]) + shift_ref[...]).astype(o_ref.dtype)


def norm_mod(x, g, shift, scale):
    B, T, D = x.shape
    nt = T // NORM_TILE
    mod_spec = pl.BlockSpec((None, None, 1, D), lambda b, t: (b, t // (nt - 1), 0, 0))
    return pl.pallas_call(
        _norm_mod_kernel,
        grid=(B, nt),
        in_specs=[
            pl.BlockSpec((None, NORM_TILE, D), lambda b, t: (b, t, 0)),
            pl.BlockSpec((1, D), lambda b, t: (0, 0)),
            mod_spec, mod_spec,
        ],
        out_specs=pl.BlockSpec((None, NORM_TILE, D), lambda b, t: (b, t, 0)),
        out_shape=jax.ShapeDtypeStruct((B, T, D), jnp.bfloat16),
        compiler_params=_cparams(("parallel", "parallel")),
        name="norm_mod",
    )(x, g.reshape(1, D), shift.reshape(B, 2, 1, D), scale.reshape(B, 2, 1, D))


def _mm_kernel(a_ref, b_ref, o_ref, acc_ref, *, nk):
    k = pl.program_id(2)
    part = jnp.dot(a_ref[...], b_ref[...], preferred_element_type=jnp.float32)
    if nk == 1:
        o_ref[...] = part.astype(o_ref.dtype)
        return

    @pl.when(k == 0)
    def _():
        acc_ref[...] = part

    @pl.when(k > 0)
    def _():
        acc_ref[...] += part

    @pl.when(k == nk - 1)
    def _():
        o_ref[...] = acc_ref[...].astype(o_ref.dtype)


def matmul(a, b, *, out_dtype=jnp.bfloat16, tm=ROW_TILE, tn=512, tk=None):
    M, K = a.shape
    _, N = b.shape
    tk = K if tk is None else tk
    nk = K // tk
    return pl.pallas_call(
        functools.partial(_mm_kernel, nk=nk),
        grid=(M // tm, N // tn, nk),
        in_specs=[pl.BlockSpec((tm, tk), lambda i, j, k: (i, k)),
                  pl.BlockSpec((tk, tn), lambda i, j, k: (k, j))],
        out_specs=pl.BlockSpec((tm, tn), lambda i, j, k: (i, j)),
        out_shape=jax.ShapeDtypeStruct((M, N), out_dtype),
        scratch_shapes=[pltpu.VMEM((tm, tn), jnp.float32)],
        compiler_params=_cparams(("parallel", "parallel", "arbitrary")),
        name="matmul",
    )(a, b)


def _mm_swiglu_kernel(a_ref, w1_ref, w3_ref, o_ref):
    a = a_ref[...]
    h1 = jnp.dot(a, w1_ref[...], preferred_element_type=jnp.float32)
    h3 = jnp.dot(a, w3_ref[...], preferred_element_type=jnp.float32)
    o_ref[...] = (h1 * jax.nn.sigmoid(h1) * h3).astype(o_ref.dtype)


def matmul_swiglu(a, w1, w3, *, tm=ROW_TILE, tn=512):
    M, K = a.shape
    _, N = w1.shape
    return pl.pallas_call(
        _mm_swiglu_kernel,
        grid=(M // tm, N // tn),
        in_specs=[pl.BlockSpec((tm, K), lambda i, j: (i, 0)),
                  pl.BlockSpec((K, tn), lambda i, j: (0, j)),
                  pl.BlockSpec((K, tn), lambda i, j: (0, j))],
        out_specs=pl.BlockSpec((tm, tn), lambda i, j: (i, j)),
        out_shape=jax.ShapeDtypeStruct((M, N), jnp.bfloat16),
        compiler_params=_cparams(("parallel", "parallel")),
        name="matmul_swiglu",
    )(a, w1, w3)


def _mm_resgate_kernel(a_ref, b_ref, res_ref, gate_ref, o_ref, acc_ref, *, nk, tm, tiles_per_batch, n_lat):
    i = pl.program_id(0)
    k = pl.program_id(2)
    part = jnp.dot(a_ref[...], b_ref[...], preferred_element_type=jnp.float32)

    def finish(acc):
        row = (i % tiles_per_batch) * tm + lax.broadcasted_iota(jnp.int32, (tm, 1), 0)
        gate = jnp.where(row < n_lat, gate_ref[0:1, :], gate_ref[1:2, :])
        o_ref[...] = res_ref[...] + gate * acc

    if nk == 1:
        finish(part)
        return

    @pl.when(k == 0)
    def _():
        acc_ref[...] = part

    @pl.when(k > 0)
    def _():
        acc_ref[...] += part

    @pl.when(k == nk - 1)
    def _():
        finish(acc_ref[...])


def matmul_resgate(a, b, res, gate, *, n_lat, tm=ROW_TILE, tn=512, tk=None):
    B, T, N = res.shape
    M, K = a.shape
    tk = K if tk is None else tk
    nk = K // tk
    tpb = T // tm
    kern = functools.partial(_mm_resgate_kernel, nk=nk, tm=tm, tiles_per_batch=tpb, n_lat=n_lat)
    out = pl.pallas_call(
        kern,
        grid=(M // tm, N // tn, nk),
        in_specs=[pl.BlockSpec((tm, tk), lambda i, j, k: (i, k)),
                  pl.BlockSpec((tk, tn), lambda i, j, k: (k, j)),
                  pl.BlockSpec((tm, tn), lambda i, j, k: (i, j)),
                  pl.BlockSpec((None, 2, tn), lambda i, j, k: (i // tpb, 0, j))],
        out_specs=pl.BlockSpec((tm, tn), lambda i, j, k: (i, j)),
        out_shape=jax.ShapeDtypeStruct((M, N), jnp.float32),
        scratch_shapes=[pltpu.VMEM((tm, tn), jnp.float32)],
        compiler_params=_cparams(("parallel", "parallel", "arbitrary")),
        name="matmul_resgate",
    )(a, b, res.reshape(M, N), gate)
    return out.reshape(B, T, N)


def _window_attn_kernel(sink_ref, q_ref, kc_ref, kp_ref, ko_ref, kn_ref, vc_ref, vp_ref, vo_ref, vn_ref, o_ref,
                        *, n_lat, n_ctx):
    n = pl.program_id(1)
    h = pl.program_id(2)
    G = WIN_GROUP
    q = q_ref[...]
    qs = jnp.concatenate([q[:, g * HEAD_DIM:(g + 1) * HEAD_DIM] for g in range(G)], axis=0)
    k = jnp.concatenate([kc_ref[...], kp_ref[...], ko_ref[...], kn_ref[...]], axis=0)
    v = jnp.concatenate([vc_ref[...], vp_ref[...], vo_ref[...], vn_ref[...]], axis=0)
    s = lax.dot_general(qs, k, (((1,), (1,)), ((), ())), preferred_element_type=jnp.float32)
    nk = n_ctx + 3 * BLOCK
    col = lax.broadcasted_iota(jnp.int32, (G * BLOCK, nk), 1)
    qpos = n * BLOCK + lax.broadcasted_iota(jnp.int32, (G * BLOCK, nk), 0) % BLOCK
    kpos = (n - 1) * BLOCK + (col - n_ctx)
    valid = (col < n_ctx) | ((jnp.abs(qpos - kpos) <= WINDOW) & (kpos >= 0) & (kpos < n_lat))
    s = jnp.where(valid, s, NEG_INF)
    sink = jnp.concatenate(
        [jnp.full((BLOCK, 1), sink_ref[h * G + g], jnp.float32) for g in range(G)], axis=0)
    m = jnp.maximum(jnp.max(s, axis=-1, keepdims=True), sink)
    e = jnp.exp(s - m)
    denom = jnp.sum(e, axis=-1, keepdims=True) + jnp.exp(sink - m)
    p = (e / denom).astype(v.dtype)
    o = jnp.dot(p, v, preferred_element_type=jnp.float32)
    for g in range(G):
        o_ref[:, g * HEAD_DIM:(g + 1) * HEAD_DIM] = o[g * BLOCK:(g + 1) * BLOCK, :].astype(o_ref.dtype)


def window_attention(q, k, v, sink, *, n_ctx):
    B, T, _ = q.shape
    L = T - n_ctx
    nb = L // BLOCK
    kv_blk = lambda f: pl.BlockSpec((None, BLOCK, HEAD_DIM), f)
    ctx_spec = pl.BlockSpec((None, n_ctx, HEAD_DIM), lambda b, n, h: (b, L // n_ctx, h))
    prev_spec = kv_blk(lambda b, n, h: (b, jnp.maximum(n - 1, 0), h))
    own_spec = kv_blk(lambda b, n, h: (b, n, h))
    next_spec = kv_blk(lambda b, n, h: (b, jnp.minimum(n + 1, nb - 1), h))
    return pl.pallas_call(
        functools.partial(_window_attn_kernel, n_lat=L, n_ctx=n_ctx),
        grid=(B, nb, WIN_KV_HEADS),
        in_specs=[pl.BlockSpec(memory_space=pltpu.SMEM),
                  pl.BlockSpec((None, BLOCK, WIN_GROUP * HEAD_DIM), lambda b, n, h: (b, n, h)),
                  ctx_spec, prev_spec, own_spec, next_spec,
                  ctx_spec, prev_spec, own_spec, next_spec],
        out_specs=pl.BlockSpec((None, BLOCK, WIN_GROUP * HEAD_DIM), lambda b, n, h: (b, n, h)),
        out_shape=jax.ShapeDtypeStruct((B, L, WIN_Q), jnp.bfloat16),
        compiler_params=_cparams(("parallel", "parallel", "parallel")),
        name="window_attention",
    )(sink.astype(jnp.float32), q, k, k, k, k, v, v, v, v)


def _diff_attn_kernel(lam_ref, q_ref, k_ref, v_ref, g_ref, o_ref, m_ref, l_ref, acc_ref,
                      *, n_keys, tk, out_scale):
    m_ref[...] = jnp.full(m_ref.shape, NEG_INF, jnp.float32)
    l_ref[...] = jnp.zeros(l_ref.shape, jnp.float32)
    acc_ref[...] = jnp.zeros(acc_ref.shape, jnp.float32)

    def step(j, carry):
        ks = pl.multiple_of(j * tk, tk)
        v = v_ref[pl.ds(ks, tk), :]
        for sub in range(2):
            q = q_ref[:, sub * DIFF_DIM:(sub + 1) * DIFF_DIM]
            k = k_ref[pl.ds(ks, tk), sub * DIFF_DIM:(sub + 1) * DIFF_DIM]
            s = lax.dot_general(q, k, (((1,), (1,)), ((), ())), preferred_element_type=jnp.float32)
            m_old = m_ref[sub]
            m_new = jnp.maximum(m_old, jnp.max(s, axis=-1, keepdims=True))
            alpha = jnp.exp2(m_old - m_new)
            p = jnp.exp2(s - m_new)
            l_ref[sub] = alpha * l_ref[sub] + jnp.sum(p, axis=-1, keepdims=True)
            acc_ref[sub] = alpha * acc_ref[sub] + jnp.dot(p.astype(v.dtype), v, preferred_element_type=jnp.float32)
            m_ref[sub] = m_new
        return carry

    lax.fori_loop(0, n_keys // tk, step, 0)
    lam = lam_ref[0]
    o = acc_ref[0] / l_ref[0] - lam * (acc_ref[1] / l_ref[1])
    o = o * lax.rsqrt(jnp.mean(o * o, axis=-1, keepdims=True) + EPS)
    o_ref[...] = (o * g_ref[...] * out_scale).astype(o_ref.dtype)


def diff_attention(q, k, v, lam, subln_g, *, n_keys, key_blk, tq, tk, out_scale):
    B, Lq, _ = q.shape
    W = 2 * DIFF_DIM
    return pl.pallas_call(
        functools.partial(_diff_attn_kernel, n_keys=n_keys, tk=tk, out_scale=out_scale),
        grid=(B, DIFF_HEADS, Lq // tq),
        in_specs=[pl.BlockSpec(memory_space=pltpu.SMEM),
                  pl.BlockSpec((None, tq, W), lambda b, h, i: (b, i, h)),
                  pl.BlockSpec((None, n_keys, W), lambda b, h, i: (b, key_blk, h)),
                  pl.BlockSpec((None, n_keys, W), lambda b, h, i: (b, key_blk, h)),
                  pl.BlockSpec((1, W), lambda b, h, i: (0, 0))],
        out_specs=pl.BlockSpec((None, tq, W), lambda b, h, i: (b, i, h)),
        scratch_shapes=[pltpu.VMEM((2, tq, 1), jnp.float32),
                        pltpu.VMEM((2, tq, 1), jnp.float32),
                        pltpu.VMEM((2, tq, W), jnp.float32)],
        out_shape=jax.ShapeDtypeStruct((B, Lq, DIFF_Q), jnp.bfloat16),
        compiler_params=_cparams(("parallel", "parallel", "arbitrary")),
        name="diff_attention",
    )(lam.reshape(1).astype(jnp.float32), q, k, v, subln_g.reshape(1, W).astype(jnp.float32))


def _moe_up_kernel(te_ref, tv_ref, x_ref, w1_ref, w3_ref, o_ref):
    i = pl.program_id(1)

    @pl.when(tv_ref[i] > 0)
    def _():
        x = x_ref[...]
        h1 = jnp.dot(x, w1_ref[...], preferred_element_type=jnp.float32)
        h3 = jnp.dot(x, w3_ref[...], preferred_element_type=jnp.float32)
        o_ref[...] = (h1 * jax.nn.sigmoid(h1) * h3).astype(o_ref.dtype)

    @pl.when(tv_ref[i] == 0)
    def _():
        o_ref[...] = jnp.zeros(o_ref.shape, o_ref.dtype)


def _moe_down_kernel(te_ref, tv_ref, g_ref, w2_ref, rg_ref, o_ref):
    i = pl.program_id(1)

    @pl.when(tv_ref[i] > 0)
    def _():
        y = jnp.dot(g_ref[...], w2_ref[...], preferred_element_type=jnp.float32)
        o_ref[...] = (rg_ref[...] * y).astype(o_ref.dtype)

    @pl.when(tv_ref[i] == 0)
    def _():
        o_ref[...] = jnp.zeros(o_ref.shape, o_ref.dtype)


def moe_experts(xs, w1, w3, w2, tile_expert, tile_valid, row_gate, *, tn_up=768, tn_down=1024):
    P, D = xs.shape
    F = w1.shape[2]
    tm = MOE_TILE
    up = pl.pallas_call(
        _moe_up_kernel,
        grid_spec=pltpu.PrefetchScalarGridSpec(
            num_scalar_prefetch=2,
            grid=(F // tn_up, P // tm),
            in_specs=[pl.BlockSpec((tm, D), lambda j, i, te, tv: (i, 0)),
                      pl.BlockSpec((None, D, tn_up), lambda j, i, te, tv: (te[i], 0, j)),
                      pl.BlockSpec((None, D, tn_up), lambda j, i, te, tv: (te[i], 0, j))],
            out_specs=pl.BlockSpec((tm, tn_up), lambda j, i, te, tv: (i, j)),
        ),
        out_shape=jax.ShapeDtypeStruct((P, F), jnp.bfloat16),
        compiler_params=_cparams(("parallel", "arbitrary")),
        name="moe_up",
    )(tile_expert, tile_valid, xs, w1, w3)
    return pl.pallas_call(
        _moe_down_kernel,
        grid_spec=pltpu.PrefetchScalarGridSpec(
            num_scalar_prefetch=2,
            grid=(D // tn_down, P // tm),
            in_specs=[pl.BlockSpec((tm, F), lambda j, i, te, tv: (i, 0)),
                      pl.BlockSpec((None, F, tn_down), lambda j, i, te, tv: (te[i], 0, j)),
                      pl.BlockSpec((tm, 1), lambda j, i, te, tv: (i, 0))],
            out_specs=pl.BlockSpec((tm, tn_down), lambda j, i, te, tv: (i, j)),
        ),
        out_shape=jax.ShapeDtypeStruct((P, D), jnp.bfloat16),
        compiler_params=_cparams(("parallel", "arbitrary")),
        name="moe_down",
    )(tile_expert, tile_valid, up, w2, row_gate)


def moe_layer(h2, router_w, router_b, w1, w3, w2):
    N, D = h2.shape
    tm = MOE_TILE
    logits = jnp.dot(h2.astype(jnp.float32), router_w, precision=HI) + router_b
    top_v, top_i = lax.top_k(logits, TOP_K)
    gates = jax.nn.softmax(top_v, axis=-1)
    A = N * TOP_K
    e_flat = top_i.reshape(A).astype(jnp.int32)
    order = jnp.argsort(e_flat, stable=True)
    e_sorted = e_flat[order]
    counts = jnp.sum(e_flat[:, None] == jnp.arange(N_EXPERTS, dtype=jnp.int32)[None, :], axis=0).astype(jnp.int32)
    padded = ((counts + tm - 1) // tm) * tm
    start_unpadded = jnp.cumsum(counts) - counts
    start_padded = jnp.cumsum(padded) - padded
    dest_sorted = start_padded[e_sorted] + (jnp.arange(A, dtype=jnp.int32) - start_unpadded[e_sorted])
    P = A + N_EXPERTS * tm
    row_token = jnp.zeros((P,), jnp.int32).at[dest_sorted].set((order // TOP_K).astype(jnp.int32))
    row_gate = jnp.zeros((P,), jnp.float32).at[dest_sorted].set(gates.reshape(A)[order])
    pos = jnp.zeros((A,), jnp.int32).at[order].set(dest_sorted).reshape(N, TOP_K)
    tile_start = jnp.arange(P // tm, dtype=jnp.int32) * tm
    ends = jnp.cumsum(padded)
    tile_expert = jnp.minimum(jnp.sum(tile_start[:, None] >= ends[None, :], axis=1), N_EXPERTS - 1).astype(jnp.int32)
    tile_valid = (tile_start < ends[-1]).astype(jnp.int32)
    xs = jnp.take(h2, row_token, axis=0)
    y = moe_experts(xs, w1, w3, w2, tile_expert, tile_valid, row_gate.reshape(P, 1))
    return (jnp.take(y, pos[:, 0], axis=0).astype(jnp.float32)
            + jnp.take(y, pos[:, 1], axis=0).astype(jnp.float32))


FFT_R = 128
FFT_K1 = 72
Z_PITCH = FFT_R + 8
S1_PITCH = 2 * FFT_K1 + 8
S2_PITCH = 2 * FFT_R + 8
HY_LANES = 128
FFT_UNROLL = 8


def _dft_tables():
    R, K1 = FFT_R, FFT_K1
    N = R * R
    i32 = jnp.int32
    b = jnp.arange(R, dtype=i32)[:, None, None]
    k1 = jnp.arange(K1, dtype=i32)[None, :, None]
    a = jnp.arange(R, dtype=i32)[None, None, :]
    th = (2.0 * math.pi / N) * ((k1 * (R * a + b)) % N).astype(jnp.float32)
    f1 = jnp.concatenate([jnp.cos(th), -jnp.sin(th)], axis=1)
    w = jnp.where(jnp.arange(K1) > R // 2, 0.0, jnp.where((jnp.arange(K1) % (R // 2)) == 0, 1.0, 2.0)) / N
    the = jnp.swapaxes(th[:, :, :R // 2], 1, 2)
    e = jnp.concatenate([w * jnp.cos(the), -w * jnp.sin(the)], axis=2)
    k2 = jnp.arange(R, dtype=i32)
    ph = (2.0 * math.pi / R) * ((k2[:, None] * k2[None, :]) % R).astype(jnp.float32)
    c, s = jnp.cos(ph), jnp.sin(ph)
    g = jnp.block([[c, s], [-s, c]])
    ginv = jnp.block([[c, -s], [s, c]])
    bf = jnp.bfloat16
    return f1.astype(bf), e.astype(bf), g.astype(bf), ginv.astype(bf)


def _fft_stage1(src_ref, f1_ref, s1_ref, n_a):
    def body(b, carry):
        zb = src_ref[pl.ds(b, n_a, stride=Z_PITCH), :]
        s1_ref[pl.ds(pl.multiple_of(b * S1_PITCH, 8), 2 * FFT_K1), :] = jnp.dot(
            f1_ref[b], zb.astype(jnp.bfloat16), preferred_element_type=jnp.float32)
        return carry
    lax.fori_loop(0, FFT_R, body, 0, unroll=FFT_UNROLL)


def _fft_stage2(s1_ref, g_ref, k1):
    are = s1_ref[pl.ds(k1, FFT_R, stride=S1_PITCH), :]
    aim = s1_ref[pl.ds(FFT_K1 + k1, FFT_R, stride=S1_PITCH), :]
    r = jnp.concatenate([are, aim], axis=0).astype(jnp.bfloat16)
    return jnp.dot(g_ref[...], r, preferred_element_type=jnp.float32)


def _filter_fft_kernel(f_ref, f1_ref, g_ref, h_ref, s1_ref):
    _fft_stage1(f_ref, f1_ref, s1_ref, FFT_R)

    def body(k1, carry):
        h_ref[k1] = _fft_stage2(s1_ref, g_ref, k1).astype(h_ref.dtype)
        return carry
    lax.fori_loop(0, FFT_K1, body, 0, unroll=FFT_UNROLL)


def filter_spectrum(filt_padded, f1, g):
    rows, width = filt_padded.shape
    nt = width // HY_LANES
    once = pl.Buffered(1)
    return pl.pallas_call(
        _filter_fft_kernel,
        grid=(nt,),
        in_specs=[pl.BlockSpec((rows, HY_LANES), lambda c: (0, c)),
                  pl.BlockSpec(f1.shape, lambda c: (0, 0, 0), pipeline_mode=once),
                  pl.BlockSpec(g.shape, lambda c: (0, 0), pipeline_mode=once)],
        out_specs=pl.BlockSpec((None, FFT_K1, 2 * FFT_R, HY_LANES), lambda c: (c, 0, 0, 0)),
        out_shape=jax.ShapeDtypeStruct((nt, FFT_K1, 2 * FFT_R, HY_LANES), jnp.bfloat16),
        scratch_shapes=[pltpu.VMEM((FFT_R * S1_PITCH, HY_LANES), jnp.float32)],
        compiler_params=_cparams(("arbitrary",)),
        name="hyena_filter_fft",
    )(filt_padded, f1, g)


def _short_conv_chunks(x_ref, w_ref, b_ref, dst_ref, n_chunks):
    R = FFT_R
    w = w_ref[...]
    bias = b_ref[...]
    row = lax.broadcasted_iota(jnp.int32, (R, HY_LANES), 0)
    for a in range(n_chunks):
        xc = x_ref[a * R:(a + 1) * R, :].astype(jnp.float32)
        prev = pltpu.roll(xc, 1, axis=0)
        nxt = pltpu.roll(xc, R - 1, axis=0)
        if a > 0:
            last = x_ref[a * R - 16:a * R, :].astype(jnp.float32)[15:16, :]
        else:
            last = jnp.zeros((1, HY_LANES), jnp.float32)
        if a < n_chunks - 1:
            first = x_ref[(a + 1) * R:(a + 1) * R + 16, :].astype(jnp.float32)[0:1, :]
        else:
            first = jnp.zeros((1, HY_LANES), jnp.float32)
        prev = jnp.where(row == 0, last, prev)
        nxt = jnp.where(row == R - 1, first, nxt)
        dst_ref[a * Z_PITCH:a * Z_PITCH + R, :] = bias + prev * w[0:1, :] + xc * w[1:2, :] + nxt * w[2:3, :]


def _hyena_conv_kernel(v_ref, x_ref, wv_ref, bv_ref, wx_ref, bx_ref, skip_ref, h_ref, f1_ref, e_ref, g_ref, gi_ref,
                       o_ref, z_ref, gate_ref, s1_ref, s2_ref, *, n_chunks):
    n = pl.program_id(2)
    R, K1 = FFT_R, FFT_K1

    @pl.when(n == 0)
    def _():
        _short_conv_chunks(v_ref, wv_ref, bv_ref, z_ref, n_chunks)

    _short_conv_chunks(x_ref, wx_ref, bx_ref, gate_ref, n_chunks)

    _fft_stage1(z_ref, f1_ref, s1_ref, n_chunks)

    def mid(k1, carry):
        x = _fft_stage2(s1_ref, g_ref, k1)
        h = h_ref[k1].astype(jnp.float32)
        xr, xi, hr, hi = x[:R], x[R:], h[:R], h[R:]
        p = jnp.concatenate([xr * hr - xi * hi, xr * hi + xi * hr], axis=0).astype(jnp.bfloat16)
        s2_ref[pl.ds(pl.multiple_of(k1 * S2_PITCH, 8), 2 * R), :] = jnp.dot(
            gi_ref[...], p, preferred_element_type=jnp.float32)
        return carry
    lax.fori_loop(0, K1, mid, 0, unroll=FFT_UNROLL)

    skip = skip_ref[...]

    def last(b, carry):
        qre = s2_ref[pl.ds(b, K1, stride=S2_PITCH), :]
        qim = s2_ref[pl.ds(R + b, K1, stride=S2_PITCH), :]
        r = jnp.concatenate([qre, qim], axis=0).astype(jnp.bfloat16)
        conv = jnp.dot(e_ref[b], r, preferred_element_type=jnp.float32)
        zold = z_ref[pl.ds(b, n_chunks, stride=Z_PITCH), :]
        gate = gate_ref[pl.ds(b, n_chunks, stride=Z_PITCH), :]
        z_ref[pl.ds(b, n_chunks, stride=Z_PITCH), :] = gate * (conv + zold * skip)
        return carry
    lax.fori_loop(0, R, last, 0, unroll=FFT_UNROLL)

    @pl.when(n == HY_ORDER - 1)
    def _():
        for a in range(n_chunks):
            o_ref[a * R:(a + 1) * R, :] = z_ref[a * Z_PITCH:a * Z_PITCH + R, :].astype(o_ref.dtype)


def hyena_conv(proj, conv_w, conv_b, skip, spec, tables, *, n_lat):
    f1, e, g, ginv = tables
    B = proj.shape[0]
    n_chunks = n_lat // FFT_R
    nt = HY_CH // HY_LANES
    once = pl.Buffered(1)
    grp = lambda n: (1 + n) * nt
    f1h = f1[:, :, :n_chunks]
    return pl.pallas_call(
        functools.partial(_hyena_conv_kernel, n_chunks=n_chunks),
        grid=(B, nt, HY_ORDER),
        in_specs=[pl.BlockSpec((None, n_lat, HY_LANES), lambda b, c, n: (b, 0, c)),
                  pl.BlockSpec((None, n_lat, HY_LANES), lambda b, c, n: (b, 0, grp(n) + c)),
                  pl.BlockSpec((HY_SHORT, HY_LANES), lambda b, c, n: (0, c)),
                  pl.BlockSpec((1, HY_LANES), lambda b, c, n: (0, c)),
                  pl.BlockSpec((HY_SHORT, HY_LANES), lambda b, c, n: (0, grp(n) + c)),
                  pl.BlockSpec((1, HY_LANES), lambda b, c, n: (0, grp(n) + c)),
                  pl.BlockSpec((None, 1, HY_LANES), lambda b, c, n: (n, 0, c)),
                  pl.BlockSpec((None, FFT_K1, 2 * FFT_R, HY_LANES), lambda b, c, n: (n * nt + c, 0, 0, 0),
                               pipeline_mode=once),
                  pl.BlockSpec(f1h.shape, lambda b, c, n: (0, 0, 0), pipeline_mode=once),
                  pl.BlockSpec(e.shape, lambda b, c, n: (0, 0, 0), pipeline_mode=once),
                  pl.BlockSpec(g.shape, lambda b, c, n: (0, 0), pipeline_mode=once),
                  pl.BlockSpec(ginv.shape, lambda b, c, n: (0, 0), pipeline_mode=once)],
        out_specs=pl.BlockSpec((None, n_lat, HY_LANES), lambda b, c, n: (b, 0, c)),
        out_shape=jax.ShapeDtypeStruct((B, n_lat, HY_CH), jnp.bfloat16),
        scratch_shapes=[pltpu.VMEM((n_chunks * Z_PITCH, HY_LANES), jnp.float32),
                        pltpu.VMEM((n_chunks * Z_PITCH, HY_LANES), jnp.float32),
                        pltpu.VMEM((FFT_R * S1_PITCH, HY_LANES), jnp.float32),
                        pltpu.VMEM((FFT_K1 * S2_PITCH, HY_LANES), jnp.float32)],
        compiler_params=_cparams(("parallel", "parallel", "arbitrary")),
        name="hyena_conv",
    )(proj, proj, conv_w, conv_b.reshape(1, -1), conv_w, conv_b.reshape(1, -1),
      skip.reshape(HY_ORDER, 1, HY_CH), spec, f1h, e, g, ginv)


def _rope_tables(L, n_ctx):
    rows = jnp.repeat(jnp.arange(L // GRID_W), GRID_W)
    cols = jnp.tile(jnp.arange(GRID_W), L // GRID_W)
    quarter = HEAD_DIM // 4
    inv = ROPE_BASE ** (-jnp.arange(quarter, dtype=jnp.float32) / quarter)
    ar = rows.astype(jnp.float32)[:, None] * inv
    ac = cols.astype(jnp.float32)[:, None] * inv
    ang = jnp.concatenate([ar, ar, ac, ac], axis=-1)
    cos = jnp.concatenate([jnp.cos(ang), jnp.ones((n_ctx, HEAD_DIM), jnp.float32)], axis=0)
    sin = jnp.concatenate([jnp.sin(ang), jnp.zeros((n_ctx, HEAD_DIM), jnp.float32)], axis=0)
    return cos, sin


def _head_norm_rope(t, g, cos, sin, scale):
    B, T, W = t.shape
    x = t.astype(jnp.float32).reshape(B, T, W // HEAD_DIM, HEAD_DIM)
    x = x * lax.rsqrt(jnp.mean(x * x, axis=-1, keepdims=True) + EPS) * g.astype(jnp.float32)
    x1, x2, x3, x4 = jnp.split(x, 4, axis=-1)
    rot = jnp.concatenate([-x2, x1, -x4, x3], axis=-1)
    x = x * cos[None, :, None, :] + rot * sin[None, :, None, :]
    return (x * scale).astype(jnp.bfloat16).reshape(B, T, W)


def _short_conv(u, w, b):
    L = u.shape[1]
    pad = HY_SHORT // 2
    up = jnp.pad(u, ((0, 0), (pad, pad), (0, 0)))
    y = b + up[:, 0:L] * w[0]
    for j in range(1, HY_SHORT):
        y = y + up[:, j:j + L] * w[j]
    return y


def _implicit_filters(L, w_in, w_hid, b, freq, w_out):
    f32 = jnp.float32
    t = jnp.linspace(0.0, 1.0, L, dtype=f32)[:, None]
    w = (2.0 * math.pi / L) * jnp.arange(L, dtype=f32)[:, None]
    f = jnp.linspace(1e-4, HY_BANDS - 1, HY_BANDS, dtype=f32)[None, :]
    z = jnp.concatenate([t, jnp.cos(f * w), -jnp.sin(f * w)], axis=-1)
    h = jnp.sin(freq[0] * (jnp.dot(z, w_in, precision=HI) + b[0]))
    for n in range(HY_FILTER_HIDDEN_LAYERS):
        h = jnp.sin(freq[n + 1] * (jnp.dot(h, w_hid[n], precision=HI) + b[n + 1]))
    h = jnp.dot(h, w_out, precision=HI).reshape(L, HY_DIRS, HY_ORDER, HY_CH)
    max_decay = math.log(HY_DECAY_TARGET) / HY_FAST_DECAY
    min_decay = math.log(HY_DECAY_TARGET) / HY_SLOW_DECAY
    deltas = jnp.abs(jnp.linspace(min_decay, max_decay, HY_ORDER * HY_CH, dtype=f32)).reshape(HY_ORDER, HY_CH)
    decay = jnp.exp(-t[:, :, None] * deltas)
    h = h * decay[:, None]
    fwd = h[:, 0]
    bwd = h[:0:-1, 1]
    return jnp.concatenate([fwd, jnp.zeros_like(fwd[:1]), bwd], axis=0)


def _hyena(u, conv_w, conv_b, f_w_in, f_w_hid, f_b, f_freq, f_w_out, skip):
    L = u.shape[1]
    u = _short_conv(u, conv_w, conv_b)
    v, *gates = jnp.split(u, HY_ORDER + 1, axis=-1)
    filt_f = jnp.fft.rfft(_implicit_filters(L, f_w_in, f_w_hid, f_b, f_freq, f_w_out), axis=0)
    z = v
    for n, g in enumerate(gates):
        conv = jnp.fft.irfft(jnp.fft.rfft(z, n=2 * L, axis=1) * filt_f[:, n], n=2 * L, axis=1)[:, :L]
        z = g * (conv + z * skip[n])
    return z


def _context_sink_attention(q, k, v, sink):
    B, Lc, _ = q.shape
    qg = q.astype(jnp.float32).reshape(B, Lc, WIN_KV_HEADS, WIN_GROUP, HEAD_DIM)
    kh = k.astype(jnp.float32).reshape(B, Lc, WIN_KV_HEADS, HEAD_DIM)
    vh = v.astype(jnp.float32).reshape(B, Lc, WIN_KV_HEADS, HEAD_DIM)
    s = jnp.einsum('bqhgd,bkhd->bhgqk', qg, kh, precision=HI)
    sink_b = jnp.broadcast_to(sink.astype(jnp.float32).reshape(WIN_KV_HEADS, WIN_GROUP, 1, 1),
                              (B, WIN_KV_HEADS, WIN_GROUP, Lc, 1))
    p = jax.nn.softmax(jnp.concatenate([sink_b, s], axis=-1), axis=-1)[..., 1:]
    o = jnp.einsum('bhgqk,bkhd->bqhgd', p, vh, precision=HI)
    return o.reshape(B, Lc, WIN_Q)


def _ada_modulation(cv, down, up, b):
    m = jnp.dot(jnp.dot(jax.nn.silu(cv), down, precision=HI), up, precision=HI) + b
    return m.reshape(m.shape[:-1] + (N_MOD, m.shape[-1] // N_MOD))


def kernel(x, c, ctx, c_ctx, norm_g, ada_down, ada_up, ada_b, ev_w_in, ev_conv_w, ev_conv_b, ev_filt_w_in, ev_filt_w_hid, ev_filt_b, ev_filt_freq, ev_filt_w_out, ev_hy_skip, ev_qk_g, ev_sink, ev_w_out, ev_ffn_w1, ev_ffn_w3, ev_ffn_w2, od_w_qkv, od_qk_g, od_lambda, od_subln_g, od_w_out, od_router_w, od_router_b, od_moe_w1, od_moe_w3, od_moe_w2):
    B, L, D = x.shape
    Lc = ctx.shape[1]
    T = L + Lc
    bf16 = jnp.bfloat16
    cos, sin = _rope_tables(L, Lc)
    tables = _dft_tables()
    X = jnp.concatenate([x, ctx], axis=1)
    qk_scale = HEAD_DIM ** -0.5
    is_ctx = (jnp.arange(T) >= L).astype(jnp.int32)

    for i in range(DEPTH):
        j = i // 2
        m_l = _ada_modulation(c, ada_down[i], ada_up[i], ada_b[i])
        m_c = _ada_modulation(c_ctx, ada_down[i], ada_up[i], ada_b[i])
        mods = jnp.stack([m_l, jnp.broadcast_to(m_c[None], (B, N_MOD, D))], axis=1)

        h = norm_mod(X, norm_g[i, 0], mods[:, :, 0], mods[:, :, 1])
        hf = h.reshape(B * T, D)
        if i % 2 == 0:
            proj = matmul(hf, ev_w_in[j].astype(bf16)).reshape(B, T, -1)
            q = proj[..., HY_WIDTH:HY_WIDTH + WIN_Q]
            k = proj[..., HY_WIDTH + WIN_Q:HY_WIDTH + WIN_Q + WIN_KV]
            v = proj[..., HY_WIDTH + WIN_Q + WIN_KV:]
            q = _head_norm_rope(q, ev_qk_g[j, 0], cos, sin, qk_scale)
            k = _head_norm_rope(k, ev_qk_g[j, 1], cos, sin, 1.0)
            filt_args = (ev_filt_w_in[j], ev_filt_w_hid[j], ev_filt_b[j], ev_filt_freq[j], ev_filt_w_out[j])
            filt = _implicit_filters(L, *filt_args).reshape(FFT_R, FFT_R, HY_ORDER * HY_CH)
            filt = jnp.pad(filt, ((0, 0), (0, Z_PITCH - FFT_R), (0, 0))).reshape(FFT_R * Z_PITCH, HY_ORDER * HY_CH)
            spec = filter_spectrum(filt, tables[0], tables[2])
            hy_l = hyena_conv(proj, ev_conv_w[j], ev_conv_b[j], ev_hy_skip[j], spec, tables, n_lat=L)
            hy_c = _hyena(proj[:, L:, :HY_WIDTH].astype(jnp.float32), ev_conv_w[j], ev_conv_b[j], *filt_args,
                          ev_hy_skip[j])
            at_l = window_attention(q, k, v, ev_sink[j], n_ctx=Lc)
            at_c = _context_sink_attention(q[:, L:], k[:, L:], v[:, L:], ev_sink[j])
            mix = jnp.concatenate(
                [jnp.concatenate([hy_l, at_l], axis=-1),
                 jnp.concatenate([hy_c.astype(bf16), at_c.astype(bf16)], axis=-1)], axis=1)
            w_out = ev_w_out[j]
        else:
            lam_init = 0.8 - 0.6 * math.exp(-0.3 * i)
            qkv = matmul(hf, od_w_qkv[j].astype(bf16)).reshape(B, T, -1)
            q = _head_norm_rope(qkv[..., :DIFF_Q], od_qk_g[j, 0], cos, sin, DIFF_DIM ** -0.5 * math.log2(math.e))
            k = _head_norm_rope(qkv[..., DIFF_Q:2 * DIFF_Q], od_qk_g[j, 1], cos, sin, 1.0)
            v = qkv[..., 2 * DIFF_Q:]
            lp = od_lambda[j].astype(jnp.float32)
            lam = jnp.exp(jnp.sum(lp[0] * lp[1])) - jnp.exp(jnp.sum(lp[2] * lp[3])) + lam_init
            o_l = diff_attention(q[:, :L], k, v, lam, od_subln_g[j], n_keys=T, key_blk=0, tq=512, tk=768,
                                 out_scale=1.0 - lam_init)
            o_c = diff_attention(q[:, L:], k, v, lam, od_subln_g[j], n_keys=Lc, key_blk=L // Lc, tq=Lc, tk=Lc,
                                 out_scale=1.0 - lam_init)
            mix = jnp.concatenate([o_l, o_c], axis=1)
            w_out = od_w_out[j]

        X = matmul_resgate(mix.reshape(B * T, D), w_out.astype(bf16), X, mods[:, :, 2], n_lat=L)
        h2 = norm_mod(X, norm_g[i, 1], mods[:, :, 3], mods[:, :, 4]).reshape(B * T, D)
        if i % 2 == 0:
            gact = matmul_swiglu(h2, ev_ffn_w1[j].astype(bf16), ev_ffn_w3[j].astype(bf16))
            X = matmul_resgate(gact, ev_ffn_w2[j].astype(bf16), X, mods[:, :, 5], n_lat=L, tk=2048)
        else:
            f = moe_layer(h2, od_router_w[j], od_router_b[j], od_moe_w1[j].astype(bf16),
                          od_moe_w3[j].astype(bf16), od_moe_w2[j].astype(bf16))
            X = X + mods[:, :, 5][:, is_ctx] * f.reshape(B, T, D)
    return X[:, :L]
```

```python
import functools
import math

import jax
import jax.numpy as jnp
from jax import lax
from jax.experimental import pallas as pl
from jax.experimental.pallas import tpu as pltpu

D_MODEL = 4096
DEPTH = 4
GRID_W = 64
N_MOD = 6
EPS = 1e-6
NEG_INF = -1e30
HEAD_DIM = 128
ROPE_BASE = 10000.0
BLOCK = 128
WINDOW = 128
HY_CH = D_MODEL // 2
HY_ORDER = 2
HY_SHORT = 3
HY_EMB = 33
HY_BANDS = (HY_EMB - 1) // 2
HY_FILTER_HIDDEN_LAYERS = 2
HY_DIRS = 2
HY_FAST_DECAY = 0.3
HY_SLOW_DECAY = 1.5
HY_DECAY_TARGET = 1e-2
HY_WIDTH = (HY_ORDER + 1) * HY_CH
WIN_HEADS = (D_MODEL - HY_CH) // HEAD_DIM
WIN_KV_HEADS = WIN_HEADS // 4
WIN_GROUP = WIN_HEADS // WIN_KV_HEADS
WIN_Q = WIN_HEADS * HEAD_DIM
WIN_KV = WIN_KV_HEADS * HEAD_DIM
DIFF_DIM = 128
DIFF_HEADS = D_MODEL // (2 * DIFF_DIM)
DIFF_Q = DIFF_HEADS * 2 * DIFF_DIM
N_EXPERTS = 8
TOP_K = 2

VMEM_LIMIT_BYTES = 56 * 1024 * 1024
ROW_TILE = 768
NORM_TILE = 256
MOE_TILE = 512

HI = lax.Precision.HIGHEST


def _cparams(sem):
    return pltpu.CompilerParams(dimension_semantics=sem, vmem_limit_bytes=VMEM_LIMIT_BYTES)


def _norm_mod_kernel(x_ref, g_ref, shift_ref, scale_ref, o_ref):
    x = x_ref[...]
    y = x * lax.rsqrt(jnp.mean(x * x, axis=-1, keepdims=True) + EPS)
    y = y * g_ref[...]
    o_ref[...] = (y * (1.0 + scale_ref[...]) + shift_ref[...]).astype(o_ref.dtype)


def norm_mod(x, g, shift, scale):
    B, T, D = x.shape
    nt = T // NORM_TILE
    mod_spec = pl.BlockSpec((None, None, 1, D), lambda b, t: (b, t // (nt - 1), 0, 0))
    return pl.pallas_call(
        _norm_mod_kernel,
        grid=(B, nt),
        in_specs=[
            pl.BlockSpec((None, NORM_TILE, D), lambda b, t: (b, t, 0)),
            pl.BlockSpec((1, D), lambda b, t: (0, 0)),
            mod_spec, mod_spec,
        ],
        out_specs=pl.BlockSpec((None, NORM_TILE, D), lambda b, t: (b, t, 0)),
        out_shape=jax.ShapeDtypeStruct((B, T, D), jnp.bfloat16),
        compiler_params=_cparams(("parallel", "parallel")),
        name="norm_mod",
    )(x, g.reshape(1, D), shift.reshape(B, 2, 1, D), scale.reshape(B, 2, 1, D))


def _mm_kernel(a_ref, b_ref, o_ref, acc_ref, *, nk):
    k = pl.program_id(2)
    part = jnp.dot(a_ref[...], b_ref[...], preferred_element_type=jnp.float32)
    if nk == 1:
        o_ref[...] = part.astype(o_ref.dtype)
        return

    @pl.when(k == 0)
    def _():
        acc_ref[...] = part

    @pl.when(k > 0)
    def _():
        acc_ref[...] += part

    @pl.when(k == nk - 1)
    def _():
        o_ref[...] = acc_ref[...].astype(o_ref.dtype)


def matmul(a, b, *, out_dtype=jnp.bfloat16, tm=ROW_TILE, tn=512, tk=None):
    M, K = a.shape
    _, N = b.shape
    tk = K if tk is None else tk
    nk = K // tk
    return pl.pallas_call(
        functools.partial(_mm_kernel, nk=nk),
        grid=(M // tm, N // tn, nk),
        in_specs=[pl.BlockSpec((tm, tk), lambda i, j, k: (i, k)),
                  pl.BlockSpec((tk, tn), lambda i, j, k: (k, j))],
        out_specs=pl.BlockSpec((tm, tn), lambda i, j, k: (i, j)),
        out_shape=jax.ShapeDtypeStruct((M, N), out_dtype),
        scratch_shapes=[pltpu.VMEM((tm, tn), jnp.float32)],
        compiler_params=_cparams(("parallel", "parallel", "arbitrary")),
        name="matmul",
    )(a, b)


def _mm_swiglu_kernel(a_ref, w1_ref, w3_ref, o_ref):
    a = a_ref[...]
    h1 = jnp.dot(a, w1_ref[...], preferred_element_type=jnp.float32)
    h3 = jnp.dot(a, w3_ref[...], preferred_element_type=jnp.float32)
    o_ref[...] = (h1 * jax.nn.sigmoid(h1) * h3).astype(o_ref.dtype)


def matmul_swiglu(a, w1, w3, *, tm=ROW_TILE, tn=512):
    M, K = a.shape
    _, N = w1.shape
    return pl.pallas_call(
        _mm_swiglu_kernel,
        grid=(M // tm, N // tn),
        in_specs=[pl.BlockSpec((tm, K), lambda i, j: (i, 0)),
                  pl.BlockSpec((K, tn), lambda i, j: (0, j)),
                  pl.BlockSpec((K, tn), lambda i, j: (0, j))],
        out_specs=pl.BlockSpec((tm, tn), lambda i, j: (i, j)),
        out_shape=jax.ShapeDtypeStruct((M, N), jnp.bfloat16),
        compiler_params=_cparams(("parallel", "parallel")),
        name="matmul_swiglu",
    )(a, w1, w3)


def _mm_resgate_kernel(a_ref, b_ref, res_ref, gate_ref, o_ref, acc_ref, *, nk, tm, tiles_per_batch, n_lat):
    i = pl.program_id(0)
    k = pl.program_id(2)
    part = jnp.dot(a_ref[...], b_ref[...], preferred_element_type=jnp.float32)

    def finish(acc):
        row = (i % tiles_per_batch) * tm + lax.broadcasted_iota(jnp.int32, (tm, 1), 0)
        gate = jnp.where(row < n_lat, gate_ref[0:1, :], gate_ref[1:2, :])
        o_ref[...] = res_ref[...] + gate * acc

    if nk == 1:
        finish(part)
        return

    @pl.when(k == 0)
    def _():
        acc_ref[...] = part

    @pl.when(k > 0)
    def _():
        acc_ref[...] += part

    @pl.when(k == nk - 1)
    def _():
        finish(acc_ref[...])


def matmul_resgate(a, b, res, gate, *, n_lat, tm=ROW_TILE, tn=512, tk=None):
    B, T, N = res.shape
    M, K = a.shape
    tk = K if tk is None else tk
    nk = K // tk
    tpb = T // tm
    kern = functools.partial(_mm_resgate_kernel, nk=nk, tm=tm, tiles_per_batch=tpb, n_lat=n_lat)
    out = pl.pallas_call(
        kern,
        grid=(M // tm, N // tn, nk),
        in_specs=[pl.BlockSpec((tm, tk), lambda i, j, k: (i, k)),
                  pl.BlockSpec((tk, tn), lambda i, j, k: (k, j)),
                  pl.BlockSpec((tm, tn), lambda i, j, k: (i, j)),
                  pl.BlockSpec((None, 2, tn), lambda i, j, k: (i // tpb, 0, j))],
        out_specs=pl.BlockSpec((tm, tn), lambda i, j, k: (i, j)),
        out_shape=jax.ShapeDtypeStruct((M, N), jnp.float32),
        scratch_shapes=[pltpu.VMEM((tm, tn), jnp.float32)],
        compiler_params=_cparams(("parallel", "parallel", "arbitrary")),
        name="matmul_resgate",
    )(a, b, res.reshape(M, N), gate)
    return out.reshape(B, T, N)


def _window_attn_kernel(sink_ref, q_ref, kc_ref, kp_ref, ko_ref, kn_ref, vc_ref, vp_ref, vo_ref, vn_ref, o_ref,
                        *, n_lat, n_ctx):
    n = pl.program_id(1)
    h = pl.program_id(2)
    G = WIN_GROUP
    q = q_ref[...]
    qs = jnp.concatenate([q[:, g * HEAD_DIM:(g + 1) * HEAD_DIM] for g in range(G)], axis=0)
    k = jnp.concatenate([kc_ref[...], kp_ref[...], ko_ref[...], kn_ref[...]], axis=0)
    v = jnp.concatenate([vc_ref[...], vp_ref[...], vo_ref[...], vn_ref[...]], axis=0)
    s = lax.dot_general(qs, k, (((1,), (1,)), ((), ())), preferred_element_type=jnp.float32)
    nk = n_ctx + 3 * BLOCK
    col = lax.broadcasted_iota(jnp.int32, (G * BLOCK, nk), 1)
    qpos = n * BLOCK + lax.broadcasted_iota(jnp.int32, (G * BLOCK, nk), 0) % BLOCK
    kpos = (n - 1) * BLOCK + (col - n_ctx)
    valid = (col < n_ctx) | ((jnp.abs(qpos - kpos) <= WINDOW) & (kpos >= 0) & (kpos < n_lat))
    s = jnp.where(valid, s, NEG_INF)
    sink = jnp.concatenate(
        [jnp.full((BLOCK, 1), sink_ref[h * G + g], jnp.float32) for g in range(G)], axis=0)
    m = jnp.maximum(jnp.max(s, axis=-1, keepdims=True), sink)
    e = jnp.exp(s - m)
    denom = jnp.sum(e, axis=-1, keepdims=True) + jnp.exp(sink - m)
    p = (e / denom).astype(v.dtype)
    o = jnp.dot(p, v, preferred_element_type=jnp.float32)
    for g in range(G):
        o_ref[:, g * HEAD_DIM:(g + 1) * HEAD_DIM] = o[g * BLOCK:(g + 1) * BLOCK, :].astype(o_ref.dtype)


def window_attention(q, k, v, sink, *, n_ctx, v_col0=0):
    B, T, _ = q.shape
    L = T - n_ctx
    nb = L // BLOCK
    kv_blk = lambda f: pl.BlockSpec((None, BLOCK, HEAD_DIM), f)
    ctx_blk = lambda off: pl.BlockSpec((None, n_ctx, HEAD_DIM), lambda b, n, h: (b, L // n_ctx, off + h))
    prev_blk = lambda off: kv_blk(lambda b, n, h: (b, jnp.maximum(n - 1, 0), off + h))
    own_blk = lambda off: kv_blk(lambda b, n, h: (b, n, off + h))
    next_blk = lambda off: kv_blk(lambda b, n, h: (b, jnp.minimum(n + 1, nb - 1), off + h))
    voff = v_col0 // HEAD_DIM
    return pl.pallas_call(
        functools.partial(_window_attn_kernel, n_lat=L, n_ctx=n_ctx),
        grid=(B, nb, WIN_KV_HEADS),
        in_specs=[pl.BlockSpec(memory_space=pltpu.SMEM),
                  pl.BlockSpec((None, BLOCK, WIN_GROUP * HEAD_DIM), lambda b, n, h: (b, n, h)),
                  ctx_blk(0), prev_blk(0), own_blk(0), next_blk(0),
                  ctx_blk(voff), prev_blk(voff), own_blk(voff), next_blk(voff)],
        out_specs=pl.BlockSpec((None, BLOCK, WIN_GROUP * HEAD_DIM), lambda b, n, h: (b, n, h)),
        out_shape=jax.ShapeDtypeStruct((B, L, WIN_Q), jnp.bfloat16),
        compiler_params=_cparams(("parallel", "parallel", "parallel")),
        name="window_attention",
    )(sink.astype(jnp.float32), q, k, k, k, k, v, v, v, v)


def _diff_attn_kernel(lam_ref, q_ref, k_ref, v_ref, g_ref, o_ref, m_ref, l_ref, acc_ref,
                      *, n_keys, tk, out_scale):
    m_ref[...] = jnp.full(m_ref.shape, NEG_INF, jnp.float32)
    l_ref[...] = jnp.zeros(l_ref.shape, jnp.float32)
    acc_ref[...] = jnp.zeros(acc_ref.shape, jnp.float32)

    def step(j, carry):
        ks = pl.multiple_of(j * tk, tk)
        v = v_ref[pl.ds(ks, tk), :]
        for sub in range(2):
            q = q_ref[:, sub * DIFF_DIM:(sub + 1) * DIFF_DIM]
            k = k_ref[pl.ds(ks, tk), sub * DIFF_DIM:(sub + 1) * DIFF_DIM]
            s = lax.dot_general(q, k, (((1,), (1,)), ((), ())), preferred_element_type=jnp.float32)
            m_old = m_ref[sub]
            m_new = jnp.maximum(m_old, jnp.max(s, axis=-1, keepdims=True))
            alpha = jnp.exp2(m_old - m_new)
            p = jnp.exp2(s - m_new)
            l_ref[sub] = alpha * l_ref[sub] + jnp.sum(p, axis=-1, keepdims=True)
            acc_ref[sub] = alpha * acc_ref[sub] + jnp.dot(p.astype(v.dtype), v, preferred_element_type=jnp.float32)
            m_ref[sub] = m_new
        return carry

    lax.fori_loop(0, n_keys // tk, step, 0)
    lam = lam_ref[0]
    o = acc_ref[0] / l_ref[0] - lam * (acc_ref[1] / l_ref[1])
    o = o * lax.rsqrt(jnp.mean(o * o, axis=-1, keepdims=True) + EPS)
    o_ref[...] = (o * g_ref[...] * out_scale).astype(o_ref.dtype)


def diff_attention(q, k, v, lam, subln_g, *, n_q, q_blk0, n_keys, key_blk, tq, tk, out_scale, v_col0=0):
    B = q.shape[0]
    W = 2 * DIFF_DIM
    voff = v_col0 // W
    return pl.pallas_call(
        functools.partial(_diff_attn_kernel, n_keys=n_keys, tk=tk, out_scale=out_scale),
        grid=(B, DIFF_HEADS, n_q // tq),
        in_specs=[pl.BlockSpec(memory_space=pltpu.SMEM),
                  pl.BlockSpec((None, tq, W), lambda b, h, i: (b, q_blk0 + i, h)),
                  pl.BlockSpec((None, n_keys, W), lambda b, h, i: (b, key_blk, h)),
                  pl.BlockSpec((None, n_keys, W), lambda b, h, i: (b, key_blk, voff + h)),
                  pl.BlockSpec((1, W), lambda b, h, i: (0, 0))],
        out_specs=pl.BlockSpec((None, tq, W), lambda b, h, i: (b, i, h)),
        scratch_shapes=[pltpu.VMEM((2, tq, 1), jnp.float32),
                        pltpu.VMEM((2, tq, 1), jnp.float32),
                        pltpu.VMEM((2, tq, W), jnp.float32)],
        out_shape=jax.ShapeDtypeStruct((B, n_q, DIFF_Q), jnp.bfloat16),
        compiler_params=_cparams(("parallel", "parallel", "arbitrary")),
        name="diff_attention",
    )(lam.reshape(1).astype(jnp.float32), q, k, v, subln_g.reshape(1, W).astype(jnp.float32))


def _moe_up_kernel(te_ref, tv_ref, x_ref, w1_ref, w3_ref, o_ref):
    i = pl.program_id(1)

    @pl.when(tv_ref[i] > 0)
    def _():
        x = x_ref[...]
        h1 = jnp.dot(x, w1_ref[...], preferred_element_type=jnp.float32)
        h3 = jnp.dot(x, w3_ref[...], preferred_element_type=jnp.float32)
        o_ref[...] = (h1 * jax.nn.sigmoid(h1) * h3).astype(o_ref.dtype)

    @pl.when(tv_ref[i] == 0)
    def _():
        o_ref[...] = jnp.zeros(o_ref.shape, o_ref.dtype)


def _moe_down_kernel(te_ref, tv_ref, g_ref, w2_ref, rg_ref, o_ref):
    i = pl.program_id(1)

    @pl.when(tv_ref[i] > 0)
    def _():
        y = jnp.dot(g_ref[...], w2_ref[...], preferred_element_type=jnp.float32)
        o_ref[...] = (rg_ref[...] * y).astype(o_ref.dtype)

    @pl.when(tv_ref[i] == 0)
    def _():
        o_ref[...] = jnp.zeros(o_ref.shape, o_ref.dtype)


def moe_experts(xs, w1, w3, w2, tile_expert, tile_valid, row_gate, *, tn_up=768, tn_down=1024):
    P, D = xs.shape
    F = w1.shape[2]
    tm = MOE_TILE
    up = pl.pallas_call(
        _moe_up_kernel,
        grid_spec=pltpu.PrefetchScalarGridSpec(
            num_scalar_prefetch=2,
            grid=(F // tn_up, P // tm),
            in_specs=[pl.BlockSpec((tm, D), lambda j, i, te, tv: (i, 0)),
                      pl.BlockSpec((None, D, tn_up), lambda j, i, te, tv: (te[i], 0, j)),
                      pl.BlockSpec((None, D, tn_up), lambda j, i, te, tv: (te[i], 0, j))],
            out_specs=pl.BlockSpec((tm, tn_up), lambda j, i, te, tv: (i, j)),
        ),
        out_shape=jax.ShapeDtypeStruct((P, F), jnp.bfloat16),
        compiler_params=_cparams(("parallel", "arbitrary")),
        name="moe_up",
    )(tile_expert, tile_valid, xs, w1, w3)
    return pl.pallas_call(
        _moe_down_kernel,
        grid_spec=pltpu.PrefetchScalarGridSpec(
            num_scalar_prefetch=2,
            grid=(D // tn_down, P // tm),
            in_specs=[pl.BlockSpec((tm, F), lambda j, i, te, tv: (i, 0)),
                      pl.BlockSpec((None, F, tn_down), lambda j, i, te, tv: (te[i], 0, j)),
                      pl.BlockSpec((tm, 1), lambda j, i, te, tv: (i, 0))],
            out_specs=pl.BlockSpec((tm, tn_down), lambda j, i, te, tv: (i, j)),
        ),
        out_shape=jax.ShapeDtypeStruct((P, D), jnp.bfloat16),
        compiler_params=_cparams(("parallel", "arbitrary")),
        name="moe_down",
    )(tile_expert, tile_valid, up, w2, row_gate)


def moe_layer(h2, router_w, router_b, w1, w3, w2):
    N, D = h2.shape
    tm = MOE_TILE
    logits = jnp.dot(h2.astype(jnp.float32), router_w, precision=HI) + router_b
    top_v, top_i = lax.top_k(logits, TOP_K)
    gates = jax.nn.softmax(top_v, axis=-1)
    A = N * TOP_K
    e_flat = top_i.reshape(A).astype(jnp.int32)
    order = jnp.argsort(e_flat, stable=True)
    e_sorted = e_flat[order]
    counts = jnp.sum(e_flat[:, None] == jnp.arange(N_EXPERTS, dtype=jnp.int32)[None, :], axis=0).astype(jnp.int32)
    padded = ((counts + tm - 1) // tm) * tm
    start_unpadded = jnp.cumsum(counts) - counts
    start_padded = jnp.cumsum(padded) - padded
    dest_sorted = start_padded[e_sorted] + (jnp.arange(A, dtype=jnp.int32) - start_unpadded[e_sorted])
    P = A + N_EXPERTS * tm
    row_token = jnp.zeros((P,), jnp.int32).at[dest_sorted].set((order // TOP_K).astype(jnp.int32))
    row_gate = jnp.zeros((P,), jnp.float32).at[dest_sorted].set(gates.reshape(A)[order])
    pos = jnp.zeros((A,), jnp.int32).at[order].set(dest_sorted).reshape(N, TOP_K)
    tile_start = jnp.arange(P // tm, dtype=jnp.int32) * tm
    ends = jnp.cumsum(padded)
    tile_expert = jnp.minimum(jnp.sum(tile_start[:, None] >= ends[None, :], axis=1), N_EXPERTS - 1).astype(jnp.int32)
    tile_valid = (tile_start < ends[-1]).astype(jnp.int32)
    xs = jnp.take(h2, row_token, axis=0)
    y = moe_experts(xs, w1, w3, w2, tile_expert, tile_valid, row_gate.reshape(P, 1))
    return (jnp.take(y, pos[:, 0], axis=0).astype(jnp.float32)
            + jnp.take(y, pos[:, 1], axis=0).astype(jnp.float32))


FFT_R = 128
FFT_K1 = 72
Z_PITCH = FFT_R + 8
S1_PITCH = 2 * FFT_K1 + 8
S2_PITCH = 2 * FFT_R + 8
HY_LANES = 128
FFT_UNROLL = 8


def _dft_tables():
    R, K1 = FFT_R, FFT_K1
    N = R * R
    i32 = jnp.int32
    b = jnp.arange(R, dtype=i32)[:, None, None]
    k1 = jnp.arange(K1, dtype=i32)[None, :, None]
    a = jnp.arange(R, dtype=i32)[None, None, :]
    th = (2.0 * math.pi / N) * ((k1 * (R * a + b)) % N).astype(jnp.float32)
    f1 = jnp.concatenate([jnp.cos(th), -jnp.sin(th)], axis=1)
    w = jnp.where(jnp.arange(K1) > R // 2, 0.0, jnp.where((jnp.arange(K1) % (R // 2)) == 0, 1.0, 2.0)) / N
    the = jnp.swapaxes(th[:, :, :R // 2], 1, 2)
    e = jnp.concatenate([w * jnp.cos(the), -w * jnp.sin(the)], axis=2)
    k2 = jnp.arange(R, dtype=i32)
    ph = (2.0 * math.pi / R) * ((k2[:, None] * k2[None, :]) % R).astype(jnp.float32)
    c, s = jnp.cos(ph), jnp.sin(ph)
    g = jnp.block([[c, s], [-s, c]])
    ginv = jnp.block([[c, -s], [s, c]])
    bf = jnp.bfloat16
    return f1.astype(bf), e.astype(bf), g.astype(bf), ginv.astype(bf)


def _fft_stage1(src_ref, f1_ref, s1_ref, n_a):
    def body(b, carry):
        zb = src_ref[pl.ds(b, n_a, stride=Z_PITCH), :]
        s1_ref[pl.ds(pl.multiple_of(b * S1_PITCH, 8), 2 * FFT_K1), :] = jnp.dot(
            f1_ref[b], zb.astype(jnp.bfloat16), preferred_element_type=jnp.float32)
        return carry
    lax.fori_loop(0, FFT_R, body, 0, unroll=FFT_UNROLL)


def _fft_stage2(s1_ref, g_ref, k1):
    are = s1_ref[pl.ds(k1, FFT_R, stride=S1_PITCH), :]
    aim = s1_ref[pl.ds(FFT_K1 + k1, FFT_R, stride=S1_PITCH), :]
    r = jnp.concatenate([are, aim], axis=0).astype(jnp.bfloat16)
    return jnp.dot(g_ref[...], r, preferred_element_type=jnp.float32)


def _filter_fft_kernel(f_ref, f1_ref, g_ref, h_ref, s1_ref):
    _fft_stage1(f_ref, f1_ref, s1_ref, FFT_R)

    def body(k1, carry):
        h_ref[k1] = _fft_stage2(s1_ref, g_ref, k1).astype(h_ref.dtype)
        return carry
    lax.fori_loop(0, FFT_K1, body, 0, unroll=FFT_UNROLL)


def filter_spectrum(filt_padded, f1, g):
    rows, width = filt_padded.shape
    nt = width // HY_LANES
    once = pl.Buffered(1)
    return pl.pallas_call(
        _filter_fft_kernel,
        grid=(nt,),
        in_specs=[pl.BlockSpec((rows, HY_LANES), lambda c: (0, c)),
                  pl.BlockSpec(f1.shape, lambda c: (0, 0, 0), pipeline_mode=once),
                  pl.BlockSpec(g.shape, lambda c: (0, 0), pipeline_mode=once)],
        out_specs=pl.BlockSpec((None, FFT_K1, 2 * FFT_R, HY_LANES), lambda c: (c, 0, 0, 0)),
        out_shape=jax.ShapeDtypeStruct((nt, FFT_K1, 2 * FFT_R, HY_LANES), jnp.bfloat16),
        scratch_shapes=[pltpu.VMEM((FFT_R * S1_PITCH, HY_LANES), jnp.float32)],
        compiler_params=_cparams(("arbitrary",)),
        name="hyena_filter_fft",
    )(filt_padded, f1, g)


def _short_conv_chunks(x_ref, w_ref, b_ref, dst_ref, n_chunks, pitch=Z_PITCH):
    R = FFT_R
    w = w_ref[...]
    bias = b_ref[...]
    row = lax.broadcasted_iota(jnp.int32, (R, HY_LANES), 0)
    for a in range(n_chunks):
        xc = x_ref[a * R:(a + 1) * R, :].astype(jnp.float32)
        prev = pltpu.roll(xc, 1, axis=0)
        nxt = pltpu.roll(xc, R - 1, axis=0)
        if a > 0:
            last = x_ref[a * R - 16:a * R, :].astype(jnp.float32)[15:16, :]
        else:
            last = jnp.zeros((1, HY_LANES), jnp.float32)
        if a < n_chunks - 1:
            first = x_ref[(a + 1) * R:(a + 1) * R + 16, :].astype(jnp.float32)[0:1, :]
        else:
            first = jnp.zeros((1, HY_LANES), jnp.float32)
        prev = jnp.where(row == 0, last, prev)
        nxt = jnp.where(row == R - 1, first, nxt)
        dst_ref[a * pitch:a * pitch + R, :] = bias + prev * w[0:1, :] + xc * w[1:2, :] + nxt * w[2:3, :]


def _hyena_conv_kernel(v_ref, x_ref, wv_ref, bv_ref, wx_ref, bx_ref, skip_ref, h_ref, f1_ref, e_ref, g_ref, gi_ref,
                       o_ref, z_ref, gate_ref, s1_ref, s2_ref, *, n_chunks):
    n = pl.program_id(2)
    R, K1 = FFT_R, FFT_K1

    @pl.when(n == 0)
    def _():
        _short_conv_chunks(v_ref, wv_ref, bv_ref, z_ref, n_chunks)

    _short_conv_chunks(x_ref, wx_ref, bx_ref, gate_ref, n_chunks)

    _fft_stage1(z_ref, f1_ref, s1_ref, n_chunks)

    def mid(k1, carry):
        x = _fft_stage2(s1_ref, g_ref, k1)
        h = h_ref[k1].astype(jnp.float32)
        xr, xi, hr, hi = x[:R], x[R:], h[:R], h[R:]
        p = jnp.concatenate([xr * hr - xi * hi, xr * hi + xi * hr], axis=0).astype(jnp.bfloat16)
        s2_ref[pl.ds(pl.multiple_of(k1 * S2_PITCH, 8), 2 * R), :] = jnp.dot(
            gi_ref[...], p, preferred_element_type=jnp.float32)
        return carry
    lax.fori_loop(0, K1, mid, 0, unroll=FFT_UNROLL)

    skip = skip_ref[...]

    def last(b, carry):
        qre = s2_ref[pl.ds(b, K1, stride=S2_PITCH), :]
        qim = s2_ref[pl.ds(R + b, K1, stride=S2_PITCH), :]
        r = jnp.concatenate([qre, qim], axis=0).astype(jnp.bfloat16)
        conv = jnp.dot(e_ref[b], r, preferred_element_type=jnp.float32)
        zold = z_ref[pl.ds(b, n_chunks, stride=Z_PITCH), :]
        gate = gate_ref[pl.ds(b, n_chunks, stride=Z_PITCH), :]
        z_ref[pl.ds(b, n_chunks, stride=Z_PITCH), :] = gate * (conv + zold * skip)
        return carry
    lax.fori_loop(0, R, last, 0, unroll=FFT_UNROLL)

    @pl.when(n == HY_ORDER - 1)
    def _():
        for a in range(n_chunks):
            o_ref[a * R:(a + 1) * R, :] = z_ref[a * Z_PITCH:a * Z_PITCH + R, :].astype(o_ref.dtype)


def hyena_conv(proj, conv_w, conv_b, skip, spec, tables, *, n_lat):
    f1, e, g, ginv = tables
    B = proj.shape[0]
    n_chunks = n_lat // FFT_R
    nt = HY_CH // HY_LANES
    once = pl.Buffered(1)
    grp = lambda n: (1 + n) * nt
    f1h = f1[:, :, :n_chunks]
    return pl.pallas_call(
        functools.partial(_hyena_conv_kernel, n_chunks=n_chunks),
        grid=(B, nt, HY_ORDER),
        in_specs=[pl.BlockSpec((None, n_lat, HY_LANES), lambda b, c, n: (b, 0, c)),
                  pl.BlockSpec((None, n_lat, HY_LANES), lambda b, c, n: (b, 0, grp(n) + c)),
                  pl.BlockSpec((HY_SHORT, HY_LANES), lambda b, c, n: (0, c)),
                  pl.BlockSpec((1, HY_LANES), lambda b, c, n: (0, c)),
                  pl.BlockSpec((HY_SHORT, HY_LANES), lambda b, c, n: (0, grp(n) + c)),
                  pl.BlockSpec((1, HY_LANES), lambda b, c, n: (0, grp(n) + c)),
                  pl.BlockSpec((None, 1, HY_LANES), lambda b, c, n: (n, 0, c)),
                  pl.BlockSpec((None, FFT_K1, 2 * FFT_R, HY_LANES), lambda b, c, n: (n * nt + c, 0, 0, 0),
                               pipeline_mode=once),
                  pl.BlockSpec(f1h.shape, lambda b, c, n: (0, 0, 0), pipeline_mode=once),
                  pl.BlockSpec(e.shape, lambda b, c, n: (0, 0, 0), pipeline_mode=once),
                  pl.BlockSpec(g.shape, lambda b, c, n: (0, 0), pipeline_mode=once),
                  pl.BlockSpec(ginv.shape, lambda b, c, n: (0, 0), pipeline_mode=once)],
        out_specs=pl.BlockSpec((None, n_lat, HY_LANES), lambda b, c, n: (b, 0, c)),
        out_shape=jax.ShapeDtypeStruct((B, n_lat, HY_CH), jnp.bfloat16),
        scratch_shapes=[pltpu.VMEM((n_chunks * Z_PITCH, HY_LANES), jnp.float32),
                        pltpu.VMEM((n_chunks * Z_PITCH, HY_LANES), jnp.float32),
                        pltpu.VMEM((FFT_R * S1_PITCH, HY_LANES), jnp.float32),
                        pltpu.VMEM((FFT_K1 * S2_PITCH, HY_LANES), jnp.float32)],
        compiler_params=_cparams(("parallel", "parallel", "arbitrary")),
        name="hyena_conv",
    )(proj, proj, conv_w, conv_b.reshape(1, -1), conv_w, conv_b.reshape(1, -1),
      skip.reshape(HY_ORDER, 1, HY_CH), spec, f1h, e, g, ginv)


def _ctx_dft_tables(n_ctx):
    N = 2 * n_ctx
    nk = -(-(n_ctx + 1) // 16) * 16
    k = jnp.arange(nk, dtype=jnp.int32)[:, None]
    t = jnp.arange(N, dtype=jnp.int32)[None, :]
    th = (2.0 * math.pi / N) * ((k * t) % N).astype(jnp.float32)
    fwd = jnp.concatenate([jnp.cos(th), -jnp.sin(th)], axis=0)
    w = jnp.where(k > n_ctx, 0.0, jnp.where((k % n_ctx) == 0, 1.0, 2.0)) / N
    inv = jnp.concatenate([(w * jnp.cos(th[:, :n_ctx])).T, (-w * jnp.sin(th[:, :n_ctx])).T], axis=1)
    return fwd.astype(jnp.bfloat16), inv.astype(jnp.bfloat16)


def _hyena_ctx_kernel(v_ref, x1_ref, x2_ref, wv_ref, bv_ref, w1_ref, b1_ref, w2_ref, b2_ref, skip_ref,
                      f0_ref, f1_ref, fwd_ref, inv_ref, o_ref, z_ref, g1_ref, g2_ref, *, n_ctx):
    nc = n_ctx // FFT_R
    _short_conv_chunks(v_ref, wv_ref, bv_ref, z_ref, nc, pitch=FFT_R)
    _short_conv_chunks(x1_ref, w1_ref, b1_ref, g1_ref, nc, pitch=FFT_R)
    _short_conv_chunks(x2_ref, w2_ref, b2_ref, g2_ref, nc, pitch=FFT_R)
    nk = fwd_ref.shape[0] // 2
    z = z_ref[...]
    for n, (filt_ref, gate_ref) in enumerate(((f0_ref, g1_ref), (f1_ref, g2_ref))):
        h = jnp.dot(fwd_ref[...], filt_ref[...].astype(jnp.bfloat16), preferred_element_type=jnp.float32)
        x = jnp.dot(fwd_ref[:, :n_ctx], z.astype(jnp.bfloat16), preferred_element_type=jnp.float32)
        xr, xi, hr, hi = x[:nk], x[nk:], h[:nk], h[nk:]
        p = jnp.concatenate([xr * hr - xi * hi, xr * hi + xi * hr], axis=0).astype(jnp.bfloat16)
        conv = jnp.dot(inv_ref[...], p, preferred_element_type=jnp.float32)
        z = gate_ref[...] * (conv + z * skip_ref[n:n + 1, :])
    o_ref[...] = z.astype(o_ref.dtype)


def hyena_ctx(proj, conv_w, conv_b, skip, filt, tables, *, n_lat, n_ctx):
    fwd, inv = tables
    B = proj.shape[0]
    nt = HY_CH // HY_LANES
    rb = n_lat // n_ctx
    cb2 = conv_b.reshape(1, -1)
    row = lambda g: pl.BlockSpec((None, n_ctx, HY_LANES), lambda b, c: (b, rb, g * nt + c))
    wsp = lambda g: pl.BlockSpec((HY_SHORT, HY_LANES), lambda b, c: (0, g * nt + c))
    bsp = lambda g: pl.BlockSpec((1, HY_LANES), lambda b, c: (0, g * nt + c))
    fsp = lambda n: pl.BlockSpec((2 * n_ctx, HY_LANES), lambda b, c: (0, n * nt + c))
    scr = pltpu.VMEM((n_ctx, HY_LANES), jnp.float32)
    return pl.pallas_call(
        functools.partial(_hyena_ctx_kernel, n_ctx=n_ctx),
        grid=(B, nt),
        in_specs=[row(0), row(1), row(2), wsp(0), bsp(0), wsp(1), bsp(1), wsp(2), bsp(2),
                  pl.BlockSpec((HY_ORDER, HY_LANES), lambda b, c: (0, c)),
                  fsp(0), fsp(1),
                  pl.BlockSpec(fwd.shape, lambda b, c: (0, 0)),
                  pl.BlockSpec(inv.shape, lambda b, c: (0, 0))],
        out_specs=pl.BlockSpec((None, n_ctx, HY_LANES), lambda b, c: (b, 0, c)),
        out_shape=jax.ShapeDtypeStruct((B, n_ctx, HY_CH), jnp.bfloat16),
        scratch_shapes=[scr, scr, scr],
        compiler_params=_cparams(("parallel", "parallel")),
        name="hyena_ctx",
    )(proj, proj, proj, conv_w, cb2, conv_w, cb2, conv_w, cb2, skip, filt, filt, fwd, inv)


def _norm_rope_kernel(x_ref, g_ref, cos_ref, sin_ref, o_ref, *, scale, nh):
    cos = cos_ref[...]
    sin = sin_ref[...]
    g = g_ref[...]
    lane = lax.broadcasted_iota(jnp.int32, cos.shape, 1)
    lower = (lane % (HEAD_DIM // 2)) < (HEAD_DIM // 4)
    for h in range(nh):
        x = x_ref[:, h * HEAD_DIM:(h + 1) * HEAD_DIM].astype(jnp.float32)
        y = x * lax.rsqrt(jnp.mean(x * x, axis=-1, keepdims=True) + EPS) * g
        rot = jnp.where(lower, -pltpu.roll(y, HEAD_DIM - HEAD_DIM // 4, axis=1), pltpu.roll(y, HEAD_DIM // 4, axis=1))
        o_ref[:, h * HEAD_DIM:(h + 1) * HEAD_DIM] = ((y * cos + rot * sin) * scale).astype(o_ref.dtype)


def head_norm_rope(t, col0, width, g, cos, sin, scale, *, tr=ROW_TILE, nh=4):
    B, T, _ = t.shape
    wb = nh * HEAD_DIM
    c0 = col0 // wb
    return pl.pallas_call(
        functools.partial(_norm_rope_kernel, scale=scale, nh=nh),
        grid=(B, T // tr, width // wb),
        in_specs=[pl.BlockSpec((None, tr, wb), lambda b, r, j: (b, r, c0 + j)),
                  pl.BlockSpec((1, HEAD_DIM), lambda b, r, j: (0, 0)),
                  pl.BlockSpec((tr, HEAD_DIM), lambda b, r, j: (r, 0)),
                  pl.BlockSpec((tr, HEAD_DIM), lambda b, r, j: (r, 0))],
        out_specs=pl.BlockSpec((None, tr, wb), lambda b, r, j: (b, r, j)),
        out_shape=jax.ShapeDtypeStruct((B, T, width), jnp.bfloat16),
        compiler_params=_cparams(("parallel", "parallel", "parallel")),
        name="head_norm_rope",
    )(t, g.reshape(1, HEAD_DIM).astype(jnp.float32), cos, sin)


def _rope_tables(L, n_ctx):
    rows = jnp.repeat(jnp.arange(L // GRID_W), GRID_W)
    cols = jnp.tile(jnp.arange(GRID_W), L // GRID_W)
    quarter = HEAD_DIM // 4
    inv = ROPE_BASE ** (-jnp.arange(quarter, dtype=jnp.float32) / quarter)
    ar = rows.astype(jnp.float32)[:, None] * inv
    ac = cols.astype(jnp.float32)[:, None] * inv
    ang = jnp.concatenate([ar, ar, ac, ac], axis=-1)
    cos = jnp.concatenate([jnp.cos(ang), jnp.ones((n_ctx, HEAD_DIM), jnp.float32)], axis=0)
    sin = jnp.concatenate([jnp.sin(ang), jnp.zeros((n_ctx, HEAD_DIM), jnp.float32)], axis=0)
    return cos, sin


def _implicit_filters(L, w_in, w_hid, b, freq, w_out):
    f32 = jnp.float32
    t = jnp.linspace(0.0, 1.0, L, dtype=f32)[:, None]
    w = (2.0 * math.pi / L) * jnp.arange(L, dtype=f32)[:, None]
    f = jnp.linspace(1e-4, HY_BANDS - 1, HY_BANDS, dtype=f32)[None, :]
    z = jnp.concatenate([t, jnp.cos(f * w), -jnp.sin(f * w)], axis=-1)
    h = jnp.sin(freq[0] * (jnp.dot(z, w_in, precision=HI) + b[0]))
    for n in range(HY_FILTER_HIDDEN_LAYERS):
        h = jnp.sin(freq[n + 1] * (jnp.dot(h, w_hid[n], precision=HI) + b[n + 1]))
    width = HY_ORDER * HY_CH
    max_decay = math.log(HY_DECAY_TARGET) / HY_FAST_DECAY
    min_decay = math.log(HY_DECAY_TARGET) / HY_SLOW_DECAY
    deltas = jnp.abs(jnp.linspace(min_decay, max_decay, width, dtype=f32))[None, :]
    fwd = jnp.dot(h, w_out[:, :width], precision=HI) * jnp.exp(-t * deltas)
    bwd = jnp.dot(h[::-1], w_out[:, width:], precision=HI) * jnp.exp(-t[::-1] * deltas)
    return jnp.concatenate([fwd, jnp.zeros((1, width), f32), bwd[:L - 1]], axis=0)


def _context_sink_attention(q, k, v, sink):
    B, Lc, _ = q.shape
    qg = q.astype(jnp.float32).reshape(B, Lc, WIN_KV_HEADS, WIN_GROUP, HEAD_DIM)
    kh = k.astype(jnp.float32).reshape(B, Lc, WIN_KV_HEADS, HEAD_DIM)
    vh = v.astype(jnp.float32).reshape(B, Lc, WIN_KV_HEADS, HEAD_DIM)
    s = jnp.einsum('bqhgd,bkhd->bhgqk', qg, kh, precision=HI)
    sink_b = jnp.broadcast_to(sink.astype(jnp.float32).reshape(WIN_KV_HEADS, WIN_GROUP, 1, 1),
                              (B, WIN_KV_HEADS, WIN_GROUP, Lc, 1))
    p = jax.nn.softmax(jnp.concatenate([sink_b, s], axis=-1), axis=-1)[..., 1:]
    o = jnp.einsum('bhgqk,bkhd->bqhgd', p, vh, precision=HI)
    return o.reshape(B, Lc, WIN_Q)


def _ada_modulation(cv, down, up, b):
    m = jnp.dot(jnp.dot(jax.nn.silu(cv), down, precision=HI), up, precision=HI) + b
    return m.reshape(m.shape[:-1] + (N_MOD, m.shape[-1] // N_MOD))


def kernel(x, c, ctx, c_ctx, norm_g, ada_down, ada_up, ada_b, ev_w_in, ev_conv_w, ev_conv_b, ev_filt_w_in, ev_filt_w_hid, ev_filt_b, ev_filt_freq, ev_filt_w_out, ev_hy_skip, ev_qk_g, ev_sink, ev_w_out, ev_ffn_w1, ev_ffn_w3, ev_ffn_w2, od_w_qkv, od_qk_g, od_lambda, od_subln_g, od_w_out, od_router_w, od_router_b, od_moe_w1, od_moe_w3, od_moe_w2):
    B, L, D = x.shape
    Lc = ctx.shape[1]
    T = L + Lc
    bf16 = jnp.bfloat16
    cos, sin = _rope_tables(L, Lc)
    tables = _dft_tables()
    ctx_tables = _ctx_dft_tables(Lc)
    X = jnp.concatenate([x, ctx], axis=1)
    qk_scale = HEAD_DIM ** -0.5
    is_ctx = (jnp.arange(T) >= L).astype(jnp.int32)

    for i in range(DEPTH):
        j = i // 2
        m_l = _ada_modulation(c, ada_down[i], ada_up[i], ada_b[i])
        m_c = _ada_modulation(c_ctx, ada_down[i], ada_up[i], ada_b[i])
        mods = jnp.stack([m_l, jnp.broadcast_to(m_c[None], (B, N_MOD, D))], axis=1)

        h = norm_mod(X, norm_g[i, 0], mods[:, :, 0], mods[:, :, 1])
        hf = h.reshape(B * T, D)
        if i % 2 == 0:
            proj = matmul(hf, ev_w_in[j].astype(bf16)).reshape(B, T, -1)
            v_col0 = HY_WIDTH + WIN_Q + WIN_KV
            q = head_norm_rope(proj, HY_WIDTH, WIN_Q, ev_qk_g[j, 0], cos, sin, qk_scale)
            k = head_norm_rope(proj, HY_WIDTH + WIN_Q, WIN_KV, ev_qk_g[j, 1], cos, sin, 1.0)
            filt_args = (ev_filt_w_in[j], ev_filt_w_hid[j], ev_filt_b[j], ev_filt_freq[j], ev_filt_w_out[j])
            filt = _implicit_filters(L, *filt_args).reshape(FFT_R, FFT_R, HY_ORDER * HY_CH)
            filt = jnp.pad(filt, ((0, 0), (0, Z_PITCH - FFT_R), (0, 0))).reshape(FFT_R * Z_PITCH, HY_ORDER * HY_CH)
            spec = filter_spectrum(filt, tables[0], tables[2])
            hy_l = hyena_conv(proj, ev_conv_w[j], ev_conv_b[j], ev_hy_skip[j], spec, tables, n_lat=L)
            hy_c = hyena_ctx(proj, ev_conv_w[j], ev_conv_b[j], ev_hy_skip[j], _implicit_filters(Lc, *filt_args),
                             ctx_tables, n_lat=L, n_ctx=Lc)
            at_l = window_attention(q, k, proj, ev_sink[j], n_ctx=Lc, v_col0=v_col0)
            at_c = _context_sink_attention(q[:, L:], k[:, L:], proj[:, L:, v_col0:], ev_sink[j])
            mix = jnp.concatenate(
                [jnp.concatenate([hy_l, at_l], axis=-1),
                 jnp.concatenate([hy_c, at_c.astype(bf16)], axis=-1)], axis=1)
            w_out = ev_w_out[j]
        else:
            lam_init = 0.8 - 0.6 * math.exp(-0.3 * i)
            qkv = matmul(hf, od_w_qkv[j].astype(bf16)).reshape(B, T, -1)
            q = head_norm_rope(qkv, 0, DIFF_Q, od_qk_g[j, 0], cos, sin, DIFF_DIM ** -0.5 * math.log2(math.e), nh=8)
            k = head_norm_rope(qkv, DIFF_Q, DIFF_Q, od_qk_g[j, 1], cos, sin, 1.0, nh=8)
            lp = od_lambda[j].astype(jnp.float32)
            lam = jnp.exp(jnp.sum(lp[0] * lp[1])) - jnp.exp(jnp.sum(lp[2] * lp[3])) + lam_init
            attn = functools.partial(diff_attention, q, k, qkv, lam, od_subln_g[j], v_col0=2 * DIFF_Q,
                                     out_scale=1.0 - lam_init)
            o_l = attn(n_q=L, q_blk0=0, n_keys=T, key_blk=0, tq=512, tk=T // 6)
            o_c = attn(n_q=Lc, q_blk0=L // Lc, n_keys=Lc, key_blk=L // Lc, tq=Lc, tk=Lc)
            mix = jnp.concatenate([o_l, o_c], axis=1)
            w_out = od_w_out[j]

        X = matmul_resgate(mix.reshape(B * T, D), w_out.astype(bf16), X, mods[:, :, 2], n_lat=L)
        h2 = norm_mod(X, norm_g[i, 1], mods[:, :, 3], mods[:, :, 4]).reshape(B * T, D)
        if i % 2 == 0:
            gact = matmul_swiglu(h2, ev_ffn_w1[j].astype(bf16), ev_ffn_w3[j].astype(bf16))
            X = matmul_resgate(gact, ev_ffn_w2[j].astype(bf16), X, mods[:, :, 5], n_lat=L, tk=2048)
        else:
            f = moe_layer(h2, od_router_w[j], od_router_b[j], od_moe_w1[j].astype(bf16),
                          od_moe_w3[j].astype(bf16), od_moe_w2[j].astype(bf16))
            X = X + mods[:, :, 5][:, is_ctx] * f.reshape(B, T, D)
    return X[:, :L]
```

```python
import functools
import math

import jax
import jax.numpy as jnp
from jax import lax
from jax.experimental import pallas as pl
from jax.experimental.pallas import tpu as pltpu

D_MODEL = 4096
DEPTH = 4
GRID_W = 64
N_MOD = 6
EPS = 1e-6
NEG_INF = -1e30
HEAD_DIM = 128
ROPE_BASE = 10000.0
BLOCK = 128
WINDOW = 128
HY_CH = D_MODEL // 2
HY_ORDER = 2
HY_SHORT = 3
HY_EMB = 33
HY_BANDS = (HY_EMB - 1) // 2
HY_FILTER_HIDDEN_LAYERS = 2
HY_DIRS = 2
HY_FAST_DECAY = 0.3
HY_SLOW_DECAY = 1.5
HY_DECAY_TARGET = 1e-2
HY_WIDTH = (HY_ORDER + 1) * HY_CH
WIN_HEADS = (D_MODEL - HY_CH) // HEAD_DIM
WIN_KV_HEADS = WIN_HEADS // 4
WIN_GROUP = WIN_HEADS // WIN_KV_HEADS
WIN_Q = WIN_HEADS * HEAD_DIM
WIN_KV = WIN_KV_HEADS * HEAD_DIM
DIFF_DIM = 128
DIFF_HEADS = D_MODEL // (2 * DIFF_DIM)
DIFF_Q = DIFF_HEADS * 2 * DIFF_DIM
N_EXPERTS = 8
TOP_K = 2

VMEM_LIMIT_BYTES = 56 * 1024 * 1024
ROW_TILE = 768
NORM_TILE = 256
MOE_TILE = 512

HI = lax.Precision.HIGHEST


def _cparams(sem):
    return pltpu.CompilerParams(dimension_semantics=sem, vmem_limit_bytes=VMEM_LIMIT_BYTES)


def _norm_mod_kernel(x_ref, g_ref, shift_ref, scale_ref, o_ref):
    x = x_ref[...]
    y = x * lax.rsqrt(jnp.mean(x * x, axis=-1, keepdims=True) + EPS)
    y = y * g_ref[...]
    o_ref[...] = (y * (1.0 + scale_ref[...]) + shift_ref[...]).astype(o_ref.dtype)


def norm_mod(x, g, shift, scale):
    B, T, D = x.shape
    nt = T // NORM_TILE
    mod_spec = pl.BlockSpec((None, None, 1, D), lambda b, t: (b, t // (nt - 1), 0, 0))
    return pl.pallas_call(
        _norm_mod_kernel,
        grid=(B, nt),
        in_specs=[
            pl.BlockSpec((None, NORM_TILE, D), lambda b, t: (b, t, 0)),
            pl.BlockSpec((1, D), lambda b, t: (0, 0)),
            mod_spec, mod_spec,
        ],
        out_specs=pl.BlockSpec((None, NORM_TILE, D), lambda b, t: (b, t, 0)),
        out_shape=jax.ShapeDtypeStruct((B, T, D), jnp.bfloat16),
        compiler_params=_cparams(("parallel", "parallel")),
        name="norm_mod",
    )(x, g.reshape(1, D), shift.reshape(B, 2, 1, D), scale.reshape(B, 2, 1, D))


def _mm_kernel(a_ref, b_ref, o_ref, acc_ref, *, nk):
    k = pl.program_id(2)
    part = jnp.dot(a_ref[...], b_ref[...], preferred_element_type=jnp.float32)
    if nk == 1:
        o_ref[...] = part.astype(o_ref.dtype)
        return

    @pl.when(k == 0)
    def _():
        acc_ref[...] = part

    @pl.when(k > 0)
    def _():
        acc_ref[...] += part

    @pl.when(k == nk - 1)
    def _():
        o_ref[...] = acc_ref[...].astype(o_ref.dtype)


def matmul(a, b, *, out_dtype=jnp.bfloat16, tm=ROW_TILE, tn=512, tk=None):
    M, K = a.shape
    _, N = b.shape
    tk = K if tk is None else tk
    nk = K // tk
    return pl.pallas_call(
        functools.partial(_mm_kernel, nk=nk),
        grid=(M // tm, N // tn, nk),
        in_specs=[pl.BlockSpec((tm, tk), lambda i, j, k: (i, k)),
                  pl.BlockSpec((tk, tn), lambda i, j, k: (k, j))],
        out_specs=pl.BlockSpec((tm, tn), lambda i, j, k: (i, j)),
        out_shape=jax.ShapeDtypeStruct((M, N), out_dtype),
        scratch_shapes=[pltpu.VMEM((tm, tn), jnp.float32)],
        compiler_params=_cparams(("parallel", "parallel", "arbitrary")),
        name="matmul",
    )(a, b)


def _mm_swiglu_kernel(a_ref, w1_ref, w3_ref, o_ref):
    a = a_ref[...]
    h1 = jnp.dot(a, w1_ref[...], preferred_element_type=jnp.float32)
    h3 = jnp.dot(a, w3_ref[...], preferred_element_type=jnp.float32)
    o_ref[...] = (h1 * jax.nn.sigmoid(h1) * h3).astype(o_ref.dtype)


def matmul_swiglu(a, w1, w3, *, tm=ROW_TILE, tn=512):
    M, K = a.shape
    _, N = w1.shape
    return pl.pallas_call(
        _mm_swiglu_kernel,
        grid=(M // tm, N // tn),
        in_specs=[pl.BlockSpec((tm, K), lambda i, j: (i, 0)),
                  pl.BlockSpec((K, tn), lambda i, j: (0, j)),
                  pl.BlockSpec((K, tn), lambda i, j: (0, j))],
        out_specs=pl.BlockSpec((tm, tn), lambda i, j: (i, j)),
        out_shape=jax.ShapeDtypeStruct((M, N), jnp.bfloat16),
        compiler_params=_cparams(("parallel", "parallel")),
        name="matmul_swiglu",
    )(a, w1, w3)


def _mm_resgate_kernel(a_ref, b_ref, res_ref, gate_ref, o_ref, acc_ref, *, nk, tm, tiles_per_batch, n_lat):
    i = pl.program_id(0)
    k = pl.program_id(2)
    part = jnp.dot(a_ref[...], b_ref[...], preferred_element_type=jnp.float32)

    def finish(acc):
        row = (i % tiles_per_batch) * tm + lax.broadcasted_iota(jnp.int32, (tm, 1), 0)
        gate = jnp.where(row < n_lat, gate_ref[0:1, :], gate_ref[1:2, :])
        o_ref[...] = res_ref[...] + gate * acc

    if nk == 1:
        finish(part)
        return

    @pl.when(k == 0)
    def _():
        acc_ref[...] = part

    @pl.when(k > 0)
    def _():
        acc_ref[...] += part

    @pl.when(k == nk - 1)
    def _():
        finish(acc_ref[...])


def matmul_resgate(a, b, res, gate, *, n_lat, tm=ROW_TILE, tn=512, tk=None):
    B, T, N = res.shape
    M, K = a.shape
    tk = K if tk is None else tk
    nk = K // tk
    tpb = T // tm
    kern = functools.partial(_mm_resgate_kernel, nk=nk, tm=tm, tiles_per_batch=tpb, n_lat=n_lat)
    out = pl.pallas_call(
        kern,
        grid=(M // tm, N // tn, nk),
        in_specs=[pl.BlockSpec((tm, tk), lambda i, j, k: (i, k)),
                  pl.BlockSpec((tk, tn), lambda i, j, k: (k, j)),
                  pl.BlockSpec((tm, tn), lambda i, j, k: (i, j)),
                  pl.BlockSpec((None, 2, tn), lambda i, j, k: (i // tpb, 0, j))],
        out_specs=pl.BlockSpec((tm, tn), lambda i, j, k: (i, j)),
        out_shape=jax.ShapeDtypeStruct((M, N), jnp.float32),
        scratch_shapes=[pltpu.VMEM((tm, tn), jnp.float32)],
        compiler_params=_cparams(("parallel", "parallel", "arbitrary")),
        name="matmul_resgate",
    )(a, b, res.reshape(M, N), gate)
    return out.reshape(B, T, N)


def _mm2_resgate_kernel(a1_ref, a2_ref, b_ref, res_ref, gate_ref, o_ref, *, tm, tiles_per_batch, n_lat):
    i = pl.program_id(0)
    k1 = a1_ref.shape[1]
    acc = (jnp.dot(a1_ref[...], b_ref[:k1, :], preferred_element_type=jnp.float32)
           + jnp.dot(a2_ref[...], b_ref[k1:, :], preferred_element_type=jnp.float32))
    row = (i % tiles_per_batch) * tm + lax.broadcasted_iota(jnp.int32, (tm, 1), 0)
    gate = jnp.where(row < n_lat, gate_ref[0:1, :], gate_ref[1:2, :])
    o_ref[...] = res_ref[...] + gate * acc


def matmul2_resgate(a1, a2, b, res, gate, *, n_lat, tm=ROW_TILE, tn=512):
    B, T, N = res.shape
    M, K1 = a1.shape
    K2 = a2.shape[1]
    tpb = T // tm
    out = pl.pallas_call(
        functools.partial(_mm2_resgate_kernel, tm=tm, tiles_per_batch=tpb, n_lat=n_lat),
        grid=(M // tm, N // tn),
        in_specs=[pl.BlockSpec((tm, K1), lambda i, j: (i, 0)),
                  pl.BlockSpec((tm, K2), lambda i, j: (i, 0)),
                  pl.BlockSpec((K1 + K2, tn), lambda i, j: (0, j)),
                  pl.BlockSpec((tm, tn), lambda i, j: (i, j)),
                  pl.BlockSpec((None, 2, tn), lambda i, j: (i // tpb, 0, j))],
        out_specs=pl.BlockSpec((tm, tn), lambda i, j: (i, j)),
        out_shape=jax.ShapeDtypeStruct((M, N), jnp.float32),
        compiler_params=_cparams(("parallel", "parallel")),
        name="matmul2_resgate",
    )(a1, a2, b, res.reshape(M, N), gate)
    return out.reshape(B, T, N)


def _moe_combine_kernel(x_ref, y0_ref, y1_ref, gate_ref, o_ref):
    y = y0_ref[...].astype(jnp.float32) + y1_ref[...].astype(jnp.float32)
    o_ref[...] = x_ref[...] + gate_ref[...] * y


def moe_combine(x, y0, y1, gate):
    B, T, D = x.shape
    nt = T // NORM_TILE
    row = pl.BlockSpec((None, NORM_TILE, D), lambda b, t: (b, t, 0))
    return pl.pallas_call(
        _moe_combine_kernel,
        grid=(B, nt),
        in_specs=[row, row, row, pl.BlockSpec((None, None, 1, D), lambda b, t: (b, t // (nt - 1), 0, 0))],
        out_specs=row,
        out_shape=jax.ShapeDtypeStruct((B, T, D), jnp.float32),
        compiler_params=_cparams(("parallel", "parallel")),
        name="moe_combine",
    )(x, y0, y1, gate.reshape(B, 2, 1, D))


def _window_attn_kernel(sink_ref, q_ref, kc_ref, kp_ref, ko_ref, kn_ref, vc_ref, vp_ref, vo_ref, vn_ref, o_ref,
                        *, n_lat, n_ctx):
    n = pl.program_id(1)
    h = pl.program_id(2)
    G = WIN_GROUP
    q = q_ref[...]
    qs = jnp.concatenate([q[:, g * HEAD_DIM:(g + 1) * HEAD_DIM] for g in range(G)], axis=0)
    k = jnp.concatenate([kc_ref[...], kp_ref[...], ko_ref[...], kn_ref[...]], axis=0)
    v = jnp.concatenate([vc_ref[...], vp_ref[...], vo_ref[...], vn_ref[...]], axis=0)
    s = lax.dot_general(qs, k, (((1,), (1,)), ((), ())), preferred_element_type=jnp.float32)
    nk = n_ctx + 3 * BLOCK
    col = lax.broadcasted_iota(jnp.int32, (G * BLOCK, nk), 1)
    qpos = n * BLOCK + lax.broadcasted_iota(jnp.int32, (G * BLOCK, nk), 0) % BLOCK
    kpos = (n - 1) * BLOCK + (col - n_ctx)
    valid = (col < n_ctx) | ((jnp.abs(qpos - kpos) <= WINDOW) & (kpos >= 0) & (kpos < n_lat) & (qpos < n_lat))
    s = jnp.where(valid, s, NEG_INF)
    sink = jnp.concatenate(
        [jnp.full((BLOCK, 1), sink_ref[h * G + g], jnp.float32) for g in range(G)], axis=0)
    m = jnp.maximum(jnp.max(s, axis=-1, keepdims=True), sink)
    e = jnp.exp(s - m)
    denom = jnp.sum(e, axis=-1, keepdims=True) + jnp.exp(sink - m)
    p = (e / denom).astype(v.dtype)
    o = jnp.dot(p, v, preferred_element_type=jnp.float32)
    for g in range(G):
        o_ref[:, g * HEAD_DIM:(g + 1) * HEAD_DIM] = o[g * BLOCK:(g + 1) * BLOCK, :].astype(o_ref.dtype)


def window_attention(q, k, v, sink, *, n_ctx, v_col0=0):
    B, T, _ = q.shape
    L = T - n_ctx
    nb = L // BLOCK
    kv_blk = lambda f: pl.BlockSpec((None, BLOCK, HEAD_DIM), f)
    ctx_blk = lambda off: pl.BlockSpec((None, n_ctx, HEAD_DIM), lambda b, n, h: (b, L // n_ctx, off + h))
    prev_blk = lambda off: kv_blk(lambda b, n, h: (b, jnp.maximum(n - 1, 0), off + h))
    own_blk = lambda off: kv_blk(lambda b, n, h: (b, n, off + h))
    next_blk = lambda off: kv_blk(lambda b, n, h: (b, jnp.minimum(n + 1, nb - 1), off + h))
    voff = v_col0 // HEAD_DIM
    return pl.pallas_call(
        functools.partial(_window_attn_kernel, n_lat=L, n_ctx=n_ctx),
        grid=(B, T // BLOCK, WIN_KV_HEADS),
        in_specs=[pl.BlockSpec(memory_space=pltpu.SMEM),
                  pl.BlockSpec((None, BLOCK, WIN_GROUP * HEAD_DIM), lambda b, n, h: (b, n, h)),
                  ctx_blk(0), prev_blk(0), own_blk(0), next_blk(0),
                  ctx_blk(voff), prev_blk(voff), own_blk(voff), next_blk(voff)],
        out_specs=pl.BlockSpec((None, BLOCK, WIN_GROUP * HEAD_DIM), lambda b, n, h: (b, n, h)),
        out_shape=jax.ShapeDtypeStruct((B, T, WIN_Q), jnp.bfloat16),
        compiler_params=_cparams(("parallel", "parallel", "parallel")),
        name="window_attention",
    )(sink.astype(jnp.float32), q, k, k, k, k, v, v, v, v)


def _diff_attn_kernel(lam_ref, q_ref, k_ref, v_ref, g_ref, buf_ref, o_ref, m_ref, l_ref, acc_ref,
                      *, n_keys, tk, out_scale):
    del buf_ref
    m_ref[...] = jnp.full(m_ref.shape, NEG_INF, jnp.float32)
    l_ref[...] = jnp.zeros(l_ref.shape, jnp.float32)
    acc_ref[...] = jnp.zeros(acc_ref.shape, jnp.float32)

    def step(j, carry):
        ks = pl.multiple_of(j * tk, tk)
        v = v_ref[pl.ds(ks, tk), :]
        scores = []
        for sub in range(2):
            q = q_ref[:, sub * DIFF_DIM:(sub + 1) * DIFF_DIM]
            k = k_ref[pl.ds(ks, tk), sub * DIFF_DIM:(sub + 1) * DIFF_DIM]
            scores.append(lax.dot_general(q, k, (((1,), (1,)), ((), ())), preferred_element_type=jnp.float32))
        for sub in range(2):
            s = scores[sub]
            m_old = m_ref[sub]
            m_new = jnp.maximum(m_old, jnp.max(s, axis=-1, keepdims=True))
            alpha = jnp.exp2(m_old - m_new)
            p = jnp.exp2(s - m_new)
            l_ref[sub] = alpha * l_ref[sub] + jnp.sum(p, axis=-1, keepdims=True)
            acc_ref[sub] = alpha * acc_ref[sub] + jnp.dot(p.astype(v.dtype), v, preferred_element_type=jnp.float32)
            m_ref[sub] = m_new
        return carry

    lax.fori_loop(0, n_keys // tk, step, 0)
    lam = lam_ref[0]
    o = acc_ref[0] / l_ref[0] - lam * (acc_ref[1] / l_ref[1])
    o = o * lax.rsqrt(jnp.mean(o * o, axis=-1, keepdims=True) + EPS)
    o_ref[...] = (o * g_ref[...] * out_scale).astype(o_ref.dtype)


def diff_attention(q, k, v, lam, subln_g, out_buf, *, n_q, q_blk0, n_keys, key_blk, tq, tk, out_scale, v_col0=0):
    B = q.shape[0]
    W = 2 * DIFF_DIM
    voff = v_col0 // W
    return pl.pallas_call(
        functools.partial(_diff_attn_kernel, n_keys=n_keys, tk=tk, out_scale=out_scale),
        grid=(B, DIFF_HEADS, n_q // tq),
        in_specs=[pl.BlockSpec(memory_space=pltpu.SMEM),
                  pl.BlockSpec((None, tq, W), lambda b, h, i: (b, q_blk0 + i, h)),
                  pl.BlockSpec((None, n_keys, W), lambda b, h, i: (b, key_blk, h)),
                  pl.BlockSpec((None, n_keys, W), lambda b, h, i: (b, key_blk, voff + h)),
                  pl.BlockSpec((1, W), lambda b, h, i: (0, 0)),
                  pl.BlockSpec(memory_space=pl.ANY)],
        out_specs=pl.BlockSpec((None, tq, W), lambda b, h, i: (b, q_blk0 + i, h)),
        scratch_shapes=[pltpu.VMEM((2, tq, 1), jnp.float32),
                        pltpu.VMEM((2, tq, 1), jnp.float32),
                        pltpu.VMEM((2, tq, W), jnp.float32)],
        out_shape=jax.ShapeDtypeStruct(out_buf.shape, out_buf.dtype),
        input_output_aliases={5: 0},
        compiler_params=_cparams(("parallel", "parallel", "arbitrary")),
        name="diff_attention",
    )(lam.reshape(1).astype(jnp.float32), q, k, v, subln_g.reshape(1, W).astype(jnp.float32), out_buf)


def _moe_up_kernel(te_ref, tv_ref, x_ref, w1_ref, w3_ref, o_ref):
    i = pl.program_id(1)

    @pl.when(tv_ref[i] > 0)
    def _():
        x = x_ref[...]
        h1 = jnp.dot(x, w1_ref[...], preferred_element_type=jnp.float32)
        h3 = jnp.dot(x, w3_ref[...], preferred_element_type=jnp.float32)
        o_ref[...] = (h1 * jax.nn.sigmoid(h1) * h3).astype(o_ref.dtype)

    @pl.when(tv_ref[i] == 0)
    def _():
        o_ref[...] = jnp.zeros(o_ref.shape, o_ref.dtype)


def _moe_down_kernel(te_ref, tv_ref, g_ref, w2_ref, rg_ref, o_ref):
    i = pl.program_id(1)

    @pl.when(tv_ref[i] > 0)
    def _():
        y = jnp.dot(g_ref[...], w2_ref[...], preferred_element_type=jnp.float32)
        o_ref[...] = (rg_ref[...] * y).astype(o_ref.dtype)

    @pl.when(tv_ref[i] == 0)
    def _():
        o_ref[...] = jnp.zeros(o_ref.shape, o_ref.dtype)


def moe_experts(xs, w1, w3, w2, tile_expert, tile_valid, row_gate, *, tn_up=768, tn_down=1024):
    P, D = xs.shape
    F = w1.shape[2]
    tm = MOE_TILE
    up = pl.pallas_call(
        _moe_up_kernel,
        grid_spec=pltpu.PrefetchScalarGridSpec(
            num_scalar_prefetch=2,
            grid=(F // tn_up, P // tm),
            in_specs=[pl.BlockSpec((tm, D), lambda j, i, te, tv: (i, 0)),
                      pl.BlockSpec((None, D, tn_up), lambda j, i, te, tv: (te[i], 0, j)),
                      pl.BlockSpec((None, D, tn_up), lambda j, i, te, tv: (te[i], 0, j))],
            out_specs=pl.BlockSpec((tm, tn_up), lambda j, i, te, tv: (i, j)),
        ),
        out_shape=jax.ShapeDtypeStruct((P, F), jnp.bfloat16),
        compiler_params=_cparams(("parallel", "arbitrary")),
        name="moe_up",
    )(tile_expert, tile_valid, xs, w1, w3)
    return pl.pallas_call(
        _moe_down_kernel,
        grid_spec=pltpu.PrefetchScalarGridSpec(
            num_scalar_prefetch=2,
            grid=(D // tn_down, P // tm),
            in_specs=[pl.BlockSpec((tm, F), lambda j, i, te, tv: (i, 0)),
                      pl.BlockSpec((None, F, tn_down), lambda j, i, te, tv: (te[i], 0, j)),
                      pl.BlockSpec((tm, 1), lambda j, i, te, tv: (i, 0))],
            out_specs=pl.BlockSpec((tm, tn_down), lambda j, i, te, tv: (i, j)),
        ),
        out_shape=jax.ShapeDtypeStruct((P, D), jnp.bfloat16),
        compiler_params=_cparams(("parallel", "arbitrary")),
        name="moe_down",
    )(tile_expert, tile_valid, up, w2, row_gate)


def moe_layer(h2, router_w, router_b, w1, w3, w2):
    N, D = h2.shape
    tm = MOE_TILE
    logits = jnp.dot(h2.astype(jnp.float32), router_w, precision=HI) + router_b
    top_v, top_i = lax.top_k(logits, TOP_K)
    gates = jax.nn.softmax(top_v, axis=-1)
    A = N * TOP_K
    e_flat = top_i.reshape(A).astype(jnp.int32)
    order = jnp.argsort(e_flat, stable=True)
    e_sorted = e_flat[order]
    counts = jnp.sum(e_flat[:, None] == jnp.arange(N_EXPERTS, dtype=jnp.int32)[None, :], axis=0).astype(jnp.int32)
    padded = ((counts + tm - 1) // tm) * tm
    start_unpadded = jnp.cumsum(counts) - counts
    start_padded = jnp.cumsum(padded) - padded
    dest_sorted = start_padded[e_sorted] + (jnp.arange(A, dtype=jnp.int32) - start_unpadded[e_sorted])
    P = A + N_EXPERTS * tm
    row_token = jnp.zeros((P,), jnp.int32).at[dest_sorted].set((order // TOP_K).astype(jnp.int32))
    row_gate = jnp.zeros((P,), jnp.float32).at[dest_sorted].set(gates.reshape(A)[order])
    pos = jnp.zeros((A,), jnp.int32).at[order].set(dest_sorted).reshape(N, TOP_K)
    tile_start = jnp.arange(P // tm, dtype=jnp.int32) * tm
    ends = jnp.cumsum(padded)
    tile_expert = jnp.minimum(jnp.sum(tile_start[:, None] >= ends[None, :], axis=1), N_EXPERTS - 1).astype(jnp.int32)
    tile_valid = (tile_start < ends[-1]).astype(jnp.int32)
    xs = jnp.take(h2, row_token, axis=0)
    y = moe_experts(xs, w1, w3, w2, tile_expert, tile_valid, row_gate.reshape(P, 1))
    return jnp.take(y, pos[:, 0], axis=0), jnp.take(y, pos[:, 1], axis=0)


FFT_R = 128
FFT_K1 = 72
Z_PITCH = FFT_R + 8
S1_PITCH = 2 * FFT_K1 + 8
S2_PITCH = 2 * FFT_R + 8
HY_LANES = 128
FFT_UNROLL = 16


def _dft_tables():
    R, K1 = FFT_R, FFT_K1
    N = R * R
    i32 = jnp.int32
    b = jnp.arange(R, dtype=i32)[:, None, None]
    k1 = jnp.arange(K1, dtype=i32)[None, :, None]
    a = jnp.arange(R, dtype=i32)[None, None, :]
    th = (2.0 * math.pi / N) * ((k1 * (R * a + b)) % N).astype(jnp.float32)
    f1 = jnp.concatenate([jnp.cos(th), -jnp.sin(th)], axis=1)
    w = jnp.where(jnp.arange(K1) > R // 2, 0.0, jnp.where((jnp.arange(K1) % (R // 2)) == 0, 1.0, 2.0)) / N
    the = jnp.swapaxes(th[:, :, :R // 2], 1, 2)
    e = jnp.concatenate([w * jnp.cos(the), -w * jnp.sin(the)], axis=2)
    k2 = jnp.arange(R, dtype=i32)
    ph = (2.0 * math.pi / R) * ((k2[:, None] * k2[None, :]) % R).astype(jnp.float32)
    c, s = jnp.cos(ph), jnp.sin(ph)
    g = jnp.block([[c, s], [-s, c]])
    ginv = jnp.block([[c, -s], [s, c]])
    bf = jnp.bfloat16
    return f1.astype(bf), e.astype(bf), g.astype(bf), ginv.astype(bf)


def _fft_stage1(src_ref, f1_ref, s1_ref, n_a):
    def body(b, carry):
        zb = src_ref[pl.ds(b, n_a, stride=Z_PITCH), :]
        s1_ref[pl.ds(pl.multiple_of(b * S1_PITCH, 8), 2 * FFT_K1), :] = jnp.dot(
            f1_ref[b], zb.astype(jnp.bfloat16), preferred_element_type=jnp.float32)
        return carry
    lax.fori_loop(0, FFT_R, body, 0, unroll=FFT_UNROLL)


def _fft_stage2(s1_ref, g_ref, k1):
    are = s1_ref[pl.ds(k1, FFT_R, stride=S1_PITCH), :]
    aim = s1_ref[pl.ds(FFT_K1 + k1, FFT_R, stride=S1_PITCH), :]
    r = jnp.concatenate([are, aim], axis=0).astype(jnp.bfloat16)
    return jnp.dot(g_ref[...], r, preferred_element_type=jnp.float32)


def _filter_fft_kernel(f_ref, f1_ref, g_ref, h_ref, s1_ref):
    _fft_stage1(f_ref, f1_ref, s1_ref, FFT_R)

    def body(k1, carry):
        h_ref[k1] = _fft_stage2(s1_ref, g_ref, k1).astype(h_ref.dtype)
        return carry
    lax.fori_loop(0, FFT_K1, body, 0, unroll=FFT_UNROLL)


def filter_spectrum(filt_padded, f1, g):
    rows, width = filt_padded.shape
    nt = width // HY_LANES
    once = pl.Buffered(1)
    return pl.pallas_call(
        _filter_fft_kernel,
        grid=(nt,),
        in_specs=[pl.BlockSpec((rows, HY_LANES), lambda c: (0, c)),
                  pl.BlockSpec(f1.shape, lambda c: (0, 0, 0), pipeline_mode=once),
                  pl.BlockSpec(g.shape, lambda c: (0, 0), pipeline_mode=once)],
        out_specs=pl.BlockSpec((None, FFT_K1, 2 * FFT_R, HY_LANES), lambda c: (c, 0, 0, 0)),
        out_shape=jax.ShapeDtypeStruct((nt, FFT_K1, 2 * FFT_R, HY_LANES), jnp.bfloat16),
        scratch_shapes=[pltpu.VMEM((FFT_R * S1_PITCH, HY_LANES), jnp.float32)],
        compiler_params=_cparams(("arbitrary",)),
        name="hyena_filter_fft",
    )(filt_padded, f1, g)


def _short_conv_chunks(x_ref, w_ref, b_ref, dst_ref, n_chunks, pitch=Z_PITCH):
    R = FFT_R
    w = w_ref[...]
    bias = b_ref[...]
    row = lax.broadcasted_iota(jnp.int32, (R, HY_LANES), 0)
    for a in range(n_chunks):
        xc = x_ref[a * R:(a + 1) * R, :].astype(jnp.float32)
        prev = pltpu.roll(xc, 1, axis=0)
        nxt = pltpu.roll(xc, R - 1, axis=0)
        if a > 0:
            last = x_ref[a * R - 16:a * R, :].astype(jnp.float32)[15:16, :]
        else:
            last = jnp.zeros((1, HY_LANES), jnp.float32)
        if a < n_chunks - 1:
            first = x_ref[(a + 1) * R:(a + 1) * R + 16, :].astype(jnp.float32)[0:1, :]
        else:
            first = jnp.zeros((1, HY_LANES), jnp.float32)
        prev = jnp.where(row == 0, last, prev)
        nxt = jnp.where(row == R - 1, first, nxt)
        dst_ref[a * pitch:a * pitch + R, :] = bias + prev * w[0:1, :] + xc * w[1:2, :] + nxt * w[2:3, :]


def _hyena_conv_kernel(v_ref, x_ref, wv_ref, bv_ref, wx_ref, bx_ref, skip_ref, h_ref, f1_ref, e_ref, g_ref, gi_ref,
                       buf_ref, o_ref, z_ref, gate_ref, s1_ref, s2_ref, *, n_chunks):
    del buf_ref
    n = pl.program_id(2)
    R, K1 = FFT_R, FFT_K1

    @pl.when(n == 0)
    def _():
        _short_conv_chunks(v_ref, wv_ref, bv_ref, z_ref, n_chunks)

    _short_conv_chunks(x_ref, wx_ref, bx_ref, gate_ref, n_chunks)

    _fft_stage1(z_ref, f1_ref, s1_ref, n_chunks)

    def mid(k1, carry):
        x = _fft_stage2(s1_ref, g_ref, k1)
        h = h_ref[k1].astype(jnp.float32)
        xr, xi, hr, hi = x[:R], x[R:], h[:R], h[R:]
        p = jnp.concatenate([xr * hr - xi * hi, xr * hi + xi * hr], axis=0).astype(jnp.bfloat16)
        s2_ref[pl.ds(pl.multiple_of(k1 * S2_PITCH, 8), 2 * R), :] = jnp.dot(
            gi_ref[...], p, preferred_element_type=jnp.float32)
        return carry
    lax.fori_loop(0, K1, mid, 0, unroll=FFT_UNROLL)

    skip = skip_ref[...]

    def last(b, carry):
        qre = s2_ref[pl.ds(b, K1, stride=S2_PITCH), :]
        qim = s2_ref[pl.ds(R + b, K1, stride=S2_PITCH), :]
        r = jnp.concatenate([qre, qim], axis=0).astype(jnp.bfloat16)
        conv = jnp.dot(e_ref[b], r, preferred_element_type=jnp.float32)
        zold = z_ref[pl.ds(b, n_chunks, stride=Z_PITCH), :]
        gate = gate_ref[pl.ds(b, n_chunks, stride=Z_PITCH), :]
        z_ref[pl.ds(b, n_chunks, stride=Z_PITCH), :] = gate * (conv + zold * skip)
        return carry
    lax.fori_loop(0, R, last, 0, unroll=FFT_UNROLL)

    @pl.when(n == HY_ORDER - 1)
    def _():
        for a in range(n_chunks):
            o_ref[a * R:(a + 1) * R, :] = z_ref[a * Z_PITCH:a * Z_PITCH + R, :].astype(o_ref.dtype)


def hyena_conv(proj, conv_w, conv_b, skip, spec, tables, out_buf, *, n_lat):
    f1, e, g, ginv = tables
    B = proj.shape[0]
    n_chunks = n_lat // FFT_R
    nt = HY_CH // HY_LANES
    once = pl.Buffered(1)
    grp = lambda n: (1 + n) * nt
    f1h = f1[:, :, :n_chunks]
    return pl.pallas_call(
        functools.partial(_hyena_conv_kernel, n_chunks=n_chunks),
        grid=(B, nt, HY_ORDER),
        in_specs=[pl.BlockSpec((None, n_lat, HY_LANES), lambda b, c, n: (b, 0, c)),
                  pl.BlockSpec((None, n_lat, HY_LANES), lambda b, c, n: (b, 0, grp(n) + c)),
                  pl.BlockSpec((HY_SHORT, HY_LANES), lambda b, c, n: (0, c)),
                  pl.BlockSpec((1, HY_LANES), lambda b, c, n: (0, c)),
                  pl.BlockSpec((HY_SHORT, HY_LANES), lambda b, c, n: (0, grp(n) + c)),
                  pl.BlockSpec((1, HY_LANES), lambda b, c, n: (0, grp(n) + c)),
                  pl.BlockSpec((None, 1, HY_LANES), lambda b, c, n: (n, 0, c)),
                  pl.BlockSpec((None, FFT_K1, 2 * FFT_R, HY_LANES), lambda b, c, n: (n * nt + c, 0, 0, 0),
                               pipeline_mode=once),
                  pl.BlockSpec(f1h.shape, lambda b, c, n: (0, 0, 0), pipeline_mode=once),
                  pl.BlockSpec(e.shape, lambda b, c, n: (0, 0, 0), pipeline_mode=once),
                  pl.BlockSpec(g.shape, lambda b, c, n: (0, 0), pipeline_mode=once),
                  pl.BlockSpec(ginv.shape, lambda b, c, n: (0, 0), pipeline_mode=once),
                  pl.BlockSpec(memory_space=pl.ANY)],
        out_specs=pl.BlockSpec((None, n_lat, HY_LANES), lambda b, c, n: (b, 0, c)),
        out_shape=jax.ShapeDtypeStruct(out_buf.shape, out_buf.dtype),
        input_output_aliases={12: 0},
        scratch_shapes=[pltpu.VMEM((n_chunks * Z_PITCH, HY_LANES), jnp.float32),
                        pltpu.VMEM((n_chunks * Z_PITCH, HY_LANES), jnp.float32),
                        pltpu.VMEM((FFT_R * S1_PITCH, HY_LANES), jnp.float32),
                        pltpu.VMEM((FFT_K1 * S2_PITCH, HY_LANES), jnp.float32)],
        compiler_params=_cparams(("parallel", "parallel", "arbitrary")),
        name="hyena_conv",
    )(proj, proj, conv_w, conv_b.reshape(1, -1), conv_w, conv_b.reshape(1, -1),
      skip.reshape(HY_ORDER, 1, HY_CH), spec, f1h, e, g, ginv, out_buf)


def _ctx_dft_tables(n_ctx):
    N = 2 * n_ctx
    nk = -(-(n_ctx + 1) // 16) * 16
    k = jnp.arange(nk, dtype=jnp.int32)[:, None]
    t = jnp.arange(N, dtype=jnp.int32)[None, :]
    th = (2.0 * math.pi / N) * ((k * t) % N).astype(jnp.float32)
    fwd = jnp.concatenate([jnp.cos(th), -jnp.sin(th)], axis=0)
    w = jnp.where(k > n_ctx, 0.0, jnp.where((k % n_ctx) == 0, 1.0, 2.0)) / N
    inv = jnp.concatenate([(w * jnp.cos(th[:, :n_ctx])).T, (-w * jnp.sin(th[:, :n_ctx])).T], axis=1)
    return fwd.astype(jnp.bfloat16), inv.astype(jnp.bfloat16)


def _hyena_ctx_kernel(v_ref, x1_ref, x2_ref, wv_ref, bv_ref, w1_ref, b1_ref, w2_ref, b2_ref, skip_ref,
                      f0_ref, f1_ref, fwd_ref, inv_ref, buf_ref, o_ref, z_ref, g1_ref, g2_ref, *, n_ctx):
    del buf_ref
    nc = n_ctx // FFT_R
    _short_conv_chunks(v_ref, wv_ref, bv_ref, z_ref, nc, pitch=FFT_R)
    _short_conv_chunks(x1_ref, w1_ref, b1_ref, g1_ref, nc, pitch=FFT_R)
    _short_conv_chunks(x2_ref, w2_ref, b2_ref, g2_ref, nc, pitch=FFT_R)
    nk = fwd_ref.shape[0] // 2
    z = z_ref[...]
    for n, (filt_ref, gate_ref) in enumerate(((f0_ref, g1_ref), (f1_ref, g2_ref))):
        h = jnp.dot(fwd_ref[...], filt_ref[...].astype(jnp.bfloat16), preferred_element_type=jnp.float32)
        x = jnp.dot(fwd_ref[:, :n_ctx], z.astype(jnp.bfloat16), preferred_element_type=jnp.float32)
        xr, xi, hr, hi = x[:nk], x[nk:], h[:nk], h[nk:]
        p = jnp.concatenate([xr * hr - xi * hi, xr * hi + xi * hr], axis=0).astype(jnp.bfloat16)
        conv = jnp.dot(inv_ref[...], p, preferred_element_type=jnp.float32)
        z = gate_ref[...] * (conv + z * skip_ref[n:n + 1, :])
    o_ref[...] = z.astype(o_ref.dtype)


def hyena_ctx(proj, conv_w, conv_b, skip, filt, tables, out_buf, *, n_lat, n_ctx):
    fwd, inv = tables
    B = proj.shape[0]
    nt = HY_CH // HY_LANES
    rb = n_lat // n_ctx
    cb2 = conv_b.reshape(1, -1)
    row = lambda g: pl.BlockSpec((None, n_ctx, HY_LANES), lambda b, c: (b, rb, g * nt + c))
    wsp = lambda g: pl.BlockSpec((HY_SHORT, HY_LANES), lambda b, c: (0, g * nt + c))
    bsp = lambda g: pl.BlockSpec((1, HY_LANES), lambda b, c: (0, g * nt + c))
    fsp = lambda n: pl.BlockSpec((2 * n_ctx, HY_LANES), lambda b, c: (0, n * nt + c))
    scr = pltpu.VMEM((n_ctx, HY_LANES), jnp.float32)
    return pl.pallas_call(
        functools.partial(_hyena_ctx_kernel, n_ctx=n_ctx),
        grid=(B, nt),
        in_specs=[row(0), row(1), row(2), wsp(0), bsp(0), wsp(1), bsp(1), wsp(2), bsp(2),
                  pl.BlockSpec((HY_ORDER, HY_LANES), lambda b, c: (0, c)),
                  fsp(0), fsp(1),
                  pl.BlockSpec(fwd.shape, lambda b, c: (0, 0)),
                  pl.BlockSpec(inv.shape, lambda b, c: (0, 0)),
                  pl.BlockSpec(memory_space=pl.ANY)],
        out_specs=pl.BlockSpec((None, n_ctx, HY_LANES), lambda b, c: (b, rb, c)),
        out_shape=jax.ShapeDtypeStruct(out_buf.shape, out_buf.dtype),
        input_output_aliases={14: 0},
        scratch_shapes=[scr, scr, scr],
        compiler_params=_cparams(("parallel", "parallel")),
        name="hyena_ctx",
    )(proj, proj, proj, conv_w, cb2, conv_w, cb2, conv_w, cb2, skip, filt, filt, fwd, inv, out_buf)


def _norm_rope_kernel(x_ref, g_ref, cos_ref, sin_ref, o_ref, *, scale, nh):
    cos = cos_ref[...]
    sin = sin_ref[...]
    g = g_ref[...]
    lane = lax.broadcasted_iota(jnp.int32, cos.shape, 1)
    lower = (lane % (HEAD_DIM // 2)) < (HEAD_DIM // 4)
    for h in range(nh):
        x = x_ref[:, h * HEAD_DIM:(h + 1) * HEAD_DIM].astype(jnp.float32)
        y = x * lax.rsqrt(jnp.mean(x * x, axis=-1, keepdims=True) + EPS) * g
        rot = jnp.where(lower, -pltpu.roll(y, HEAD_DIM - HEAD_DIM // 4, axis=1), pltpu.roll(y, HEAD_DIM // 4, axis=1))
        o_ref[:, h * HEAD_DIM:(h + 1) * HEAD_DIM] = ((y * cos + rot * sin) * scale).astype(o_ref.dtype)


def head_norm_rope(t, col0, width, g, cos, sin, scale, *, tr=ROW_TILE, nh=4):
    B, T, _ = t.shape
    wb = nh * HEAD_DIM
    c0 = col0 // wb
    return pl.pallas_call(
        functools.partial(_norm_rope_kernel, scale=scale, nh=nh),
        grid=(B, T // tr, width // wb),
        in_specs=[pl.BlockSpec((None, tr, wb), lambda b, r, j: (b, r, c0 + j)),
                  pl.BlockSpec((1, HEAD_DIM), lambda b, r, j: (0, 0)),
                  pl.BlockSpec((tr, HEAD_DIM), lambda b, r, j: (r, 0)),
                  pl.BlockSpec((tr, HEAD_DIM), lambda b, r, j: (r, 0))],
        out_specs=pl.BlockSpec((None, tr, wb), lambda b, r, j: (b, r, j)),
        out_shape=jax.ShapeDtypeStruct((B, T, width), jnp.bfloat16),
        compiler_params=_cparams(("parallel", "parallel", "parallel")),
        name="head_norm_rope",
    )(t, g.reshape(1, HEAD_DIM).astype(jnp.float32), cos, sin)


def _rope_tables(L, n_ctx):
    rows = jnp.repeat(jnp.arange(L // GRID_W), GRID_W)
    cols = jnp.tile(jnp.arange(GRID_W), L // GRID_W)
    quarter = HEAD_DIM // 4
    inv = ROPE_BASE ** (-jnp.arange(quarter, dtype=jnp.float32) / quarter)
    ar = rows.astype(jnp.float32)[:, None] * inv
    ac = cols.astype(jnp.float32)[:, None] * inv
    ang = jnp.concatenate([ar, ar, ac, ac], axis=-1)
    cos = jnp.concatenate([jnp.cos(ang), jnp.ones((n_ctx, HEAD_DIM), jnp.float32)], axis=0)
    sin = jnp.concatenate([jnp.sin(ang), jnp.zeros((n_ctx, HEAD_DIM), jnp.float32)], axis=0)
    return cos, sin


def _implicit_filters(L, w_in, w_hid, b, freq, w_out):
    f32 = jnp.float32
    t = jnp.linspace(0.0, 1.0, L, dtype=f32)[:, None]
    w = (2.0 * math.pi / L) * jnp.arange(L, dtype=f32)[:, None]
    f = jnp.linspace(1e-4, HY_BANDS - 1, HY_BANDS, dtype=f32)[None, :]
    z = jnp.concatenate([t, jnp.cos(f * w), -jnp.sin(f * w)], axis=-1)
    h = jnp.sin(freq[0] * (jnp.dot(z, w_in, precision=HI) + b[0]))
    for n in range(HY_FILTER_HIDDEN_LAYERS):
        h = jnp.sin(freq[n + 1] * (jnp.dot(h, w_hid[n], precision=HI) + b[n + 1]))
    width = HY_ORDER * HY_CH
    max_decay = math.log(HY_DECAY_TARGET) / HY_FAST_DECAY
    min_decay = math.log(HY_DECAY_TARGET) / HY_SLOW_DECAY
    deltas = jnp.abs(jnp.linspace(min_decay, max_decay, width, dtype=f32))[None, :]
    fwd = jnp.dot(h, w_out[:, :width], precision=HI) * jnp.exp(-t * deltas)
    bwd = jnp.dot(h[::-1], w_out[:, width:], precision=HI) * jnp.exp(-t[::-1] * deltas)
    return jnp.concatenate([fwd, jnp.zeros((1, width), f32), bwd[:L - 1]], axis=0)


def _ada_modulation(cv, down, up, b):
    m = jnp.dot(jnp.dot(jax.nn.silu(cv), down, precision=HI), up, precision=HI) + b
    return m.reshape(m.shape[:-1] + (N_MOD, m.shape[-1] // N_MOD))


def kernel(x, c, ctx, c_ctx, norm_g, ada_down, ada_up, ada_b, ev_w_in, ev_conv_w, ev_conv_b, ev_filt_w_in, ev_filt_w_hid, ev_filt_b, ev_filt_freq, ev_filt_w_out, ev_hy_skip, ev_qk_g, ev_sink, ev_w_out, ev_ffn_w1, ev_ffn_w3, ev_ffn_w2, od_w_qkv, od_qk_g, od_lambda, od_subln_g, od_w_out, od_router_w, od_router_b, od_moe_w1, od_moe_w3, od_moe_w2):
    B, L, D = x.shape
    Lc = ctx.shape[1]
    T = L + Lc
    bf16 = jnp.bfloat16
    cos, sin = _rope_tables(L, Lc)
    tables = _dft_tables()
    ctx_tables = _ctx_dft_tables(Lc)
    X = jnp.concatenate([x, ctx], axis=1)
    qk_scale = HEAD_DIM ** -0.5

    for i in range(DEPTH):
        j = i // 2
        m_l = _ada_modulation(c, ada_down[i], ada_up[i], ada_b[i])
        m_c = _ada_modulation(c_ctx, ada_down[i], ada_up[i], ada_b[i])
        mods = jnp.stack([m_l, jnp.broadcast_to(m_c[None], (B, N_MOD, D))], axis=1)

        h = norm_mod(X, norm_g[i, 0], mods[:, :, 0], mods[:, :, 1])
        hf = h.reshape(B * T, D)
        if i % 2 == 0:
            proj = matmul(hf, ev_w_in[j].astype(bf16)).reshape(B, T, -1)
            v_col0 = HY_WIDTH + WIN_Q + WIN_KV
            q = head_norm_rope(proj, HY_WIDTH, WIN_Q, ev_qk_g[j, 0], cos, sin, qk_scale)
            k = head_norm_rope(proj, HY_WIDTH + WIN_Q, WIN_KV, ev_qk_g[j, 1], cos, sin, 1.0)
            filt_args = (ev_filt_w_in[j], ev_filt_w_hid[j], ev_filt_b[j], ev_filt_freq[j], ev_filt_w_out[j])
            filt = _implicit_filters(L, *filt_args).reshape(FFT_R, FFT_R, HY_ORDER * HY_CH)
            filt = jnp.pad(filt, ((0, 0), (0, Z_PITCH - FFT_R), (0, 0))).reshape(FFT_R * Z_PITCH, HY_ORDER * HY_CH)
            spec = filter_spectrum(filt, tables[0], tables[2])
            hy = jnp.zeros((B, T, HY_CH), bf16)
            hy = hyena_conv(proj, ev_conv_w[j], ev_conv_b[j], ev_hy_skip[j], spec, tables, hy, n_lat=L)
            hy = hyena_ctx(proj, ev_conv_w[j], ev_conv_b[j], ev_hy_skip[j], _implicit_filters(Lc, *filt_args),
                           ctx_tables, hy, n_lat=L, n_ctx=Lc)
            att = window_attention(q, k, proj, ev_sink[j], n_ctx=Lc, v_col0=v_col0)
            X = matmul2_resgate(hy.reshape(B * T, HY_CH), att.reshape(B * T, WIN_Q), ev_w_out[j].astype(bf16),
                                X, mods[:, :, 2], n_lat=L)
        else:
            lam_init = 0.8 - 0.6 * math.exp(-0.3 * i)
            qkv = matmul(hf, od_w_qkv[j].astype(bf16)).reshape(B, T, -1)
            q = head_norm_rope(qkv, 0, DIFF_Q, od_qk_g[j, 0], cos, sin, DIFF_DIM ** -0.5 * math.log2(math.e), nh=8)
            k = head_norm_rope(qkv, DIFF_Q, DIFF_Q, od_qk_g[j, 1], cos, sin, 1.0, nh=8)
            lp = od_lambda[j].astype(jnp.float32)
            lam = jnp.exp(jnp.sum(lp[0] * lp[1])) - jnp.exp(jnp.sum(lp[2] * lp[3])) + lam_init
            attn = functools.partial(diff_attention, q, k, qkv, lam, od_subln_g[j], v_col0=2 * DIFF_Q,
                                     out_scale=1.0 - lam_init)
            o = jnp.zeros((B, T, DIFF_Q), bf16)
            o = attn(o, n_q=L, q_blk0=0, n_keys=T, key_blk=0, tq=512, tk=T // 6)
            o = attn(o, n_q=Lc, q_blk0=L // Lc, n_keys=Lc, key_blk=L // Lc, tq=Lc, tk=Lc)
            X = matmul_resgate(o.reshape(B * T, D), od_w_out[j].astype(bf16), X, mods[:, :, 2], n_lat=L)
        h2 = norm_mod(X, norm_g[i, 1], mods[:, :, 3], mods[:, :, 4]).reshape(B * T, D)
        if i % 2 == 0:
            gact = matmul_swiglu(h2, ev_ffn_w1[j].astype(bf16), ev_ffn_w3[j].astype(bf16))
            X = matmul_resgate(gact, ev_ffn_w2[j].astype(bf16), X, mods[:, :, 5], n_lat=L, tk=2048)
        else:
            y0, y1 = moe_layer(h2, od_router_w[j], od_router_b[j], od_moe_w1[j].astype(bf16),
                               od_moe_w3[j].astype(bf16), od_moe_w2[j].astype(bf16))
            X = moe_combine(X, y0.reshape(B, T, D), y1.reshape(B, T, D), mods[:, :, 5])
    return X[:, :L]
```

```python
import functools
import math

import jax
import jax.numpy as jnp
from jax import lax
from jax.experimental import pallas as pl
from jax.experimental.pallas import tpu as pltpu

D_MODEL = 4096
DEPTH = 4
GRID_W = 64
N_MOD = 6
EPS = 1e-6
NEG_INF = -1e30
HEAD_DIM = 128
ROPE_BASE = 10000.0
BLOCK = 128
WINDOW = 128
HY_CH = D_MODEL // 2
HY_ORDER = 2
HY_SHORT = 3
HY_EMB = 33
HY_BANDS = (HY_EMB - 1) // 2
HY_FILTER_HIDDEN_LAYERS = 2
HY_DIRS = 2
HY_FAST_DECAY = 0.3
HY_SLOW_DECAY = 1.5
HY_DECAY_TARGET = 1e-2
HY_WIDTH = (HY_ORDER + 1) * HY_CH
WIN_HEADS = (D_MODEL - HY_CH) // HEAD_DIM
WIN_KV_HEADS = WIN_HEADS // 4
WIN_GROUP = WIN_HEADS // WIN_KV_HEADS
WIN_Q = WIN_HEADS * HEAD_DIM
WIN_KV = WIN_KV_HEADS * HEAD_DIM
DIFF_DIM = 128
DIFF_HEADS = D_MODEL // (2 * DIFF_DIM)
DIFF_Q = DIFF_HEADS * 2 * DIFF_DIM
N_EXPERTS = 8
TOP_K = 2
assert DEPTH % 2 == 0, "the last layer is assumed to be an (odd) differential-attention / expert layer"

VMEM_LIMIT_BYTES = 56 * 1024 * 1024
ROW_TILE = 768
NORM_TILE = 256
MOE_TILE = 512

HI = lax.Precision.HIGHEST


def _cparams(sem):
    return pltpu.CompilerParams(dimension_semantics=sem, vmem_limit_bytes=VMEM_LIMIT_BYTES)


def _norm_mod_kernel(x_ref, g_ref, shift_ref, scale_ref, o_ref):
    x = x_ref[...]
    y = x * lax.rsqrt(jnp.mean(x * x, axis=-1, keepdims=True) + EPS)
    y = y * g_ref[...]
    o_ref[...] = (y * (1.0 + scale_ref[...]) + shift_ref[...]).astype(o_ref.dtype)


def norm_mod(x, g, shift, scale):
    B, T, D = x.shape
    nt = T // NORM_TILE
    mod_spec = pl.BlockSpec((None, None, 1, D), lambda b, t: (b, t // (nt - 1), 0, 0))
    return pl.pallas_call(
        _norm_mod_kernel,
        grid=(B, nt),
        in_specs=[
            pl.BlockSpec((None, NORM_TILE, D), lambda b, t: (b, t, 0)),
            pl.BlockSpec((1, D), lambda b, t: (0, 0)),
            mod_spec, mod_spec,
        ],
        out_specs=pl.BlockSpec((None, NORM_TILE, D), lambda b, t: (b, t, 0)),
        out_shape=jax.ShapeDtypeStruct((B, T, D), jnp.bfloat16),
        compiler_params=_cparams(("parallel", "parallel")),
        name="norm_mod",
    )(x, g.reshape(1, D), shift.reshape(B, 2, 1, D), scale.reshape(B, 2, 1, D))


def _mm_kernel(a_ref, b_ref, o_ref, acc_ref, *, nk):
    k = pl.program_id(2)
    part = jnp.dot(a_ref[...], b_ref[...], preferred_element_type=jnp.float32)
    if nk == 1:
        o_ref[...] = part.astype(o_ref.dtype)
        return

    @pl.when(k == 0)
    def _():
        acc_ref[...] = part

    @pl.when(k > 0)
    def _():
        acc_ref[...] += part

    @pl.when(k == nk - 1)
    def _():
        o_ref[...] = acc_ref[...].astype(o_ref.dtype)


def matmul(a, b, *, out_dtype=jnp.bfloat16, tm=ROW_TILE, tn=512, tk=None):
    M, K = a.shape
    _, N = b.shape
    tk = K if tk is None else tk
    nk = K // tk
    return pl.pallas_call(
        functools.partial(_mm_kernel, nk=nk),
        grid=(M // tm, N // tn, nk),
        in_specs=[pl.BlockSpec((tm, tk), lambda i, j, k: (i, k)),
                  pl.BlockSpec((tk, tn), lambda i, j, k: (k, j))],
        out_specs=pl.BlockSpec((tm, tn), lambda i, j, k: (i, j)),
        out_shape=jax.ShapeDtypeStruct((M, N), out_dtype),
        scratch_shapes=[pltpu.VMEM((tm, tn), jnp.float32)],
        compiler_params=_cparams(("parallel", "parallel", "arbitrary")),
        name="matmul",
    )(a, b)


def _mm_swiglu_kernel(a_ref, w1_ref, w3_ref, o_ref):
    a = a_ref[...]
    h1 = jnp.dot(a, w1_ref[...], preferred_element_type=jnp.float32)
    h3 = jnp.dot(a, w3_ref[...], preferred_element_type=jnp.float32)
    o_ref[...] = (h1 * jax.nn.sigmoid(h1) * h3).astype(o_ref.dtype)


def matmul_swiglu(a, w1, w3, *, tm=ROW_TILE, tn=512):
    M, K = a.shape
    _, N = w1.shape
    return pl.pallas_call(
        _mm_swiglu_kernel,
        grid=(M // tm, N // tn),
        in_specs=[pl.BlockSpec((tm, K), lambda i, j: (i, 0)),
                  pl.BlockSpec((K, tn), lambda i, j: (0, j)),
                  pl.BlockSpec((K, tn), lambda i, j: (0, j))],
        out_specs=pl.BlockSpec((tm, tn), lambda i, j: (i, j)),
        out_shape=jax.ShapeDtypeStruct((M, N), jnp.bfloat16),
        compiler_params=_cparams(("parallel", "parallel")),
        name="matmul_swiglu",
    )(a, w1, w3)


def _mm_resgate_kernel(a_ref, b_ref, res_ref, gate_ref, o_ref, acc_ref, *, nk, tm, tiles_per_batch, n_lat):
    i = pl.program_id(0)
    k = pl.program_id(2)
    part = jnp.dot(a_ref[...], b_ref[...], preferred_element_type=jnp.float32)

    def finish(acc):
        row = (i % tiles_per_batch) * tm + lax.broadcasted_iota(jnp.int32, (tm, 1), 0)
        gate = jnp.where(row < n_lat, gate_ref[0:1, :], gate_ref[1:2, :])
        o_ref[...] = res_ref[...] + gate * acc

    if nk == 1:
        finish(part)
        return

    @pl.when(k == 0)
    def _():
        acc_ref[...] = part

    @pl.when(k > 0)
    def _():
        acc_ref[...] += part

    @pl.when(k == nk - 1)
    def _():
        finish(acc_ref[...])


def matmul_resgate(a, b, res, gate, *, n_lat, tm=ROW_TILE, tn=512, tk=None):
    B, T, N = res.shape
    M, K = a.shape
    tk = K if tk is None else tk
    nk = K // tk
    tpb = T // tm
    kern = functools.partial(_mm_resgate_kernel, nk=nk, tm=tm, tiles_per_batch=tpb, n_lat=n_lat)
    out = pl.pallas_call(
        kern,
        grid=(M // tm, N // tn, nk),
        in_specs=[pl.BlockSpec((tm, tk), lambda i, j, k: (i, k)),
                  pl.BlockSpec((tk, tn), lambda i, j, k: (k, j)),
                  pl.BlockSpec((tm, tn), lambda i, j, k: (i, j)),
                  pl.BlockSpec((None, 2, tn), lambda i, j, k: (i // tpb, 0, j))],
        out_specs=pl.BlockSpec((tm, tn), lambda i, j, k: (i, j)),
        out_shape=jax.ShapeDtypeStruct((M, N), jnp.float32),
        scratch_shapes=[pltpu.VMEM((tm, tn), jnp.float32)],
        compiler_params=_cparams(("parallel", "parallel", "arbitrary")),
        name="matmul_resgate",
    )(a, b, res.reshape(M, N), gate)
    return out.reshape(B, T, N)


def _mm2_resgate_kernel(a1_ref, a2_ref, b_ref, res_ref, gate_ref, o_ref, *, tm, tiles_per_batch, n_lat):
    i = pl.program_id(0)
    k1 = a1_ref.shape[1]
    acc = (jnp.dot(a1_ref[...], b_ref[:k1, :], preferred_element_type=jnp.float32)
           + jnp.dot(a2_ref[...], b_ref[k1:, :], preferred_element_type=jnp.float32))
    row = (i % tiles_per_batch) * tm + lax.broadcasted_iota(jnp.int32, (tm, 1), 0)
    gate = jnp.where(row < n_lat, gate_ref[0:1, :], gate_ref[1:2, :])
    o_ref[...] = res_ref[...] + gate * acc


def matmul2_resgate(a1, a2, b, res, gate, *, n_lat, tm=ROW_TILE, tn=512):
    B, T, N = res.shape
    M, K1 = a1.shape
    K2 = a2.shape[1]
    tpb = T // tm
    out = pl.pallas_call(
        functools.partial(_mm2_resgate_kernel, tm=tm, tiles_per_batch=tpb, n_lat=n_lat),
        grid=(M // tm, N // tn),
        in_specs=[pl.BlockSpec((tm, K1), lambda i, j: (i, 0)),
                  pl.BlockSpec((tm, K2), lambda i, j: (i, 0)),
                  pl.BlockSpec((K1 + K2, tn), lambda i, j: (0, j)),
                  pl.BlockSpec((tm, tn), lambda i, j: (i, j)),
                  pl.BlockSpec((None, 2, tn), lambda i, j: (i // tpb, 0, j))],
        out_specs=pl.BlockSpec((tm, tn), lambda i, j: (i, j)),
        out_shape=jax.ShapeDtypeStruct((M, N), jnp.float32),
        compiler_params=_cparams(("parallel", "parallel")),
        name="matmul2_resgate",
    )(a1, a2, b, res.reshape(M, N), gate)
    return out.reshape(B, T, N)


def _moe_combine_kernel(x_ref, y0_ref, y1_ref, gate_ref, o_ref):
    y = y0_ref[...].astype(jnp.float32) + y1_ref[...].astype(jnp.float32)
    o_ref[...] = x_ref[...] + gate_ref[...] * y


def moe_combine(x, y0, y1, gate, *, n_rows):
    B, T, D = x.shape
    nt = T // NORM_TILE
    row = pl.BlockSpec((None, NORM_TILE, D), lambda b, t: (b, t, 0))
    return pl.pallas_call(
        _moe_combine_kernel,
        grid=(B, n_rows // NORM_TILE),
        in_specs=[row, row, row, pl.BlockSpec((None, None, 1, D), lambda b, t: (b, t // (nt - 1), 0, 0))],
        out_specs=row,
        out_shape=jax.ShapeDtypeStruct((B, n_rows, D), jnp.float32),
        compiler_params=_cparams(("parallel", "parallel")),
        name="moe_combine",
    )(x, y0, y1, gate.reshape(B, 2, 1, D))


def _window_attn_kernel(sink_ref, q_ref, kc_ref, kp_ref, ko_ref, kn_ref, vc_ref, vp_ref, vo_ref, vn_ref, o_ref,
                        *, n_lat, n_ctx):
    n = pl.program_id(1)
    h = pl.program_id(2)
    G = WIN_GROUP
    q = q_ref[...]
    qs = jnp.concatenate([q[:, g * HEAD_DIM:(g + 1) * HEAD_DIM] for g in range(G)], axis=0)
    k = jnp.concatenate([kc_ref[...], kp_ref[...], ko_ref[...], kn_ref[...]], axis=0)
    v = jnp.concatenate([vc_ref[...], vp_ref[...], vo_ref[...], vn_ref[...]], axis=0)
    s = lax.dot_general(qs, k, (((1,), (1,)), ((), ())), preferred_element_type=jnp.float32)
    nk = n_ctx + 3 * BLOCK
    col = lax.broadcasted_iota(jnp.int32, (G * BLOCK, nk), 1)
    qpos = n * BLOCK + lax.broadcasted_iota(jnp.int32, (G * BLOCK, nk), 0) % BLOCK
    kpos = (n - 1) * BLOCK + (col - n_ctx)
    valid = (col < n_ctx) | ((jnp.abs(qpos - kpos) <= WINDOW) & (kpos >= 0) & (kpos < n_lat) & (qpos < n_lat))
    s = jnp.where(valid, s, NEG_INF)
    sink = jnp.concatenate(
        [jnp.full((BLOCK, 1), sink_ref[h * G + g], jnp.float32) for g in range(G)], axis=0)
    m = jnp.maximum(jnp.max(s, axis=-1, keepdims=True), sink)
    e = jnp.exp(s - m)
    denom = jnp.sum(e, axis=-1, keepdims=True) + jnp.exp(sink - m)
    p = (e / denom).astype(v.dtype)
    o = jnp.dot(p, v, preferred_element_type=jnp.float32)
    for g in range(G):
        o_ref[:, g * HEAD_DIM:(g + 1) * HEAD_DIM] = o[g * BLOCK:(g + 1) * BLOCK, :].astype(o_ref.dtype)


def window_attention(q, k, v, sink, *, n_ctx, v_col0=0):
    B, T, _ = q.shape
    L = T - n_ctx
    nb = L // BLOCK
    kv_blk = lambda f: pl.BlockSpec((None, BLOCK, HEAD_DIM), f)
    ctx_blk = lambda off: pl.BlockSpec((None, n_ctx, HEAD_DIM), lambda b, n, h: (b, L // n_ctx, off + h))
    prev_blk = lambda off: kv_blk(lambda b, n, h: (b, jnp.maximum(n - 1, 0), off + h))
    own_blk = lambda off: kv_blk(lambda b, n, h: (b, n, off + h))
    next_blk = lambda off: kv_blk(lambda b, n, h: (b, jnp.minimum(n + 1, nb - 1), off + h))
    voff = v_col0 // HEAD_DIM
    return pl.pallas_call(
        functools.partial(_window_attn_kernel, n_lat=L, n_ctx=n_ctx),
        grid=(B, T // BLOCK, WIN_KV_HEADS),
        in_specs=[pl.BlockSpec(memory_space=pltpu.SMEM),
                  pl.BlockSpec((None, BLOCK, WIN_GROUP * HEAD_DIM), lambda b, n, h: (b, n, h)),
                  ctx_blk(0), prev_blk(0), own_blk(0), next_blk(0),
                  ctx_blk(voff), prev_blk(voff), own_blk(voff), next_blk(voff)],
        out_specs=pl.BlockSpec((None, BLOCK, WIN_GROUP * HEAD_DIM), lambda b, n, h: (b, n, h)),
        out_shape=jax.ShapeDtypeStruct((B, T, WIN_Q), jnp.bfloat16),
        compiler_params=_cparams(("parallel", "parallel", "parallel")),
        name="window_attention",
    )(sink.astype(jnp.float32), q, k, k, k, k, v, v, v, v)


def _diff_attn_kernel(lam_ref, q_ref, k_ref, v_ref, g_ref, buf_ref, o_ref, m_ref, l_ref, acc_ref,
                      *, n_keys, tk, out_scale):
    del buf_ref
    m_ref[...] = jnp.full(m_ref.shape, NEG_INF, jnp.float32)
    l_ref[...] = jnp.zeros(l_ref.shape, jnp.float32)
    acc_ref[...] = jnp.zeros(acc_ref.shape, jnp.float32)

    def step(j, carry):
        ks = pl.multiple_of(j * tk, tk)
        v = v_ref[pl.ds(ks, tk), :]
        scores = []
        for sub in range(2):
            q = q_ref[:, sub * DIFF_DIM:(sub + 1) * DIFF_DIM]
            k = k_ref[pl.ds(ks, tk), sub * DIFF_DIM:(sub + 1) * DIFF_DIM]
            scores.append(lax.dot_general(q, k, (((1,), (1,)), ((), ())), preferred_element_type=jnp.float32))
        for sub in range(2):
            s = scores[sub]
            m_old = m_ref[sub]
            m_new = jnp.maximum(m_old, jnp.max(s, axis=-1, keepdims=True))
            alpha = jnp.exp2(m_old - m_new)
            p = jnp.exp2(s - m_new)
            l_ref[sub] = alpha * l_ref[sub] + jnp.sum(p, axis=-1, keepdims=True)
            acc_ref[sub] = alpha * acc_ref[sub] + jnp.dot(p.astype(v.dtype), v, preferred_element_type=jnp.float32)
            m_ref[sub] = m_new
        return carry

    lax.fori_loop(0, n_keys // tk, step, 0)
    lam = lam_ref[0]
    o = acc_ref[0] / l_ref[0] - lam * (acc_ref[1] / l_ref[1])
    o = o * lax.rsqrt(jnp.mean(o * o, axis=-1, keepdims=True) + EPS)
    o_ref[...] = (o * g_ref[...] * out_scale).astype(o_ref.dtype)


def diff_attention(q, k, v, lam, subln_g, out_buf, *, n_q, q_blk0, n_keys, key_blk, tq, tk, out_scale, v_col0=0):
    B = q.shape[0]
    W = 2 * DIFF_DIM
    voff = v_col0 // W
    return pl.pallas_call(
        functools.partial(_diff_attn_kernel, n_keys=n_keys, tk=tk, out_scale=out_scale),
        grid=(B, DIFF_HEADS, n_q // tq),
        in_specs=[pl.BlockSpec(memory_space=pltpu.SMEM),
                  pl.BlockSpec((None, tq, W), lambda b, h, i: (b, q_blk0 + i, h)),
                  pl.BlockSpec((None, n_keys, W), lambda b, h, i: (b, key_blk, h)),
                  pl.BlockSpec((None, n_keys, W), lambda b, h, i: (b, key_blk, voff + h)),
                  pl.BlockSpec((1, W), lambda b, h, i: (0, 0)),
                  pl.BlockSpec(memory_space=pl.ANY)],
        out_specs=pl.BlockSpec((None, tq, W), lambda b, h, i: (b, q_blk0 + i, h)),
        scratch_shapes=[pltpu.VMEM((2, tq, 1), jnp.float32),
                        pltpu.VMEM((2, tq, 1), jnp.float32),
                        pltpu.VMEM((2, tq, W), jnp.float32)],
        out_shape=jax.ShapeDtypeStruct(out_buf.shape, out_buf.dtype),
        input_output_aliases={5: 0},
        compiler_params=_cparams(("parallel", "parallel", "arbitrary")),
        name="diff_attention",
    )(lam.reshape(1).astype(jnp.float32), q, k, v, subln_g.reshape(1, W).astype(jnp.float32), out_buf)


def _moe_up_kernel(te_ref, tv_ref, x_ref, w1_ref, w3_ref, o_ref):
    i = pl.program_id(1)

    @pl.when(tv_ref[i] > 0)
    def _():
        x = x_ref[...]
        h1 = jnp.dot(x, w1_ref[...], preferred_element_type=jnp.float32)
        h3 = jnp.dot(x, w3_ref[...], preferred_element_type=jnp.float32)
        o_ref[...] = (h1 * jax.nn.sigmoid(h1) * h3).astype(o_ref.dtype)

    @pl.when(tv_ref[i] == 0)
    def _():
        o_ref[...] = jnp.zeros(o_ref.shape, o_ref.dtype)


def _moe_down_kernel(te_ref, tv_ref, g_ref, w2_ref, rg_ref, o_ref):
    i = pl.program_id(1)

    @pl.when(tv_ref[i] > 0)
    def _():
        y = jnp.dot(g_ref[...], w2_ref[...], preferred_element_type=jnp.float32)
        o_ref[...] = (rg_ref[...] * y).astype(o_ref.dtype)

    @pl.when(tv_ref[i] == 0)
    def _():
        o_ref[...] = jnp.zeros(o_ref.shape, o_ref.dtype)


def moe_experts(xs, w1, w3, w2, tile_expert, tile_valid, row_gate, *, tn_up=768, tn_down=1024):
    P, D = xs.shape
    F = w1.shape[2]
    tm = MOE_TILE
    up = pl.pallas_call(
        _moe_up_kernel,
        grid_spec=pltpu.PrefetchScalarGridSpec(
            num_scalar_prefetch=2,
            grid=(F // tn_up, P // tm),
            in_specs=[pl.BlockSpec((tm, D), lambda j, i, te, tv: (i, 0)),
                      pl.BlockSpec((None, D, tn_up), lambda j, i, te, tv: (te[i], 0, j)),
                      pl.BlockSpec((None, D, tn_up), lambda j, i, te, tv: (te[i], 0, j))],
            out_specs=pl.BlockSpec((tm, tn_up), lambda j, i, te, tv: (i, j)),
        ),
        out_shape=jax.ShapeDtypeStruct((P, F), jnp.bfloat16),
        compiler_params=_cparams(("parallel", "arbitrary")),
        name="moe_up",
    )(tile_expert, tile_valid, xs, w1, w3)
    return pl.pallas_call(
        _moe_down_kernel,
        grid_spec=pltpu.PrefetchScalarGridSpec(
            num_scalar_prefetch=2,
            grid=(D // tn_down, P // tm),
            in_specs=[pl.BlockSpec((tm, F), lambda j, i, te, tv: (i, 0)),
                      pl.BlockSpec((None, F, tn_down), lambda j, i, te, tv: (te[i], 0, j)),
                      pl.BlockSpec((tm, 1), lambda j, i, te, tv: (i, 0))],
            out_specs=pl.BlockSpec((tm, tn_down), lambda j, i, te, tv: (i, j)),
        ),
        out_shape=jax.ShapeDtypeStruct((P, D), jnp.bfloat16),
        compiler_params=_cparams(("parallel", "arbitrary")),
        name="moe_down",
    )(tile_expert, tile_valid, up, w2, row_gate)


def moe_layer(h2, router_w, router_b, w1, w3, w2):
    N, D = h2.shape
    tm = MOE_TILE
    logits = jnp.dot(h2.astype(jnp.float32), router_w, precision=HI) + router_b
    top_v, top_i = lax.top_k(logits, TOP_K)
    gates = jax.nn.softmax(top_v, axis=-1)
    A = N * TOP_K
    e_flat = top_i.reshape(A).astype(jnp.int32)
    order = jnp.argsort(e_flat, stable=True)
    e_sorted = e_flat[order]
    counts = jnp.sum(e_flat[:, None] == jnp.arange(N_EXPERTS, dtype=jnp.int32)[None, :], axis=0).astype(jnp.int32)
    padded = ((counts + tm - 1) // tm) * tm
    start_unpadded = jnp.cumsum(counts) - counts
    start_padded = jnp.cumsum(padded) - padded
    dest_sorted = start_padded[e_sorted] + (jnp.arange(A, dtype=jnp.int32) - start_unpadded[e_sorted])
    P = A + N_EXPERTS * tm
    tile_start = jnp.arange(P // tm, dtype=jnp.int32) * tm
    ends = jnp.cumsum(padded)
    tile_expert = jnp.minimum(jnp.sum(tile_start[:, None] >= ends[None, :], axis=1), N_EXPERTS - 1).astype(jnp.int32)
    tile_valid = (tile_start < ends[-1]).astype(jnp.int32)
    rows = jnp.arange(P, dtype=jnp.int32)
    row_e = tile_expert[rows // tm]
    row_idx = rows - start_padded[row_e]
    row_ok = (row_idx < counts[row_e]) & (rows < ends[-1])
    src = order[jnp.clip(start_unpadded[row_e] + row_idx, 0, A - 1)]
    row_token = jnp.where(row_ok, src // TOP_K, 0).astype(jnp.int32)
    row_gate = jnp.where(row_ok, gates.reshape(A)[src], 0.0)
    pos = dest_sorted[jnp.argsort(order)].reshape(N, TOP_K)
    xs = jnp.take(h2, row_token, axis=0)
    y = moe_experts(xs, w1, w3, w2, tile_expert, tile_valid, row_gate.reshape(P, 1))
    return jnp.take(y, pos[:, 0], axis=0), jnp.take(y, pos[:, 1], axis=0)


FFT_R = 128
FFT_K1 = 72
Z_PITCH = FFT_R + 8
S1_PITCH = 2 * FFT_K1 + 8
S2_PITCH = 2 * FFT_R + 8
HY_LANES = 128
FFT_UNROLL = 16


def _dft_tables():
    R, K1 = FFT_R, FFT_K1
    N = R * R
    i32 = jnp.int32
    b = jnp.arange(R, dtype=i32)[:, None, None]
    k1 = jnp.arange(K1, dtype=i32)[None, :, None]
    a = jnp.arange(R, dtype=i32)[None, None, :]
    th = (2.0 * math.pi / N) * ((k1 * (R * a + b)) % N).astype(jnp.float32)
    f1 = jnp.concatenate([jnp.cos(th), -jnp.sin(th)], axis=1)
    w = jnp.where(jnp.arange(K1) > R // 2, 0.0, jnp.where((jnp.arange(K1) % (R // 2)) == 0, 1.0, 2.0)) / N
    the = jnp.swapaxes(th[:, :, :R // 2], 1, 2)
    e = jnp.concatenate([w * jnp.cos(the), -w * jnp.sin(the)], axis=2)
    k2 = jnp.arange(R, dtype=i32)
    ph = (2.0 * math.pi / R) * ((k2[:, None] * k2[None, :]) % R).astype(jnp.float32)
    c, s = jnp.cos(ph), jnp.sin(ph)
    g = jnp.block([[c, s], [-s, c]])
    ginv = jnp.block([[c, -s], [s, c]])
    bf = jnp.bfloat16
    return f1.astype(bf), e.astype(bf), g.astype(bf), ginv.astype(bf)


def _fft_stage1(src_ref, f1_ref, s1_ref, n_a):
    def body(b, carry):
        zb = src_ref[pl.ds(b, n_a, stride=Z_PITCH), :]
        s1_ref[pl.ds(pl.multiple_of(b * S1_PITCH, 8), 2 * FFT_K1), :] = jnp.dot(
            f1_ref[b], zb.astype(jnp.bfloat16), preferred_element_type=jnp.float32)
        return carry
    lax.fori_loop(0, FFT_R, body, 0, unroll=FFT_UNROLL)


def _fft_stage2(s1_ref, g_ref, k1):
    are = s1_ref[pl.ds(k1, FFT_R, stride=S1_PITCH), :]
    aim = s1_ref[pl.ds(FFT_K1 + k1, FFT_R, stride=S1_PITCH), :]
    r = jnp.concatenate([are, aim], axis=0).astype(jnp.bfloat16)
    return jnp.dot(g_ref[...], r, preferred_element_type=jnp.float32)


def _filter_fft_kernel(f_ref, f1_ref, g_ref, h_ref, s1_ref):
    _fft_stage1(f_ref, f1_ref, s1_ref, FFT_R)

    def body(k1, carry):
        h_ref[k1] = _fft_stage2(s1_ref, g_ref, k1).astype(h_ref.dtype)
        return carry
    lax.fori_loop(0, FFT_K1, body, 0, unroll=FFT_UNROLL)


def filter_spectrum(filt_padded, f1, g):
    rows, width = filt_padded.shape
    nt = width // HY_LANES
    once = pl.Buffered(1)
    return pl.pallas_call(
        _filter_fft_kernel,
        grid=(nt,),
        in_specs=[pl.BlockSpec((rows, HY_LANES), lambda c: (0, c)),
                  pl.BlockSpec(f1.shape, lambda c: (0, 0, 0), pipeline_mode=once),
                  pl.BlockSpec(g.shape, lambda c: (0, 0), pipeline_mode=once)],
        out_specs=pl.BlockSpec((None, FFT_K1, 2 * FFT_R, HY_LANES), lambda c: (c, 0, 0, 0)),
        out_shape=jax.ShapeDtypeStruct((nt, FFT_K1, 2 * FFT_R, HY_LANES), jnp.bfloat16),
        scratch_shapes=[pltpu.VMEM((FFT_R * S1_PITCH, HY_LANES), jnp.float32)],
        compiler_params=_cparams(("arbitrary",)),
        name="hyena_filter_fft",
    )(filt_padded, f1, g)


def _short_conv_chunks(x_ref, w_ref, b_ref, dst_ref, n_chunks, pitch=Z_PITCH):
    R = FFT_R
    w = w_ref[...]
    bias = b_ref[...]
    row = lax.broadcasted_iota(jnp.int32, (R, HY_LANES), 0)
    for a in range(n_chunks):
        xc = x_ref[a * R:(a + 1) * R, :].astype(jnp.float32)
        prev = pltpu.roll(xc, 1, axis=0)
        nxt = pltpu.roll(xc, R - 1, axis=0)
        if a > 0:
            last = x_ref[a * R - 16:a * R, :].astype(jnp.float32)[15:16, :]
        else:
            last = jnp.zeros((1, HY_LANES), jnp.float32)
        if a < n_chunks - 1:
            first = x_ref[(a + 1) * R:(a + 1) * R + 16, :].astype(jnp.float32)[0:1, :]
        else:
            first = jnp.zeros((1, HY_LANES), jnp.float32)
        prev = jnp.where(row == 0, last, prev)
        nxt = jnp.where(row == R - 1, first, nxt)
        dst_ref[a * pitch:a * pitch + R, :] = bias + prev * w[0:1, :] + xc * w[1:2, :] + nxt * w[2:3, :]


def _hyena_conv_kernel(v_ref, x_ref, wv_ref, bv_ref, wx_ref, bx_ref, skip_ref, h_ref, f1_ref, e_ref, g_ref, gi_ref,
                       buf_ref, o_ref, z_ref, gate_ref, s1_ref, s2_ref, *, n_chunks):
    del buf_ref
    n = pl.program_id(2)
    R, K1 = FFT_R, FFT_K1

    @pl.when(n == 0)
    def _():
        _short_conv_chunks(v_ref, wv_ref, bv_ref, z_ref, n_chunks)

    _short_conv_chunks(x_ref, wx_ref, bx_ref, gate_ref, n_chunks)

    _fft_stage1(z_ref, f1_ref, s1_ref, n_chunks)

    def mid(k1, carry):
        x = _fft_stage2(s1_ref, g_ref, k1)
        h = h_ref[k1].astype(jnp.float32)
        xr, xi, hr, hi = x[:R], x[R:], h[:R], h[R:]
        p = jnp.concatenate([xr * hr - xi * hi, xr * hi + xi * hr], axis=0).astype(jnp.bfloat16)
        s2_ref[pl.ds(pl.multiple_of(k1 * S2_PITCH, 8), 2 * R), :] = jnp.dot(
            gi_ref[...], p, preferred_element_type=jnp.float32)
        return carry
    lax.fori_loop(0, K1, mid, 0, unroll=FFT_UNROLL)

    skip = skip_ref[...]

    def last(b, carry):
        qre = s2_ref[pl.ds(b, K1, stride=S2_PITCH), :]
        qim = s2_ref[pl.ds(R + b, K1, stride=S2_PITCH), :]
        r = jnp.concatenate([qre, qim], axis=0).astype(jnp.bfloat16)
        conv = jnp.dot(e_ref[b], r, preferred_element_type=jnp.float32)
        zold = z_ref[pl.ds(b, n_chunks, stride=Z_PITCH), :]
        gate = gate_ref[pl.ds(b, n_chunks, stride=Z_PITCH), :]
        z_ref[pl.ds(b, n_chunks, stride=Z_PITCH), :] = gate * (conv + zold * skip)
        return carry
    lax.fori_loop(0, R, last, 0, unroll=FFT_UNROLL)

    @pl.when(n == HY_ORDER - 1)
    def _():
        for a in range(n_chunks):
            o_ref[a * R:(a + 1) * R, :] = z_ref[a * Z_PITCH:a * Z_PITCH + R, :].astype(o_ref.dtype)


def hyena_conv(proj, conv_w, conv_b, skip, spec, tables, out_buf, *, n_lat):
    f1, e, g, ginv = tables
    B = proj.shape[0]
    n_chunks = n_lat // FFT_R
    nt = HY_CH // HY_LANES
    once = pl.Buffered(1)
    grp = lambda n: (1 + n) * nt
    f1h = f1[:, :, :n_chunks]
    return pl.pallas_call(
        functools.partial(_hyena_conv_kernel, n_chunks=n_chunks),
        grid=(B, nt, HY_ORDER),
        in_specs=[pl.BlockSpec((None, n_lat, HY_LANES), lambda b, c, n: (b, 0, c)),
                  pl.BlockSpec((None, n_lat, HY_LANES), lambda b, c, n: (b, 0, grp(n) + c)),
                  pl.BlockSpec((HY_SHORT, HY_LANES), lambda b, c, n: (0, c)),
                  pl.BlockSpec((1, HY_LANES), lambda b, c, n: (0, c)),
                  pl.BlockSpec((HY_SHORT, HY_LANES), lambda b, c, n: (0, grp(n) + c)),
                  pl.BlockSpec((1, HY_LANES), lambda b, c, n: (0, grp(n) + c)),
                  pl.BlockSpec((None, 1, HY_LANES), lambda b, c, n: (n, 0, c)),
                  pl.BlockSpec((None, FFT_K1, 2 * FFT_R, HY_LANES), lambda b, c, n: (n * nt + c, 0, 0, 0),
                               pipeline_mode=once),
                  pl.BlockSpec(f1h.shape, lambda b, c, n: (0, 0, 0), pipeline_mode=once),
                  pl.BlockSpec(e.shape, lambda b, c, n: (0, 0, 0), pipeline_mode=once),
                  pl.BlockSpec(g.shape, lambda b, c, n: (0, 0), pipeline_mode=once),
                  pl.BlockSpec(ginv.shape, lambda b, c, n: (0, 0), pipeline_mode=once),
                  pl.BlockSpec(memory_space=pl.ANY)],
        out_specs=pl.BlockSpec((None, n_lat, HY_LANES), lambda b, c, n: (b, 0, c)),
        out_shape=jax.ShapeDtypeStruct(out_buf.shape, out_buf.dtype),
        input_output_aliases={12: 0},
        scratch_shapes=[pltpu.VMEM((n_chunks * Z_PITCH, HY_LANES), jnp.float32),
                        pltpu.VMEM((n_chunks * Z_PITCH, HY_LANES), jnp.float32),
                        pltpu.VMEM((FFT_R * S1_PITCH, HY_LANES), jnp.float32),
                        pltpu.VMEM((FFT_K1 * S2_PITCH, HY_LANES), jnp.float32)],
        compiler_params=_cparams(("parallel", "parallel", "arbitrary")),
        name="hyena_conv",
    )(proj, proj, conv_w, conv_b.reshape(1, -1), conv_w, conv_b.reshape(1, -1),
      skip.reshape(HY_ORDER, 1, HY_CH), spec, f1h, e, g, ginv, out_buf)


def _ctx_dft_tables(n_ctx):
    N = 2 * n_ctx
    nk = -(-(n_ctx + 1) // 16) * 16
    k = jnp.arange(nk, dtype=jnp.int32)[:, None]
    t = jnp.arange(N, dtype=jnp.int32)[None, :]
    th = (2.0 * math.pi / N) * ((k * t) % N).astype(jnp.float32)
    fwd = jnp.concatenate([jnp.cos(th), -jnp.sin(th)], axis=0)
    w = jnp.where(k > n_ctx, 0.0, jnp.where((k % n_ctx) == 0, 1.0, 2.0)) / N
    inv = jnp.concatenate([(w * jnp.cos(th[:, :n_ctx])).T, (-w * jnp.sin(th[:, :n_ctx])).T], axis=1)
    return fwd.astype(jnp.bfloat16), inv.astype(jnp.bfloat16)


def _hyena_ctx_kernel(v_ref, x1_ref, x2_ref, wv_ref, bv_ref, w1_ref, b1_ref, w2_ref, b2_ref, skip_ref,
                      f0_ref, f1_ref, fwd_ref, inv_ref, buf_ref, o_ref, z_ref, g1_ref, g2_ref, *, n_ctx):
    del buf_ref
    nc = n_ctx // FFT_R
    _short_conv_chunks(v_ref, wv_ref, bv_ref, z_ref, nc, pitch=FFT_R)
    _short_conv_chunks(x1_ref, w1_ref, b1_ref, g1_ref, nc, pitch=FFT_R)
    _short_conv_chunks(x2_ref, w2_ref, b2_ref, g2_ref, nc, pitch=FFT_R)
    nk = fwd_ref.shape[0] // 2
    z = z_ref[...]
    for n, (filt_ref, gate_ref) in enumerate(((f0_ref, g1_ref), (f1_ref, g2_ref))):
        h = jnp.dot(fwd_ref[...], filt_ref[...].astype(jnp.bfloat16), preferred_element_type=jnp.float32)
        x = jnp.dot(fwd_ref[:, :n_ctx], z.astype(jnp.bfloat16), preferred_element_type=jnp.float32)
        xr, xi, hr, hi = x[:nk], x[nk:], h[:nk], h[nk:]
        p = jnp.concatenate([xr * hr - xi * hi, xr * hi + xi * hr], axis=0).astype(jnp.bfloat16)
        conv = jnp.dot(inv_ref[...], p, preferred_element_type=jnp.float32)
        z = gate_ref[...] * (conv + z * skip_ref[n:n + 1, :])
    o_ref[...] = z.astype(o_ref.dtype)


def hyena_ctx(proj, conv_w, conv_b, skip, filt, tables, out_buf, *, n_lat, n_ctx):
    fwd, inv = tables
    B = proj.shape[0]
    nt = HY_CH // HY_LANES
    rb = n_lat // n_ctx
    cb2 = conv_b.reshape(1, -1)
    row = lambda g: pl.BlockSpec((None, n_ctx, HY_LANES), lambda b, c: (b, rb, g * nt + c))
    wsp = lambda g: pl.BlockSpec((HY_SHORT, HY_LANES), lambda b, c: (0, g * nt + c))
    bsp = lambda g: pl.BlockSpec((1, HY_LANES), lambda b, c: (0, g * nt + c))
    fsp = lambda n: pl.BlockSpec((2 * n_ctx, HY_LANES), lambda b, c: (0, n * nt + c))
    scr = pltpu.VMEM((n_ctx, HY_LANES), jnp.float32)
    return pl.pallas_call(
        functools.partial(_hyena_ctx_kernel, n_ctx=n_ctx),
        grid=(B, nt),
        in_specs=[row(0), row(1), row(2), wsp(0), bsp(0), wsp(1), bsp(1), wsp(2), bsp(2),
                  pl.BlockSpec((HY_ORDER, HY_LANES), lambda b, c: (0, c)),
                  fsp(0), fsp(1),
                  pl.BlockSpec(fwd.shape, lambda b, c: (0, 0)),
                  pl.BlockSpec(inv.shape, lambda b, c: (0, 0)),
                  pl.BlockSpec(memory_space=pl.ANY)],
        out_specs=pl.BlockSpec((None, n_ctx, HY_LANES), lambda b, c: (b, rb, c)),
        out_shape=jax.ShapeDtypeStruct(out_buf.shape, out_buf.dtype),
        input_output_aliases={14: 0},
        scratch_shapes=[scr, scr, scr],
        compiler_params=_cparams(("parallel", "parallel")),
        name="hyena_ctx",
    )(proj, proj, proj, conv_w, cb2, conv_w, cb2, conv_w, cb2, skip, filt, filt, fwd, inv, out_buf)


def _norm_rope_kernel(x_ref, g_ref, cos_ref, sin_ref, o_ref, *, scale, nh):
    cos = cos_ref[...]
    sin = sin_ref[...]
    g = g_ref[...]
    lane = lax.broadcasted_iota(jnp.int32, cos.shape, 1)
    lower = (lane % (HEAD_DIM // 2)) < (HEAD_DIM // 4)
    for h in range(nh):
        x = x_ref[:, h * HEAD_DIM:(h + 1) * HEAD_DIM].astype(jnp.float32)
        y = x * lax.rsqrt(jnp.mean(x * x, axis=-1, keepdims=True) + EPS) * g
        rot = jnp.where(lower, -pltpu.roll(y, HEAD_DIM - HEAD_DIM // 4, axis=1), pltpu.roll(y, HEAD_DIM // 4, axis=1))
        o_ref[:, h * HEAD_DIM:(h + 1) * HEAD_DIM] = ((y * cos + rot * sin) * scale).astype(o_ref.dtype)


def head_norm_rope(t, col0, width, g, cos, sin, scale, *, tr=ROW_TILE, nh=4):
    B, T, _ = t.shape
    wb = nh * HEAD_DIM
    c0 = col0 // wb
    return pl.pallas_call(
        functools.partial(_norm_rope_kernel, scale=scale, nh=nh),
        grid=(B, T // tr, width // wb),
        in_specs=[pl.BlockSpec((None, tr, wb), lambda b, r, j: (b, r, c0 + j)),
                  pl.BlockSpec((1, HEAD_DIM), lambda b, r, j: (0, 0)),
                  pl.BlockSpec((tr, HEAD_DIM), lambda b, r, j: (r, 0)),
                  pl.BlockSpec((tr, HEAD_DIM), lambda b, r, j: (r, 0))],
        out_specs=pl.BlockSpec((None, tr, wb), lambda b, r, j: (b, r, j)),
        out_shape=jax.ShapeDtypeStruct((B, T, width), jnp.bfloat16),
        compiler_params=_cparams(("parallel", "parallel", "parallel")),
        name="head_norm_rope",
    )(t, g.reshape(1, HEAD_DIM).astype(jnp.float32), cos, sin)


def _rope_tables(L, n_ctx):
    rows = jnp.repeat(jnp.arange(L // GRID_W), GRID_W)
    cols = jnp.tile(jnp.arange(GRID_W), L // GRID_W)
    quarter = HEAD_DIM // 4
    inv = ROPE_BASE ** (-jnp.arange(quarter, dtype=jnp.float32) / quarter)
    ar = rows.astype(jnp.float32)[:, None] * inv
    ac = cols.astype(jnp.float32)[:, None] * inv
    ang = jnp.concatenate([ar, ar, ac, ac], axis=-1)
    cos = jnp.concatenate([jnp.cos(ang), jnp.ones((n_ctx, HEAD_DIM), jnp.float32)], axis=0)
    sin = jnp.concatenate([jnp.sin(ang), jnp.zeros((n_ctx, HEAD_DIM), jnp.float32)], axis=0)
    return cos, sin


def _implicit_filters(L, w_in, w_hid, b, freq, w_out, pitch=FFT_R):
    f32 = jnp.float32
    t = jnp.linspace(0.0, 1.0, L, dtype=f32)[:, None]
    w = (2.0 * math.pi / L) * jnp.arange(L, dtype=f32)[:, None]
    f = jnp.linspace(1e-4, HY_BANDS - 1, HY_BANDS, dtype=f32)[None, :]
    z = jnp.concatenate([t, jnp.cos(f * w), -jnp.sin(f * w)], axis=-1)
    h = jnp.sin(freq[0] * (jnp.dot(z, w_in, precision=HI) + b[0]))
    for n in range(HY_FILTER_HIDDEN_LAYERS):
        h = jnp.sin(freq[n + 1] * (jnp.dot(h, w_hid[n], precision=HI) + b[n + 1]))
    width = HY_ORDER * HY_CH
    max_decay = math.log(HY_DECAY_TARGET) / HY_FAST_DECAY
    min_decay = math.log(HY_DECAY_TARGET) / HY_SLOW_DECAY
    deltas = jnp.abs(jnp.linspace(min_decay, max_decay, width, dtype=f32))[None, :]
    hb = jnp.concatenate([jnp.zeros((1, h.shape[1]), f32), h[:0:-1]], axis=0)
    tb = jnp.concatenate([jnp.zeros((1, 1), f32), t[:0:-1]], axis=0)
    chunk = lambda a: jnp.pad(a.reshape(2, L // FFT_R, FFT_R, a.shape[-1]),
                              ((0, 0), (0, 0), (0, pitch - FFT_R), (0, 0))).reshape(2, -1, a.shape[-1])
    hh = chunk(jnp.stack([h, hb]))
    tt = chunk(jnp.stack([t, tb]))
    w2 = jnp.stack([w_out[:, :width], w_out[:, width:]])
    filt = jnp.einsum('hrk,hkw->hrw', hh, w2, precision=HI) * jnp.exp(-tt * deltas)
    return filt.reshape(-1, width)


def _ada_modulation(cv, down, up, b):
    m = jnp.dot(jnp.dot(jax.nn.silu(cv), down, precision=HI), up, precision=HI) + b
    return m.reshape(m.shape[:-1] + (N_MOD, m.shape[-1] // N_MOD))


def kernel(x, c, ctx, c_ctx, norm_g, ada_down, ada_up, ada_b, ev_w_in, ev_conv_w, ev_conv_b, ev_filt_w_in, ev_filt_w_hid, ev_filt_b, ev_filt_freq, ev_filt_w_out, ev_hy_skip, ev_qk_g, ev_sink, ev_w_out, ev_ffn_w1, ev_ffn_w3, ev_ffn_w2, od_w_qkv, od_qk_g, od_lambda, od_subln_g, od_w_out, od_router_w, od_router_b, od_moe_w1, od_moe_w3, od_moe_w2):
    B, L, D = x.shape
    Lc = ctx.shape[1]
    T = L + Lc
    bf16 = jnp.bfloat16
    cos, sin = _rope_tables(L, Lc)
    tables = _dft_tables()
    ctx_tables = _ctx_dft_tables(Lc)
    X = jnp.concatenate([x, ctx], axis=1)
    qk_scale = HEAD_DIM ** -0.5

    for i in range(DEPTH):
        j = i // 2
        m_l = _ada_modulation(c, ada_down[i], ada_up[i], ada_b[i])
        m_c = _ada_modulation(c_ctx, ada_down[i], ada_up[i], ada_b[i])
        mods = jnp.stack([m_l, jnp.broadcast_to(m_c[None], (B, N_MOD, D))], axis=1)

        h = norm_mod(X, norm_g[i, 0], mods[:, :, 0], mods[:, :, 1])
        hf = h.reshape(B * T, D)
        if i % 2 == 0:
            proj = matmul(hf, ev_w_in[j].astype(bf16)).reshape(B, T, -1)
            v_col0 = HY_WIDTH + WIN_Q + WIN_KV
            q = head_norm_rope(proj, HY_WIDTH, WIN_Q, ev_qk_g[j, 0], cos, sin, qk_scale)
            k = head_norm_rope(proj, HY_WIDTH + WIN_Q, WIN_KV, ev_qk_g[j, 1], cos, sin, 1.0)
            filt_args = (ev_filt_w_in[j], ev_filt_w_hid[j], ev_filt_b[j], ev_filt_freq[j], ev_filt_w_out[j])
            spec = filter_spectrum(_implicit_filters(L, *filt_args, pitch=Z_PITCH), tables[0], tables[2])
            hy = jnp.zeros((B, T, HY_CH), bf16)
            hy = hyena_conv(proj, ev_conv_w[j], ev_conv_b[j], ev_hy_skip[j], spec, tables, hy, n_lat=L)
            hy = hyena_ctx(proj, ev_conv_w[j], ev_conv_b[j], ev_hy_skip[j], _implicit_filters(Lc, *filt_args),
                           ctx_tables, hy, n_lat=L, n_ctx=Lc)
            att = window_attention(q, k, proj, ev_sink[j], n_ctx=Lc, v_col0=v_col0)
            X = matmul2_resgate(hy.reshape(B * T, HY_CH), att.reshape(B * T, WIN_Q), ev_w_out[j].astype(bf16),
                                X, mods[:, :, 2], n_lat=L)
        else:
            lam_init = 0.8 - 0.6 * math.exp(-0.3 * i)
            qkv = matmul(hf, od_w_qkv[j].astype(bf16)).reshape(B, T, -1)
            q = head_norm_rope(qkv, 0, DIFF_Q, od_qk_g[j, 0], cos, sin, DIFF_DIM ** -0.5 * math.log2(math.e), nh=8)
            k = head_norm_rope(qkv, DIFF_Q, DIFF_Q, od_qk_g[j, 1], cos, sin, 1.0, nh=8)
            lp = od_lambda[j].astype(jnp.float32)
            lam = jnp.exp(jnp.sum(lp[0] * lp[1])) - jnp.exp(jnp.sum(lp[2] * lp[3])) + lam_init
            attn = functools.partial(diff_attention, q, k, qkv, lam, od_subln_g[j], v_col0=2 * DIFF_Q,
                                     out_scale=1.0 - lam_init)
            o = attn(h, n_q=L, q_blk0=0, n_keys=T, key_blk=0, tq=512, tk=T // 6)
            o = attn(o, n_q=Lc, q_blk0=L // Lc, n_keys=Lc, key_blk=L // Lc, tq=Lc, tk=Lc)
            X = matmul_resgate(o.reshape(B * T, D), od_w_out[j].astype(bf16), X, mods[:, :, 2], n_lat=L)
        h2 = norm_mod(X, norm_g[i, 1], mods[:, :, 3], mods[:, :, 4]).reshape(B * T, D)
        if i % 2 == 0:
            gact = matmul_swiglu(h2, ev_ffn_w1[j].astype(bf16), ev_ffn_w3[j].astype(bf16))
            X = matmul_resgate(gact, ev_ffn_w2[j].astype(bf16), X, mods[:, :, 5], n_lat=L, tk=2048)
        else:
            y0, y1 = moe_layer(h2, od_router_w[j], od_router_b[j], od_moe_w1[j].astype(bf16),
                               od_moe_w3[j].astype(bf16), od_moe_w2[j].astype(bf16))
            X = moe_combine(X, y0.reshape(B, T, D), y1.reshape(B, T, D), mods[:, :, 5],
                            n_rows=L if i == DEPTH - 1 else T)
    return X
```

```python
import functools
import math

import jax
import jax.numpy as jnp
from jax import lax
from jax.experimental import pallas as pl
from jax.experimental.pallas import tpu as pltpu

D_MODEL = 4096
DEPTH = 4
GRID_W = 64
N_MOD = 6
EPS = 1e-6
NEG_INF = -1e30
HEAD_DIM = 128
ROPE_BASE = 10000.0
BLOCK = 128
WINDOW = 128
HY_CH = D_MODEL // 2
HY_ORDER = 2
HY_SHORT = 3
HY_EMB = 33
HY_BANDS = (HY_EMB - 1) // 2
HY_FILTER_HIDDEN_LAYERS = 2
HY_DIRS = 2
HY_FAST_DECAY = 0.3
HY_SLOW_DECAY = 1.5
HY_DECAY_TARGET = 1e-2
HY_WIDTH = (HY_ORDER + 1) * HY_CH
WIN_HEADS = (D_MODEL - HY_CH) // HEAD_DIM
WIN_KV_HEADS = WIN_HEADS // 4
WIN_GROUP = WIN_HEADS // WIN_KV_HEADS
WIN_Q = WIN_HEADS * HEAD_DIM
WIN_KV = WIN_KV_HEADS * HEAD_DIM
DIFF_DIM = 128
DIFF_HEADS = D_MODEL // (2 * DIFF_DIM)
DIFF_Q = DIFF_HEADS * 2 * DIFF_DIM
N_EXPERTS = 8
TOP_K = 2
assert DEPTH % 2 == 0, "the last layer is assumed to be an (odd) differential-attention / expert layer"

VMEM_LIMIT_BYTES = 56 * 1024 * 1024
ROW_TILE = 768
NORM_TILE = 256
MOE_TILE = 512

HI = lax.Precision.HIGHEST


def _cparams(sem):
    return pltpu.CompilerParams(dimension_semantics=sem, vmem_limit_bytes=VMEM_LIMIT_BYTES)


def _cast_kernel(w_ref, o_ref):
    o_ref[...] = w_ref[...].astype(o_ref.dtype)


def cast_layer(w, j, *, tr=512):
    lead = w.shape[1:-2]
    R, W = w.shape[-2:]
    rows = math.prod(lead) * R
    w3 = w.reshape(w.shape[0], rows, W)
    tw = next(t for t in (2048, 1536, 1024, 512) if W % t == 0)
    out = pl.pallas_call(
        _cast_kernel,
        grid=(rows // tr, W // tw),
        in_specs=[pl.BlockSpec((None, tr, tw), lambda r, c: (j, r, c))],
        out_specs=pl.BlockSpec((tr, tw), lambda r, c: (r, c)),
        out_shape=jax.ShapeDtypeStruct((rows, W), jnp.bfloat16),
        compiler_params=_cparams(("parallel", "parallel")),
        name="cast_layer",
    )(w3)
    return out.reshape(lead + (R, W))


def _norm_mod_kernel(x_ref, g_ref, shift_ref, scale_ref, o_ref):
    x = x_ref[...]
    y = x * lax.rsqrt(jnp.mean(x * x, axis=-1, keepdims=True) + EPS)
    y = y * g_ref[...]
    o_ref[...] = (y * (1.0 + scale_ref[...]) + shift_ref[...]).astype(o_ref.dtype)


def norm_mod(x, g, shift, scale):
    B, T, D = x.shape
    nt = T // NORM_TILE
    mod_spec = pl.BlockSpec((None, None, 1, D), lambda b, t: (b, t // (nt - 1), 0, 0))
    return pl.pallas_call(
        _norm_mod_kernel,
        grid=(B, nt),
        in_specs=[
            pl.BlockSpec((None, NORM_TILE, D), lambda b, t: (b, t, 0)),
            pl.BlockSpec((1, D), lambda b, t: (0, 0)),
            mod_spec, mod_spec,
        ],
        out_specs=pl.BlockSpec((None, NORM_TILE, D), lambda b, t: (b, t, 0)),
        out_shape=jax.ShapeDtypeStruct((B, T, D), jnp.bfloat16),
        compiler_params=_cparams(("parallel", "parallel")),
        name="norm_mod",
    )(x, g.reshape(1, D), shift.reshape(B, 2, 1, D), scale.reshape(B, 2, 1, D))


def _mm_kernel(a_ref, b_ref, o_ref, acc_ref, *, nk):
    k = pl.program_id(2)
    part = jnp.dot(a_ref[...], b_ref[...], preferred_element_type=jnp.float32)
    if nk == 1:
        o_ref[...] = part.astype(o_ref.dtype)
        return

    @pl.when(k == 0)
    def _():
        acc_ref[...] = part

    @pl.when(k > 0)
    def _():
        acc_ref[...] += part

    @pl.when(k == nk - 1)
    def _():
        o_ref[...] = acc_ref[...].astype(o_ref.dtype)


def matmul(a, b, *, out_dtype=jnp.bfloat16, tm=ROW_TILE, tn=512, tk=None):
    M, K = a.shape
    _, N = b.shape
    tk = K if tk is None else tk
    nk = K // tk
    return pl.pallas_call(
        functools.partial(_mm_kernel, nk=nk),
        grid=(M // tm, N // tn, nk),
        in_specs=[pl.BlockSpec((tm, tk), lambda i, j, k: (i, k)),
                  pl.BlockSpec((tk, tn), lambda i, j, k: (k, j))],
        out_specs=pl.BlockSpec((tm, tn), lambda i, j, k: (i, j)),
        out_shape=jax.ShapeDtypeStruct((M, N), out_dtype),
        scratch_shapes=[pltpu.VMEM((tm, tn), jnp.float32)],
        compiler_params=_cparams(("parallel", "parallel", "arbitrary")),
        name="matmul",
    )(a, b)


def _mm_swiglu_kernel(a_ref, w1_ref, w3_ref, o_ref):
    a = a_ref[...]
    h1 = jnp.dot(a, w1_ref[...], preferred_element_type=jnp.float32)
    h3 = jnp.dot(a, w3_ref[...], preferred_element_type=jnp.float32)
    o_ref[...] = (h1 * jax.nn.sigmoid(h1) * h3).astype(o_ref.dtype)


def matmul_swiglu(a, w1, w3, *, tm=ROW_TILE, tn=512):
    M, K = a.shape
    _, N = w1.shape
    return pl.pallas_call(
        _mm_swiglu_kernel,
        grid=(M // tm, N // tn),
        in_specs=[pl.BlockSpec((tm, K), lambda i, j: (i, 0)),
                  pl.BlockSpec((K, tn), lambda i, j: (0, j)),
                  pl.BlockSpec((K, tn), lambda i, j: (0, j))],
        out_specs=pl.BlockSpec((tm, tn), lambda i, j: (i, j)),
        out_shape=jax.ShapeDtypeStruct((M, N), jnp.bfloat16),
        compiler_params=_cparams(("parallel", "parallel")),
        name="matmul_swiglu",
    )(a, w1, w3)


def _mm_resgate_kernel(a_ref, b_ref, res_ref, gate_ref, o_ref, acc_ref, *, nk, tm, tiles_per_batch, n_lat):
    i = pl.program_id(0)
    k = pl.program_id(2)
    part = jnp.dot(a_ref[...], b_ref[...], preferred_element_type=jnp.float32)

    def finish(acc):
        row = (i % tiles_per_batch) * tm + lax.broadcasted_iota(jnp.int32, (tm, 1), 0)
        gate = jnp.where(row < n_lat, gate_ref[0:1, :], gate_ref[1:2, :])
        o_ref[...] = res_ref[...] + gate * acc

    if nk == 1:
        finish(part)
        return

    @pl.when(k == 0)
    def _():
        acc_ref[...] = part

    @pl.when(k > 0)
    def _():
        acc_ref[...] += part

    @pl.when(k == nk - 1)
    def _():
        finish(acc_ref[...])


def matmul_resgate(a, b, res, gate, *, n_lat, tm=ROW_TILE, tn=512, tk=None):
    B, T, N = res.shape
    M, K = a.shape
    tk = K if tk is None else tk
    nk = K // tk
    tpb = T // tm
    kern = functools.partial(_mm_resgate_kernel, nk=nk, tm=tm, tiles_per_batch=tpb, n_lat=n_lat)
    out = pl.pallas_call(
        kern,
        grid=(M // tm, N // tn, nk),
        in_specs=[pl.BlockSpec((tm, tk), lambda i, j, k: (i, k)),
                  pl.BlockSpec((tk, tn), lambda i, j, k: (k, j)),
                  pl.BlockSpec((tm, tn), lambda i, j, k: (i, j)),
                  pl.BlockSpec((None, 2, tn), lambda i, j, k: (i // tpb, 0, j))],
        out_specs=pl.BlockSpec((tm, tn), lambda i, j, k: (i, j)),
        out_shape=jax.ShapeDtypeStruct((M, N), jnp.float32),
        scratch_shapes=[pltpu.VMEM((tm, tn), jnp.float32)],
        compiler_params=_cparams(("parallel", "parallel", "arbitrary")),
        name="matmul_resgate",
    )(a, b, res.reshape(M, N), gate)
    return out.reshape(B, T, N)


def _mm2_resgate_kernel(a1_ref, a2_ref, b_ref, res_ref, gate_ref, o_ref, *, tm, tiles_per_batch, n_lat):
    i = pl.program_id(0)
    k1 = a1_ref.shape[1]
    acc = (jnp.dot(a1_ref[...], b_ref[:k1, :], preferred_element_type=jnp.float32)
           + jnp.dot(a2_ref[...], b_ref[k1:, :], preferred_element_type=jnp.float32))
    row = (i % tiles_per_batch) * tm + lax.broadcasted_iota(jnp.int32, (tm, 1), 0)
    gate = jnp.where(row < n_lat, gate_ref[0:1, :], gate_ref[1:2, :])
    o_ref[...] = res_ref[...] + gate * acc


def matmul2_resgate(a1, a2, b, res, gate, *, n_lat, tm=ROW_TILE, tn=512):
    B, T, N = res.shape
    M, K1 = a1.shape
    K2 = a2.shape[1]
    tpb = T // tm
    out = pl.pallas_call(
        functools.partial(_mm2_resgate_kernel, tm=tm, tiles_per_batch=tpb, n_lat=n_lat),
        grid=(M // tm, N // tn),
        in_specs=[pl.BlockSpec((tm, K1), lambda i, j: (i, 0)),
                  pl.BlockSpec((tm, K2), lambda i, j: (i, 0)),
                  pl.BlockSpec((K1 + K2, tn), lambda i, j: (0, j)),
                  pl.BlockSpec((tm, tn), lambda i, j: (i, j)),
                  pl.BlockSpec((None, 2, tn), lambda i, j: (i // tpb, 0, j))],
        out_specs=pl.BlockSpec((tm, tn), lambda i, j: (i, j)),
        out_shape=jax.ShapeDtypeStruct((M, N), jnp.float32),
        compiler_params=_cparams(("parallel", "parallel")),
        name="matmul2_resgate",
    )(a1, a2, b, res.reshape(M, N), gate)
    return out.reshape(B, T, N)


def _moe_combine_kernel(x_ref, y0_ref, y1_ref, gate_ref, o_ref):
    y = y0_ref[...].astype(jnp.float32) + y1_ref[...].astype(jnp.float32)
    o_ref[...] = x_ref[...] + gate_ref[...] * y


def moe_combine(x, y0, y1, gate, *, n_rows):
    B, T, D = x.shape
    nt = T // NORM_TILE
    row = pl.BlockSpec((None, NORM_TILE, D), lambda b, t: (b, t, 0))
    return pl.pallas_call(
        _moe_combine_kernel,
        grid=(B, n_rows // NORM_TILE),
        in_specs=[row, row, row, pl.BlockSpec((None, None, 1, D), lambda b, t: (b, t // (nt - 1), 0, 0))],
        out_specs=row,
        out_shape=jax.ShapeDtypeStruct((B, n_rows, D), jnp.float32),
        compiler_params=_cparams(("parallel", "parallel")),
        name="moe_combine",
    )(x, y0, y1, gate.reshape(B, 2, 1, D))


def _window_attn_kernel(sink_ref, q_ref, kc_ref, kp_ref, ko_ref, kn_ref, vc_ref, vp_ref, vo_ref, vn_ref, o_ref,
                        *, n_lat, n_ctx):
    n = pl.program_id(1)
    h = pl.program_id(2)
    G = WIN_GROUP
    q = q_ref[...]
    qs = jnp.concatenate([q[:, g * HEAD_DIM:(g + 1) * HEAD_DIM] for g in range(G)], axis=0)
    k = jnp.concatenate([kc_ref[...], kp_ref[...], ko_ref[...], kn_ref[...]], axis=0)
    v = jnp.concatenate([vc_ref[...], vp_ref[...], vo_ref[...], vn_ref[...]], axis=0)
    s = lax.dot_general(qs, k, (((1,), (1,)), ((), ())), preferred_element_type=jnp.float32)
    nk = n_ctx + 3 * BLOCK
    col = lax.broadcasted_iota(jnp.int32, (G * BLOCK, nk), 1)
    qpos = n * BLOCK + lax.broadcasted_iota(jnp.int32, (G * BLOCK, nk), 0) % BLOCK
    kpos = (n - 1) * BLOCK + (col - n_ctx)
    valid = (col < n_ctx) | ((jnp.abs(qpos - kpos) <= WINDOW) & (kpos >= 0) & (kpos < n_lat) & (qpos < n_lat))
    s = jnp.where(valid, s, NEG_INF)
    sink = jnp.concatenate(
        [jnp.full((BLOCK, 1), sink_ref[h * G + g], jnp.float32) for g in range(G)], axis=0)
    m = jnp.maximum(jnp.max(s, axis=-1, keepdims=True), sink)
    e = jnp.exp(s - m)
    denom = jnp.sum(e, axis=-1, keepdims=True) + jnp.exp(sink - m)
    p = (e / denom).astype(v.dtype)
    o = jnp.dot(p, v, preferred_element_type=jnp.float32)
    for g in range(G):
        o_ref[:, g * HEAD_DIM:(g + 1) * HEAD_DIM] = o[g * BLOCK:(g + 1) * BLOCK, :].astype(o_ref.dtype)


def window_attention(q, k, v, sink, *, n_ctx, v_col0=0):
    B, T, _ = q.shape
    L = T - n_ctx
    nb = L // BLOCK
    kv_blk = lambda f: pl.BlockSpec((None, BLOCK, HEAD_DIM), f)
    ctx_blk = lambda off: pl.BlockSpec((None, n_ctx, HEAD_DIM), lambda b, n, h: (b, L // n_ctx, off + h))
    prev_blk = lambda off: kv_blk(lambda b, n, h: (b, jnp.maximum(n - 1, 0), off + h))
    own_blk = lambda off: kv_blk(lambda b, n, h: (b, n, off + h))
    next_blk = lambda off: kv_blk(lambda b, n, h: (b, jnp.minimum(n + 1, nb - 1), off + h))
    voff = v_col0 // HEAD_DIM
    return pl.pallas_call(
        functools.partial(_window_attn_kernel, n_lat=L, n_ctx=n_ctx),
        grid=(B, T // BLOCK, WIN_KV_HEADS),
        in_specs=[pl.BlockSpec(memory_space=pltpu.SMEM),
                  pl.BlockSpec((None, BLOCK, WIN_GROUP * HEAD_DIM), lambda b, n, h: (b, n, h)),
                  ctx_blk(0), prev_blk(0), own_blk(0), next_blk(0),
                  ctx_blk(voff), prev_blk(voff), own_blk(voff), next_blk(voff)],
        out_specs=pl.BlockSpec((None, BLOCK, WIN_GROUP * HEAD_DIM), lambda b, n, h: (b, n, h)),
        out_shape=jax.ShapeDtypeStruct((B, T, WIN_Q), jnp.bfloat16),
        compiler_params=_cparams(("parallel", "parallel", "parallel")),
        name="window_attention",
    )(sink.astype(jnp.float32), q, k, k, k, k, v, v, v, v)


def _diff_attn_kernel(lam_ref, q_ref, k_ref, v_ref, g_ref, buf_ref, o_ref, m_ref, l_ref, acc_ref,
                      *, n_keys, tk, out_scale):
    del buf_ref
    m_ref[...] = jnp.full(m_ref.shape, NEG_INF, jnp.float32)
    l_ref[...] = jnp.zeros(l_ref.shape, jnp.float32)
    acc_ref[...] = jnp.zeros(acc_ref.shape, jnp.float32)

    def step(j, carry):
        ks = pl.multiple_of(j * tk, tk)
        v = v_ref[pl.ds(ks, tk), :]
        scores = []
        for sub in range(2):
            q = q_ref[:, sub * DIFF_DIM:(sub + 1) * DIFF_DIM]
            k = k_ref[pl.ds(ks, tk), sub * DIFF_DIM:(sub + 1) * DIFF_DIM]
            scores.append(lax.dot_general(q, k, (((1,), (1,)), ((), ())), preferred_element_type=jnp.float32))
        for sub in range(2):
            s = scores[sub]
            m_old = m_ref[sub]
            m_new = jnp.maximum(m_old, jnp.max(s, axis=-1, keepdims=True))
            alpha = jnp.exp2(m_old - m_new)
            p = jnp.exp2(s - m_new)
            l_ref[sub] = alpha * l_ref[sub] + jnp.sum(p, axis=-1, keepdims=True)
            acc_ref[sub] = alpha * acc_ref[sub] + jnp.dot(p.astype(v.dtype), v, preferred_element_type=jnp.float32)
            m_ref[sub] = m_new
        return carry

    lax.fori_loop(0, n_keys // tk, step, 0)
    lam = lam_ref[0]
    o = acc_ref[0] / l_ref[0] - lam * (acc_ref[1] / l_ref[1])
    o = o * lax.rsqrt(jnp.mean(o * o, axis=-1, keepdims=True) + EPS)
    o_ref[...] = (o * g_ref[...] * out_scale).astype(o_ref.dtype)


def diff_attention(q, k, v, lam, subln_g, out_buf, *, n_q, q_blk0, n_keys, key_blk, tq, tk, out_scale, v_col0=0):
    B = q.shape[0]
    W = 2 * DIFF_DIM
    voff = v_col0 // W
    return pl.pallas_call(
        functools.partial(_diff_attn_kernel, n_keys=n_keys, tk=tk, out_scale=out_scale),
        grid=(B, DIFF_HEADS, n_q // tq),
        in_specs=[pl.BlockSpec(memory_space=pltpu.SMEM),
                  pl.BlockSpec((None, tq, W), lambda b, h, i: (b, q_blk0 + i, h)),
                  pl.BlockSpec((None, n_keys, W), lambda b, h, i: (b, key_blk, h)),
                  pl.BlockSpec((None, n_keys, W), lambda b, h, i: (b, key_blk, voff + h)),
                  pl.BlockSpec((1, W), lambda b, h, i: (0, 0)),
                  pl.BlockSpec(memory_space=pl.ANY)],
        out_specs=pl.BlockSpec((None, tq, W), lambda b, h, i: (b, q_blk0 + i, h)),
        scratch_shapes=[pltpu.VMEM((2, tq, 1), jnp.float32),
                        pltpu.VMEM((2, tq, 1), jnp.float32),
                        pltpu.VMEM((2, tq, W), jnp.float32)],
        out_shape=jax.ShapeDtypeStruct(out_buf.shape, out_buf.dtype),
        input_output_aliases={5: 0},
        compiler_params=_cparams(("parallel", "parallel", "arbitrary")),
        name="diff_attention",
    )(lam.reshape(1).astype(jnp.float32), q, k, v, subln_g.reshape(1, W).astype(jnp.float32), out_buf)


def _moe_up_kernel(te_ref, tv_ref, x_ref, w1_ref, w3_ref, o_ref):
    i = pl.program_id(1)

    @pl.when(tv_ref[i] > 0)
    def _():
        x = x_ref[...]
        h1 = jnp.dot(x, w1_ref[...], preferred_element_type=jnp.float32)
        h3 = jnp.dot(x, w3_ref[...], preferred_element_type=jnp.float32)
        o_ref[...] = (h1 * jax.nn.sigmoid(h1) * h3).astype(o_ref.dtype)

    @pl.when(tv_ref[i] == 0)
    def _():
        o_ref[...] = jnp.zeros(o_ref.shape, o_ref.dtype)


def _moe_down_kernel(te_ref, tv_ref, g_ref, w2_ref, rg_ref, o_ref):
    i = pl.program_id(1)

    @pl.when(tv_ref[i] > 0)
    def _():
        y = jnp.dot(g_ref[...], w2_ref[...], preferred_element_type=jnp.float32)
        o_ref[...] = (rg_ref[...] * y).astype(o_ref.dtype)

    @pl.when(tv_ref[i] == 0)
    def _():
        o_ref[...] = jnp.zeros(o_ref.shape, o_ref.dtype)


def moe_experts(xs, w1, w3, w2, tile_expert, tile_valid, row_gate, *, tn_up=768, tn_down=1024):
    P, D = xs.shape
    F = w1.shape[2]
    tm = MOE_TILE
    up = pl.pallas_call(
        _moe_up_kernel,
        grid_spec=pltpu.PrefetchScalarGridSpec(
            num_scalar_prefetch=2,
            grid=(F // tn_up, P // tm),
            in_specs=[pl.BlockSpec((tm, D), lambda j, i, te, tv: (i, 0)),
                      pl.BlockSpec((None, D, tn_up), lambda j, i, te, tv: (te[i], 0, j)),
                      pl.BlockSpec((None, D, tn_up), lambda j, i, te, tv: (te[i], 0, j))],
            out_specs=pl.BlockSpec((tm, tn_up), lambda j, i, te, tv: (i, j)),
        ),
        out_shape=jax.ShapeDtypeStruct((P, F), jnp.bfloat16),
        compiler_params=_cparams(("parallel", "arbitrary")),
        name="moe_up",
    )(tile_expert, tile_valid, xs, w1, w3)
    return pl.pallas_call(
        _moe_down_kernel,
        grid_spec=pltpu.PrefetchScalarGridSpec(
            num_scalar_prefetch=2,
            grid=(D // tn_down, P // tm),
            in_specs=[pl.BlockSpec((tm, F), lambda j, i, te, tv: (i, 0)),
                      pl.BlockSpec((None, F, tn_down), lambda j, i, te, tv: (te[i], 0, j)),
                      pl.BlockSpec((tm, 1), lambda j, i, te, tv: (i, 0))],
            out_specs=pl.BlockSpec((tm, tn_down), lambda j, i, te, tv: (i, j)),
        ),
        out_shape=jax.ShapeDtypeStruct((P, D), jnp.bfloat16),
        compiler_params=_cparams(("parallel", "arbitrary")),
        name="moe_down",
    )(tile_expert, tile_valid, up, w2, row_gate)


def moe_layer(h2, router_w, router_b, w1, w3, w2):
    N, D = h2.shape
    tm = MOE_TILE
    logits = jnp.dot(h2.astype(jnp.float32), router_w, precision=HI) + router_b
    top_v, top_i = lax.top_k(logits, TOP_K)
    gates = jax.nn.softmax(top_v, axis=-1)
    A = N * TOP_K
    e_flat = top_i.reshape(A).astype(jnp.int32)
    order = jnp.argsort(e_flat, stable=True)
    e_sorted = e_flat[order]
    counts = jnp.sum(e_flat[:, None] == jnp.arange(N_EXPERTS, dtype=jnp.int32)[None, :], axis=0).astype(jnp.int32)
    padded = ((counts + tm - 1) // tm) * tm
    start_unpadded = jnp.cumsum(counts) - counts
    start_padded = jnp.cumsum(padded) - padded
    dest_sorted = start_padded[e_sorted] + (jnp.arange(A, dtype=jnp.int32) - start_unpadded[e_sorted])
    P = A + N_EXPERTS * tm
    tile_start = jnp.arange(P // tm, dtype=jnp.int32) * tm
    ends = jnp.cumsum(padded)
    tile_expert = jnp.minimum(jnp.sum(tile_start[:, None] >= ends[None, :], axis=1), N_EXPERTS - 1).astype(jnp.int32)
    tile_valid = (tile_start < ends[-1]).astype(jnp.int32)
    rows = jnp.arange(P, dtype=jnp.int32)
    row_e = tile_expert[rows // tm]
    row_idx = rows - start_padded[row_e]
    row_ok = (row_idx < counts[row_e]) & (rows < ends[-1])
    src = order[jnp.clip(start_unpadded[row_e] + row_idx, 0, A - 1)]
    row_token = jnp.where(row_ok, src // TOP_K, 0).astype(jnp.int32)
    row_gate = jnp.where(row_ok, gates.reshape(A)[src], 0.0)
    pos = dest_sorted[jnp.argsort(order)].reshape(N, TOP_K)
    take = lambda a, idx: a.at[idx].get(mode="promise_in_bounds")
    xs = take(h2, row_token)
    y = moe_experts(xs, w1, w3, w2, tile_expert, tile_valid, row_gate.reshape(P, 1))
    return take(y, pos[:, 0]), take(y, pos[:, 1])


FFT_R = 128
FFT_K1 = 72
Z_PITCH = FFT_R + 8
S1_PITCH = 2 * FFT_K1 + 8
S2_PITCH = 2 * FFT_R + 8
HY_LANES = 128
FFT_UNROLL = 16


def _dft_tables():
    R, K1 = FFT_R, FFT_K1
    N = R * R
    i32 = jnp.int32
    b = jnp.arange(R, dtype=i32)[:, None, None]
    k1 = jnp.arange(K1, dtype=i32)[None, :, None]
    a = jnp.arange(R, dtype=i32)[None, None, :]
    th = (2.0 * math.pi / N) * ((k1 * (R * a + b)) % N).astype(jnp.float32)
    f1 = jnp.concatenate([jnp.cos(th), -jnp.sin(th)], axis=1)
    w = jnp.where(jnp.arange(K1) > R // 2, 0.0, jnp.where((jnp.arange(K1) % (R // 2)) == 0, 1.0, 2.0)) / N
    the = jnp.swapaxes(th[:, :, :R // 2], 1, 2)
    e = jnp.concatenate([w * jnp.cos(the), -w * jnp.sin(the)], axis=2)
    k2 = jnp.arange(R, dtype=i32)
    ph = (2.0 * math.pi / R) * ((k2[:, None] * k2[None, :]) % R).astype(jnp.float32)
    c, s = jnp.cos(ph), jnp.sin(ph)
    g = jnp.block([[c, s], [-s, c]])
    ginv = jnp.block([[c, -s], [s, c]])
    bf = jnp.bfloat16
    return f1.astype(bf), e.astype(bf), g.astype(bf), ginv.astype(bf)


def _fft_stage1(src_ref, f1_ref, s1_ref, n_a):
    def body(b, carry):
        zb = src_ref[pl.ds(b, n_a, stride=Z_PITCH), :]
        s1_ref[pl.ds(pl.multiple_of(b * S1_PITCH, 8), 2 * FFT_K1), :] = jnp.dot(
            f1_ref[b], zb.astype(jnp.bfloat16), preferred_element_type=jnp.float32)
        return carry
    lax.fori_loop(0, FFT_R, body, 0, unroll=FFT_UNROLL)


def _fft_stage2(s1_ref, g_ref, k1):
    are = s1_ref[pl.ds(k1, FFT_R, stride=S1_PITCH), :]
    aim = s1_ref[pl.ds(FFT_K1 + k1, FFT_R, stride=S1_PITCH), :]
    r = jnp.concatenate([are, aim], axis=0).astype(jnp.bfloat16)
    return jnp.dot(g_ref[...], r, preferred_element_type=jnp.float32)


def _filter_fft_kernel(f_ref, f1_ref, g_ref, h_ref, s1_ref):
    _fft_stage1(f_ref, f1_ref, s1_ref, FFT_R)

    def body(k1, carry):
        h_ref[k1] = _fft_stage2(s1_ref, g_ref, k1).astype(h_ref.dtype)
        return carry
    lax.fori_loop(0, FFT_K1, body, 0, unroll=FFT_UNROLL)


def filter_spectrum(filt_padded, f1, g):
    rows, width = filt_padded.shape
    nt = width // HY_LANES
    once = pl.Buffered(1)
    return pl.pallas_call(
        _filter_fft_kernel,
        grid=(nt,),
        in_specs=[pl.BlockSpec((rows, HY_LANES), lambda c: (0, c)),
                  pl.BlockSpec(f1.shape, lambda c: (0, 0, 0), pipeline_mode=once),
                  pl.BlockSpec(g.shape, lambda c: (0, 0), pipeline_mode=once)],
        out_specs=pl.BlockSpec((None, FFT_K1, 2 * FFT_R, HY_LANES), lambda c: (c, 0, 0, 0)),
        out_shape=jax.ShapeDtypeStruct((nt, FFT_K1, 2 * FFT_R, HY_LANES), jnp.bfloat16),
        scratch_shapes=[pltpu.VMEM((FFT_R * S1_PITCH, HY_LANES), jnp.float32)],
        compiler_params=_cparams(("arbitrary",)),
        name="hyena_filter_fft",
    )(filt_padded, f1, g)


def _short_conv_chunks(x_ref, w_ref, b_ref, dst_ref, n_chunks, pitch=Z_PITCH):
    R = FFT_R
    w = w_ref[...]
    bias = b_ref[...]
    row = lax.broadcasted_iota(jnp.int32, (R, HY_LANES), 0)
    for a in range(n_chunks):
        xc = x_ref[a * R:(a + 1) * R, :].astype(jnp.float32)
        prev = pltpu.roll(xc, 1, axis=0)
        nxt = pltpu.roll(xc, R - 1, axis=0)
        if a > 0:
            last = x_ref[a * R - 16:a * R, :].astype(jnp.float32)[15:16, :]
        else:
            last = jnp.zeros((1, HY_LANES), jnp.float32)
        if a < n_chunks - 1:
            first = x_ref[(a + 1) * R:(a + 1) * R + 16, :].astype(jnp.float32)[0:1, :]
        else:
            first = jnp.zeros((1, HY_LANES), jnp.float32)
        prev = jnp.where(row == 0, last, prev)
        nxt = jnp.where(row == R - 1, first, nxt)
        dst_ref[a * pitch:a * pitch + R, :] = bias + prev * w[0:1, :] + xc * w[1:2, :] + nxt * w[2:3, :]


def _hyena_conv_kernel(v_ref, x_ref, wv_ref, bv_ref, wx_ref, bx_ref, skip_ref, h_ref, f1_ref, e_ref, g_ref, gi_ref,
                       buf_ref, o_ref, z_ref, gate_ref, s1_ref, s2_ref, *, n_chunks):
    del buf_ref
    n = pl.program_id(2)
    R, K1 = FFT_R, FFT_K1

    @pl.when(n == 0)
    def _():
        _short_conv_chunks(v_ref, wv_ref, bv_ref, z_ref, n_chunks)

    _short_conv_chunks(x_ref, wx_ref, bx_ref, gate_ref, n_chunks)

    _fft_stage1(z_ref, f1_ref, s1_ref, n_chunks)

    def mid(k1, carry):
        x = _fft_stage2(s1_ref, g_ref, k1)
        h = h_ref[k1].astype(jnp.float32)
        xr, xi, hr, hi = x[:R], x[R:], h[:R], h[R:]
        p = jnp.concatenate([xr * hr - xi * hi, xr * hi + xi * hr], axis=0).astype(jnp.bfloat16)
        s2_ref[pl.ds(pl.multiple_of(k1 * S2_PITCH, 8), 2 * R), :] = jnp.dot(
            gi_ref[...], p, preferred_element_type=jnp.float32)
        return carry
    lax.fori_loop(0, K1, mid, 0, unroll=FFT_UNROLL)

    skip = skip_ref[...]

    def last(b, carry):
        qre = s2_ref[pl.ds(b, K1, stride=S2_PITCH), :]
        qim = s2_ref[pl.ds(R + b, K1, stride=S2_PITCH), :]
        r = jnp.concatenate([qre, qim], axis=0).astype(jnp.bfloat16)
        conv = jnp.dot(e_ref[b], r, preferred_element_type=jnp.float32)
        zold = z_ref[pl.ds(b, n_chunks, stride=Z_PITCH), :]
        gate = gate_ref[pl.ds(b, n_chunks, stride=Z_PITCH), :]
        z_ref[pl.ds(b, n_chunks, stride=Z_PITCH), :] = gate * (conv + zold * skip)
        return carry
    lax.fori_loop(0, R, last, 0, unroll=FFT_UNROLL)

    @pl.when(n == HY_ORDER - 1)
    def _():
        for a in range(n_chunks):
            o_ref[a * R:(a + 1) * R, :] = z_ref[a * Z_PITCH:a * Z_PITCH + R, :].astype(o_ref.dtype)


def hyena_conv(proj, conv_w, conv_b, skip, spec, tables, out_buf, *, n_lat):
    f1, e, g, ginv = tables
    B = proj.shape[0]
    n_chunks = n_lat // FFT_R
    nt = HY_CH // HY_LANES
    once = pl.Buffered(1)
    grp = lambda n: (1 + n) * nt
    f1h = f1[:, :, :n_chunks]
    return pl.pallas_call(
        functools.partial(_hyena_conv_kernel, n_chunks=n_chunks),
        grid=(B, nt, HY_ORDER),
        in_specs=[pl.BlockSpec((None, n_lat, HY_LANES), lambda b, c, n: (b, 0, c)),
                  pl.BlockSpec((None, n_lat, HY_LANES), lambda b, c, n: (b, 0, grp(n) + c)),
                  pl.BlockSpec((HY_SHORT, HY_LANES), lambda b, c, n: (0, c)),
                  pl.BlockSpec((1, HY_LANES), lambda b, c, n: (0, c)),
                  pl.BlockSpec((HY_SHORT, HY_LANES), lambda b, c, n: (0, grp(n) + c)),
                  pl.BlockSpec((1, HY_LANES), lambda b, c, n: (0, grp(n) + c)),
                  pl.BlockSpec((None, 1, HY_LANES), lambda b, c, n: (n, 0, c)),
                  pl.BlockSpec((None, FFT_K1, 2 * FFT_R, HY_LANES), lambda b, c, n: (n * nt + c, 0, 0, 0),
                               pipeline_mode=once),
                  pl.BlockSpec(f1h.shape, lambda b, c, n: (0, 0, 0), pipeline_mode=once),
                  pl.BlockSpec(e.shape, lambda b, c, n: (0, 0, 0), pipeline_mode=once),
                  pl.BlockSpec(g.shape, lambda b, c, n: (0, 0), pipeline_mode=once),
                  pl.BlockSpec(ginv.shape, lambda b, c, n: (0, 0), pipeline_mode=once),
                  pl.BlockSpec(memory_space=pl.ANY)],
        out_specs=pl.BlockSpec((None, n_lat, HY_LANES), lambda b, c, n: (b, 0, c)),
        out_shape=jax.ShapeDtypeStruct(out_buf.shape, out_buf.dtype),
        input_output_aliases={12: 0},
        scratch_shapes=[pltpu.VMEM((n_chunks * Z_PITCH, HY_LANES), jnp.float32),
                        pltpu.VMEM((n_chunks * Z_PITCH, HY_LANES), jnp.float32),
                        pltpu.VMEM((FFT_R * S1_PITCH, HY_LANES), jnp.float32),
                        pltpu.VMEM((FFT_K1 * S2_PITCH, HY_LANES), jnp.float32)],
        compiler_params=_cparams(("parallel", "parallel", "arbitrary")),
        name="hyena_conv",
    )(proj, proj, conv_w, conv_b.reshape(1, -1), conv_w, conv_b.reshape(1, -1),
      skip.reshape(HY_ORDER, 1, HY_CH), spec, f1h, e, g, ginv, out_buf)


def _ctx_dft_tables(n_ctx):
    N = 2 * n_ctx
    nk = -(-(n_ctx + 1) // 16) * 16
    k = jnp.arange(nk, dtype=jnp.int32)[:, None]
    t = jnp.arange(N, dtype=jnp.int32)[None, :]
    th = (2.0 * math.pi / N) * ((k * t) % N).astype(jnp.float32)
    fwd = jnp.concatenate([jnp.cos(th), -jnp.sin(th)], axis=0)
    w = jnp.where(k > n_ctx, 0.0, jnp.where((k % n_ctx) == 0, 1.0, 2.0)) / N
    inv = jnp.concatenate([(w * jnp.cos(th[:, :n_ctx])).T, (-w * jnp.sin(th[:, :n_ctx])).T], axis=1)
    return fwd.astype(jnp.bfloat16), inv.astype(jnp.bfloat16)


def _hyena_ctx_kernel(v_ref, x1_ref, x2_ref, wv_ref, bv_ref, w1_ref, b1_ref, w2_ref, b2_ref, skip_ref,
                      f0_ref, f1_ref, fwd_ref, inv_ref, buf_ref, o_ref, z_ref, g1_ref, g2_ref, *, n_ctx):
    del buf_ref
    nc = n_ctx // FFT_R
    _short_conv_chunks(v_ref, wv_ref, bv_ref, z_ref, nc, pitch=FFT_R)
    _short_conv_chunks(x1_ref, w1_ref, b1_ref, g1_ref, nc, pitch=FFT_R)
    _short_conv_chunks(x2_ref, w2_ref, b2_ref, g2_ref, nc, pitch=FFT_R)
    nk = fwd_ref.shape[0] // 2
    z = z_ref[...]
    for n, (filt_ref, gate_ref) in enumerate(((f0_ref, g1_ref), (f1_ref, g2_ref))):
        h = jnp.dot(fwd_ref[...], filt_ref[...].astype(jnp.bfloat16), preferred_element_type=jnp.float32)
        x = jnp.dot(fwd_ref[:, :n_ctx], z.astype(jnp.bfloat16), preferred_element_type=jnp.float32)
        xr, xi, hr, hi = x[:nk], x[nk:], h[:nk], h[nk:]
        p = jnp.concatenate([xr * hr - xi * hi, xr * hi + xi * hr], axis=0).astype(jnp.bfloat16)
        conv = jnp.dot(inv_ref[...], p, preferred_element_type=jnp.float32)
        z = gate_ref[...] * (conv + z * skip_ref[n:n + 1, :])
    o_ref[...] = z.astype(o_ref.dtype)


def hyena_ctx(proj, conv_w, conv_b, skip, filt, tables, out_buf, *, n_lat, n_ctx):
    fwd, inv = tables
    B = proj.shape[0]
    nt = HY_CH // HY_LANES
    rb = n_lat // n_ctx
    cb2 = conv_b.reshape(1, -1)
    row = lambda g: pl.BlockSpec((None, n_ctx, HY_LANES), lambda b, c: (b, rb, g * nt + c))
    wsp = lambda g: pl.BlockSpec((HY_SHORT, HY_LANES), lambda b, c: (0, g * nt + c))
    bsp = lambda g: pl.BlockSpec((1, HY_LANES), lambda b, c: (0, g * nt + c))
    fsp = lambda n: pl.BlockSpec((2 * n_ctx, HY_LANES), lambda b, c: (0, n * nt + c))
    scr = pltpu.VMEM((n_ctx, HY_LANES), jnp.float32)
    return pl.pallas_call(
        functools.partial(_hyena_ctx_kernel, n_ctx=n_ctx),
        grid=(B, nt),
        in_specs=[row(0), row(1), row(2), wsp(0), bsp(0), wsp(1), bsp(1), wsp(2), bsp(2),
                  pl.BlockSpec((HY_ORDER, HY_LANES), lambda b, c: (0, c)),
                  fsp(0), fsp(1),
                  pl.BlockSpec(fwd.shape, lambda b, c: (0, 0)),
                  pl.BlockSpec(inv.shape, lambda b, c: (0, 0)),
                  pl.BlockSpec(memory_space=pl.ANY)],
        out_specs=pl.BlockSpec((None, n_ctx, HY_LANES), lambda b, c: (b, rb, c)),
        out_shape=jax.ShapeDtypeStruct(out_buf.shape, out_buf.dtype),
        input_output_aliases={14: 0},
        scratch_shapes=[scr, scr, scr],
        compiler_params=_cparams(("parallel", "parallel")),
        name="hyena_ctx",
    )(proj, proj, proj, conv_w, cb2, conv_w, cb2, conv_w, cb2, skip, filt, filt, fwd, inv, out_buf)


def _norm_rope_kernel(x_ref, g_ref, cos_ref, sin_ref, o_ref, *, scale, nh):
    cos = cos_ref[...]
    sin = sin_ref[...]
    g = g_ref[...]
    lane = lax.broadcasted_iota(jnp.int32, cos.shape, 1)
    lower = (lane % (HEAD_DIM // 2)) < (HEAD_DIM // 4)
    for h in range(nh):
        x = x_ref[:, h * HEAD_DIM:(h + 1) * HEAD_DIM].astype(jnp.float32)
        y = x * lax.rsqrt(jnp.mean(x * x, axis=-1, keepdims=True) + EPS) * g
        rot = jnp.where(lower, -pltpu.roll(y, HEAD_DIM - HEAD_DIM // 4, axis=1), pltpu.roll(y, HEAD_DIM // 4, axis=1))
        o_ref[:, h * HEAD_DIM:(h + 1) * HEAD_DIM] = ((y * cos + rot * sin) * scale).astype(o_ref.dtype)


def head_norm_rope(t, col0, width, g, cos, sin, scale, *, tr=ROW_TILE, nh=4):
    B, T, _ = t.shape
    wb = nh * HEAD_DIM
    c0 = col0 // wb
    return pl.pallas_call(
        functools.partial(_norm_rope_kernel, scale=scale, nh=nh),
        grid=(B, T // tr, width // wb),
        in_specs=[pl.BlockSpec((None, tr, wb), lambda b, r, j: (b, r, c0 + j)),
                  pl.BlockSpec((1, HEAD_DIM), lambda b, r, j: (0, 0)),
                  pl.BlockSpec((tr, HEAD_DIM), lambda b, r, j: (r, 0)),
                  pl.BlockSpec((tr, HEAD_DIM), lambda b, r, j: (r, 0))],
        out_specs=pl.BlockSpec((None, tr, wb), lambda b, r, j: (b, r, j)),
        out_shape=jax.ShapeDtypeStruct((B, T, width), jnp.bfloat16),
        compiler_params=_cparams(("parallel", "parallel", "parallel")),
        name="head_norm_rope",
    )(t, g.reshape(1, HEAD_DIM).astype(jnp.float32), cos, sin)


def _rope_tables(L, n_ctx):
    rows = jnp.repeat(jnp.arange(L // GRID_W), GRID_W)
    cols = jnp.tile(jnp.arange(GRID_W), L // GRID_W)
    quarter = HEAD_DIM // 4
    inv = ROPE_BASE ** (-jnp.arange(quarter, dtype=jnp.float32) / quarter)
    ar = rows.astype(jnp.float32)[:, None] * inv
    ac = cols.astype(jnp.float32)[:, None] * inv
    ang = jnp.concatenate([ar, ar, ac, ac], axis=-1)
    cos = jnp.concatenate([jnp.cos(ang), jnp.ones((n_ctx, HEAD_DIM), jnp.float32)], axis=0)
    sin = jnp.concatenate([jnp.sin(ang), jnp.zeros((n_ctx, HEAD_DIM), jnp.float32)], axis=0)
    return cos, sin


def _implicit_filters(L, w_in, w_hid, b, freq, w_out, pitch=FFT_R):
    f32 = jnp.float32
    t = jnp.linspace(0.0, 1.0, L, dtype=f32)[:, None]
    w = (2.0 * math.pi / L) * jnp.arange(L, dtype=f32)[:, None]
    f = jnp.linspace(1e-4, HY_BANDS - 1, HY_BANDS, dtype=f32)[None, :]
    z = jnp.concatenate([t, jnp.cos(f * w), -jnp.sin(f * w)], axis=-1)
    h = jnp.sin(freq[0] * (jnp.dot(z, w_in, precision=HI) + b[0]))
    for n in range(HY_FILTER_HIDDEN_LAYERS):
        h = jnp.sin(freq[n + 1] * (jnp.dot(h, w_hid[n], precision=HI) + b[n + 1]))
    width = HY_ORDER * HY_CH
    max_decay = math.log(HY_DECAY_TARGET) / HY_FAST_DECAY
    min_decay = math.log(HY_DECAY_TARGET) / HY_SLOW_DECAY
    deltas = jnp.abs(jnp.linspace(min_decay, max_decay, width, dtype=f32))[None, :]
    hb = jnp.concatenate([jnp.zeros((1, h.shape[1]), f32), h[:0:-1]], axis=0)
    tb = jnp.concatenate([jnp.zeros((1, 1), f32), t[:0:-1]], axis=0)
    chunk = lambda a: jnp.pad(a.reshape(2, L // FFT_R, FFT_R, a.shape[-1]),
                              ((0, 0), (0, 0), (0, pitch - FFT_R), (0, 0))).reshape(2, -1, a.shape[-1])
    hh = chunk(jnp.stack([h, hb]))
    tt = chunk(jnp.stack([t, tb]))
    w2 = jnp.stack([w_out[:, :width], w_out[:, width:]])
    filt = jnp.einsum('hrk,hkw->hrw', hh, w2, precision=HI) * jnp.exp(-tt * deltas)
    return filt.reshape(-1, width)


def _ada_modulation(cv, down, up, b):
    m = jnp.dot(jnp.dot(jax.nn.silu(cv), down, precision=HI), up, precision=HI) + b
    return m.reshape(m.shape[:-1] + (N_MOD, m.shape[-1] // N_MOD))


def kernel(x, c, ctx, c_ctx, norm_g, ada_down, ada_up, ada_b, ev_w_in, ev_conv_w, ev_conv_b, ev_filt_w_in, ev_filt_w_hid, ev_filt_b, ev_filt_freq, ev_filt_w_out, ev_hy_skip, ev_qk_g, ev_sink, ev_w_out, ev_ffn_w1, ev_ffn_w3, ev_ffn_w2, od_w_qkv, od_qk_g, od_lambda, od_subln_g, od_w_out, od_router_w, od_router_b, od_moe_w1, od_moe_w3, od_moe_w2):
    B, L, D = x.shape
    Lc = ctx.shape[1]
    T = L + Lc
    bf16 = jnp.bfloat16
    cos, sin = _rope_tables(L, Lc)
    tables = _dft_tables()
    ctx_tables = _ctx_dft_tables(Lc)
    X = jnp.concatenate([x, ctx], axis=1)
    qk_scale = HEAD_DIM ** -0.5

    for i in range(DEPTH):
        j = i // 2
        m_l = _ada_modulation(c, ada_down[i], ada_up[i], ada_b[i])
        m_c = _ada_modulation(c_ctx, ada_down[i], ada_up[i], ada_b[i])
        mods = jnp.stack([m_l, jnp.broadcast_to(m_c[None], (B, N_MOD, D))], axis=1)

        h = norm_mod(X, norm_g[i, 0], mods[:, :, 0], mods[:, :, 1])
        hf = h.reshape(B * T, D)
        if i % 2 == 0:
            proj = matmul(hf, cast_layer(ev_w_in, j)).reshape(B, T, -1)
            v_col0 = HY_WIDTH + WIN_Q + WIN_KV
            q = head_norm_rope(proj, HY_WIDTH, WIN_Q, ev_qk_g[j, 0], cos, sin, qk_scale)
            k = head_norm_rope(proj, HY_WIDTH + WIN_Q, WIN_KV, ev_qk_g[j, 1], cos, sin, 1.0)
            filt_args = (ev_filt_w_in[j], ev_filt_w_hid[j], ev_filt_b[j], ev_filt_freq[j], ev_filt_w_out[j])
            spec = filter_spectrum(_implicit_filters(L, *filt_args, pitch=Z_PITCH), tables[0], tables[2])
            hy = jnp.zeros((B, T, HY_CH), bf16)
            hy = hyena_conv(proj, ev_conv_w[j], ev_conv_b[j], ev_hy_skip[j], spec, tables, hy, n_lat=L)
            hy = hyena_ctx(proj, ev_conv_w[j], ev_conv_b[j], ev_hy_skip[j], _implicit_filters(Lc, *filt_args),
                           ctx_tables, hy, n_lat=L, n_ctx=Lc)
            att = window_attention(q, k, proj, ev_sink[j], n_ctx=Lc, v_col0=v_col0)
            X = matmul2_resgate(hy.reshape(B * T, HY_CH), att.reshape(B * T, WIN_Q), cast_layer(ev_w_out, j),
                                X, mods[:, :, 2], n_lat=L)
        else:
            lam_init = 0.8 - 0.6 * math.exp(-0.3 * i)
            qkv = matmul(hf, cast_layer(od_w_qkv, j)).reshape(B, T, -1)
            q = head_norm_rope(qkv, 0, DIFF_Q, od_qk_g[j, 0], cos, sin, DIFF_DIM ** -0.5 * math.log2(math.e), nh=8)
            k = head_norm_rope(qkv, DIFF_Q, DIFF_Q, od_qk_g[j, 1], cos, sin, 1.0, nh=8)
            lp = od_lambda[j].astype(jnp.float32)
            lam = jnp.exp(jnp.sum(lp[0] * lp[1])) - jnp.exp(jnp.sum(lp[2] * lp[3])) + lam_init
            attn = functools.partial(diff_attention, q, k, qkv, lam, od_subln_g[j], v_col0=2 * DIFF_Q,
                                     out_scale=1.0 - lam_init)
            o = attn(h, n_q=L, q_blk0=0, n_keys=T, key_blk=0, tq=512, tk=T // 6)
            o = attn(o, n_q=Lc, q_blk0=L // Lc, n_keys=Lc, key_blk=L // Lc, tq=Lc, tk=Lc)
            X = matmul_resgate(o.reshape(B * T, D), cast_layer(od_w_out, j), X, mods[:, :, 2], n_lat=L)
        h2 = norm_mod(X, norm_g[i, 1], mods[:, :, 3], mods[:, :, 4]).reshape(B * T, D)
        if i % 2 == 0:
            gact = matmul_swiglu(h2, cast_layer(ev_ffn_w1, j), cast_layer(ev_ffn_w3, j))
            X = matmul_resgate(gact, cast_layer(ev_ffn_w2, j), X, mods[:, :, 5], n_lat=L)
        else:
            y0, y1 = moe_layer(h2, od_router_w[j], od_router_b[j], cast_layer(od_moe_w1, j),
                               cast_layer(od_moe_w3, j), cast_layer(od_moe_w2, j))
            X = moe_combine(X, y0.reshape(B, T, D), y1.reshape(B, T, D), mods[:, :, 5],
                            n_rows=L if i == DEPTH - 1 else T)
    return X
```

```python
import functools
import math

import jax
import jax.numpy as jnp
from jax import lax
from jax.experimental import pallas as pl
from jax.experimental.pallas import tpu as pltpu

D_MODEL = 4096
DEPTH = 4
GRID_W = 64
N_MOD = 6
EPS = 1e-6
NEG_INF = -1e30
HEAD_DIM = 128
ROPE_BASE = 10000.0
BLOCK = 128
WINDOW = 128
HY_CH = D_MODEL // 2
HY_ORDER = 2
HY_SHORT = 3
HY_EMB = 33
HY_BANDS = (HY_EMB - 1) // 2
HY_FILTER_HIDDEN_LAYERS = 2
HY_DIRS = 2
HY_FAST_DECAY = 0.3
HY_SLOW_DECAY = 1.5
HY_DECAY_TARGET = 1e-2
HY_WIDTH = (HY_ORDER + 1) * HY_CH
WIN_HEADS = (D_MODEL - HY_CH) // HEAD_DIM
WIN_KV_HEADS = WIN_HEADS // 4
WIN_GROUP = WIN_HEADS // WIN_KV_HEADS
WIN_Q = WIN_HEADS * HEAD_DIM
WIN_KV = WIN_KV_HEADS * HEAD_DIM
DIFF_DIM = 128
DIFF_HEADS = D_MODEL // (2 * DIFF_DIM)
DIFF_Q = DIFF_HEADS * 2 * DIFF_DIM
N_EXPERTS = 8
TOP_K = 2
assert DEPTH % 2 == 0, "the last layer is assumed to be an (odd) differential-attention / expert layer"

VMEM_LIMIT_BYTES = 56 * 1024 * 1024
ROW_TILE = 768
NORM_TILE = 256
MOE_TILE = 512

HI = lax.Precision.HIGHEST


def _cparams(sem):
    return pltpu.CompilerParams(dimension_semantics=sem, vmem_limit_bytes=VMEM_LIMIT_BYTES)


def _cast_kernel(w_ref, o_ref):
    o_ref[...] = w_ref[...].astype(o_ref.dtype)


def cast_layer(w, j, *, tr=1024):
    lead = w.shape[1:-2]
    R, W = w.shape[-2:]
    rows = math.prod(lead) * R
    w3 = w.reshape(w.shape[0], rows, W)
    tw = next(t for t in (2048, 1536, 1024, 512) if W % t == 0)
    out = pl.pallas_call(
        _cast_kernel,
        grid=(rows // tr, W // tw),
        in_specs=[pl.BlockSpec((None, tr, tw), lambda r, c: (j, r, c))],
        out_specs=pl.BlockSpec((tr, tw), lambda r, c: (r, c)),
        out_shape=jax.ShapeDtypeStruct((rows, W), jnp.bfloat16),
        compiler_params=_cparams(("parallel", "parallel")),
        name="cast_layer",
    )(w3)
    return out.reshape(lead + (R, W))


def _norm_mod_kernel(x_ref, g_ref, shift_ref, scale_ref, o_ref):
    x = x_ref[...]
    y = x * lax.rsqrt(jnp.mean(x * x, axis=-1, keepdims=True) + EPS)
    y = y * g_ref[...]
    o_ref[...] = (y * (1.0 + scale_ref[...]) + shift_ref[...]).astype(o_ref.dtype)


def norm_mod(x, g, shift, scale):
    B, T, D = x.shape
    nt = T // NORM_TILE
    mod_spec = pl.BlockSpec((None, None, 1, D), lambda b, t: (b, t // (nt - 1), 0, 0))
    return pl.pallas_call(
        _norm_mod_kernel,
        grid=(B, nt),
        in_specs=[
            pl.BlockSpec((None, NORM_TILE, D), lambda b, t: (b, t, 0)),
            pl.BlockSpec((1, D), lambda b, t: (0, 0)),
            mod_spec, mod_spec,
        ],
        out_specs=pl.BlockSpec((None, NORM_TILE, D), lambda b, t: (b, t, 0)),
        out_shape=jax.ShapeDtypeStruct((B, T, D), jnp.bfloat16),
        compiler_params=_cparams(("parallel", "parallel")),
        name="norm_mod",
    )(x, g.reshape(1, D), shift.reshape(B, 2, 1, D), scale.reshape(B, 2, 1, D))


def _mm_kernel(a_ref, b_ref, o_ref, acc_ref, *, nk):
    k = pl.program_id(2)
    part = jnp.dot(a_ref[...], b_ref[...], preferred_element_type=jnp.float32)
    if nk == 1:
        o_ref[...] = part.astype(o_ref.dtype)
        return

    @pl.when(k == 0)
    def _():
        acc_ref[...] = part

    @pl.when(k > 0)
    def _():
        acc_ref[...] += part

    @pl.when(k == nk - 1)
    def _():
        o_ref[...] = acc_ref[...].astype(o_ref.dtype)


def matmul(a, b, *, out_dtype=jnp.bfloat16, tm=ROW_TILE, tn=512, tk=None):
    M, K = a.shape
    _, N = b.shape
    tk = K if tk is None else tk
    nk = K // tk
    return pl.pallas_call(
        functools.partial(_mm_kernel, nk=nk),
        grid=(M // tm, N // tn, nk),
        in_specs=[pl.BlockSpec((tm, tk), lambda i, j, k: (i, k)),
                  pl.BlockSpec((tk, tn), lambda i, j, k: (k, j))],
        out_specs=pl.BlockSpec((tm, tn), lambda i, j, k: (i, j)),
        out_shape=jax.ShapeDtypeStruct((M, N), out_dtype),
        scratch_shapes=[pltpu.VMEM((tm, tn), jnp.float32)],
        compiler_params=_cparams(("parallel", "parallel", "arbitrary")),
        name="matmul",
    )(a, b)


def _mm_swiglu_kernel(a_ref, w1_ref, w3_ref, o_ref):
    a = a_ref[...]
    h1 = jnp.dot(a, w1_ref[...], preferred_element_type=jnp.float32)
    h3 = jnp.dot(a, w3_ref[...], preferred_element_type=jnp.float32)
    o_ref[...] = (h1 * jax.nn.sigmoid(h1) * h3).astype(o_ref.dtype)


def matmul_swiglu(a, w1, w3, *, tm=ROW_TILE, tn=512):
    M, K = a.shape
    _, N = w1.shape
    return pl.pallas_call(
        _mm_swiglu_kernel,
        grid=(M // tm, N // tn),
        in_specs=[pl.BlockSpec((tm, K), lambda i, j: (i, 0)),
                  pl.BlockSpec((K, tn), lambda i, j: (0, j)),
                  pl.BlockSpec((K, tn), lambda i, j: (0, j))],
        out_specs=pl.BlockSpec((tm, tn), lambda i, j: (i, j)),
        out_shape=jax.ShapeDtypeStruct((M, N), jnp.bfloat16),
        compiler_params=_cparams(("parallel", "parallel")),
        name="matmul_swiglu",
    )(a, w1, w3)


def _mm_resgate_kernel(a_ref, b_ref, res_ref, gate_ref, o_ref, acc_ref, *, nk, tm, tiles_per_batch, n_lat):
    i = pl.program_id(0)
    k = pl.program_id(2)
    part = jnp.dot(a_ref[...], b_ref[...], preferred_element_type=jnp.float32)

    def finish(acc):
        row = (i % tiles_per_batch) * tm + lax.broadcasted_iota(jnp.int32, (tm, 1), 0)
        gate = jnp.where(row < n_lat, gate_ref[0:1, :], gate_ref[1:2, :])
        o_ref[...] = res_ref[...] + gate * acc

    if nk == 1:
        finish(part)
        return

    @pl.when(k == 0)
    def _():
        acc_ref[...] = part

    @pl.when(k > 0)
    def _():
        acc_ref[...] += part

    @pl.when(k == nk - 1)
    def _():
        finish(acc_ref[...])


def matmul_resgate(a, b, res, gate, *, n_lat, tm=ROW_TILE, tn=512, tk=None):
    B, T, N = res.shape
    M, K = a.shape
    tk = K if tk is None else tk
    nk = K // tk
    tpb = T // tm
    kern = functools.partial(_mm_resgate_kernel, nk=nk, tm=tm, tiles_per_batch=tpb, n_lat=n_lat)
    out = pl.pallas_call(
        kern,
        grid=(M // tm, N // tn, nk),
        in_specs=[pl.BlockSpec((tm, tk), lambda i, j, k: (i, k)),
                  pl.BlockSpec((tk, tn), lambda i, j, k: (k, j)),
                  pl.BlockSpec((tm, tn), lambda i, j, k: (i, j)),
                  pl.BlockSpec((None, 2, tn), lambda i, j, k: (i // tpb, 0, j))],
        out_specs=pl.BlockSpec((tm, tn), lambda i, j, k: (i, j)),
        out_shape=jax.ShapeDtypeStruct((M, N), jnp.float32),
        scratch_shapes=[pltpu.VMEM((tm, tn), jnp.float32)],
        compiler_params=_cparams(("parallel", "parallel", "arbitrary")),
        name="matmul_resgate",
    )(a, b, res.reshape(M, N), gate)
    return out.reshape(B, T, N)


def _mm2_resgate_kernel(a1_ref, a2_ref, b_ref, res_ref, gate_ref, o_ref, *, tm, tiles_per_batch, n_lat):
    i = pl.program_id(0)
    k1 = a1_ref.shape[1]
    acc = (jnp.dot(a1_ref[...], b_ref[:k1, :], preferred_element_type=jnp.float32)
           + jnp.dot(a2_ref[...], b_ref[k1:, :], preferred_element_type=jnp.float32))
    row = (i % tiles_per_batch) * tm + lax.broadcasted_iota(jnp.int32, (tm, 1), 0)
    gate = jnp.where(row < n_lat, gate_ref[0:1, :], gate_ref[1:2, :])
    o_ref[...] = res_ref[...] + gate * acc


def matmul2_resgate(a1, a2, b, res, gate, *, n_lat, tm=ROW_TILE, tn=512):
    B, T, N = res.shape
    M, K1 = a1.shape
    K2 = a2.shape[1]
    tpb = T // tm
    out = pl.pallas_call(
        functools.partial(_mm2_resgate_kernel, tm=tm, tiles_per_batch=tpb, n_lat=n_lat),
        grid=(M // tm, N // tn),
        in_specs=[pl.BlockSpec((tm, K1), lambda i, j: (i, 0)),
                  pl.BlockSpec((tm, K2), lambda i, j: (i, 0)),
                  pl.BlockSpec((K1 + K2, tn), lambda i, j: (0, j)),
                  pl.BlockSpec((tm, tn), lambda i, j: (i, j)),
                  pl.BlockSpec((None, 2, tn), lambda i, j: (i // tpb, 0, j))],
        out_specs=pl.BlockSpec((tm, tn), lambda i, j: (i, j)),
        out_shape=jax.ShapeDtypeStruct((M, N), jnp.float32),
        compiler_params=_cparams(("parallel", "parallel")),
        name="matmul2_resgate",
    )(a1, a2, b, res.reshape(M, N), gate)
    return out.reshape(B, T, N)


def _moe_combine_kernel(x_ref, y0_ref, y1_ref, gate_ref, o_ref):
    y = y0_ref[...].astype(jnp.float32) + y1_ref[...].astype(jnp.float32)
    o_ref[...] = x_ref[...] + gate_ref[...] * y


def moe_combine(x, y0, y1, gate, *, n_rows):
    B, T, D = x.shape
    nt = T // NORM_TILE
    row = pl.BlockSpec((None, NORM_TILE, D), lambda b, t: (b, t, 0))
    return pl.pallas_call(
        _moe_combine_kernel,
        grid=(B, n_rows // NORM_TILE),
        in_specs=[row, row, row, pl.BlockSpec((None, None, 1, D), lambda b, t: (b, t // (nt - 1), 0, 0))],
        out_specs=row,
        out_shape=jax.ShapeDtypeStruct((B, n_rows, D), jnp.float32),
        compiler_params=_cparams(("parallel", "parallel")),
        name="moe_combine",
    )(x, y0, y1, gate.reshape(B, 2, 1, D))


def _window_attn_kernel(sink_ref, q_ref, kc_ref, kp_ref, ko_ref, kn_ref, vc_ref, vp_ref, vo_ref, vn_ref, o_ref,
                        *, n_lat, n_ctx):
    n = pl.program_id(1)
    h = pl.program_id(2)
    G = WIN_GROUP
    q = q_ref[...]
    qs = jnp.concatenate([q[:, g * HEAD_DIM:(g + 1) * HEAD_DIM] for g in range(G)], axis=0)
    k = jnp.concatenate([kc_ref[...], kp_ref[...], ko_ref[...], kn_ref[...]], axis=0)
    v = jnp.concatenate([vc_ref[...], vp_ref[...], vo_ref[...], vn_ref[...]], axis=0)
    s = lax.dot_general(qs, k, (((1,), (1,)), ((), ())), preferred_element_type=jnp.float32)
    nk = n_ctx + 3 * BLOCK
    col = lax.broadcasted_iota(jnp.int32, (G * BLOCK, nk), 1)
    qpos = n * BLOCK + lax.broadcasted_iota(jnp.int32, (G * BLOCK, nk), 0) % BLOCK
    kpos = (n - 1) * BLOCK + (col - n_ctx)
    valid = (col < n_ctx) | ((jnp.abs(qpos - kpos) <= WINDOW) & (kpos >= 0) & (kpos < n_lat) & (qpos < n_lat))
    s = jnp.where(valid, s, NEG_INF)
    sink = jnp.concatenate(
        [jnp.full((BLOCK, 1), sink_ref[h * G + g], jnp.float32) for g in range(G)], axis=0)
    m = jnp.maximum(jnp.max(s, axis=-1, keepdims=True), sink)
    e = jnp.exp(s - m)
    denom = jnp.sum(e, axis=-1, keepdims=True) + jnp.exp(sink - m)
    p = (e / denom).astype(v.dtype)
    o = jnp.dot(p, v, preferred_element_type=jnp.float32)
    for g in range(G):
        o_ref[:, g * HEAD_DIM:(g + 1) * HEAD_DIM] = o[g * BLOCK:(g + 1) * BLOCK, :].astype(o_ref.dtype)


def window_attention(q, k, v, sink, *, n_ctx, v_col0=0):
    B, T, _ = q.shape
    L = T - n_ctx
    nb = L // BLOCK
    kv_blk = lambda f: pl.BlockSpec((None, BLOCK, HEAD_DIM), f)
    ctx_blk = lambda off: pl.BlockSpec((None, n_ctx, HEAD_DIM), lambda b, n, h: (b, L // n_ctx, off + h))
    prev_blk = lambda off: kv_blk(lambda b, n, h: (b, jnp.maximum(n - 1, 0), off + h))
    own_blk = lambda off: kv_blk(lambda b, n, h: (b, n, off + h))
    next_blk = lambda off: kv_blk(lambda b, n, h: (b, jnp.minimum(n + 1, nb - 1), off + h))
    voff = v_col0 // HEAD_DIM
    return pl.pallas_call(
        functools.partial(_window_attn_kernel, n_lat=L, n_ctx=n_ctx),
        grid=(B, T // BLOCK, WIN_KV_HEADS),
        in_specs=[pl.BlockSpec(memory_space=pltpu.SMEM),
                  pl.BlockSpec((None, BLOCK, WIN_GROUP * HEAD_DIM), lambda b, n, h: (b, n, h)),
                  ctx_blk(0), prev_blk(0), own_blk(0), next_blk(0),
                  ctx_blk(voff), prev_blk(voff), own_blk(voff), next_blk(voff)],
        out_specs=pl.BlockSpec((None, BLOCK, WIN_GROUP * HEAD_DIM), lambda b, n, h: (b, n, h)),
        out_shape=jax.ShapeDtypeStruct((B, T, WIN_Q), jnp.bfloat16),
        compiler_params=_cparams(("parallel", "parallel", "parallel")),
        name="window_attention",
    )(sink.astype(jnp.float32), q, k, k, k, k, v, v, v, v)


def _diff_attn_kernel(lam_ref, q_ref, k_ref, v_ref, g_ref, buf_ref, o_ref, m_ref, l_ref, acc_ref,
                      *, n_keys, tk, out_scale):
    del buf_ref
    m_ref[...] = jnp.full(m_ref.shape, NEG_INF, jnp.float32)
    l_ref[...] = jnp.zeros(l_ref.shape, jnp.float32)
    acc_ref[...] = jnp.zeros(acc_ref.shape, jnp.float32)

    def step(j, carry):
        ks = pl.multiple_of(j * tk, tk)
        v = v_ref[pl.ds(ks, tk), :]
        scores = []
        for sub in range(2):
            q = q_ref[:, sub * DIFF_DIM:(sub + 1) * DIFF_DIM]
            k = k_ref[pl.ds(ks, tk), sub * DIFF_DIM:(sub + 1) * DIFF_DIM]
            scores.append(lax.dot_general(q, k, (((1,), (1,)), ((), ())), preferred_element_type=jnp.float32))
        for sub in range(2):
            s = scores[sub]
            m_old = m_ref[sub]
            m_new = jnp.maximum(m_old, jnp.max(s, axis=-1, keepdims=True))
            alpha = jnp.exp2(m_old - m_new)
            p = jnp.exp2(s - m_new)
            l_ref[sub] = alpha * l_ref[sub] + jnp.sum(p, axis=-1, keepdims=True)
            acc_ref[sub] = alpha * acc_ref[sub] + jnp.dot(p.astype(v.dtype), v, preferred_element_type=jnp.float32)
            m_ref[sub] = m_new
        return carry

    lax.fori_loop(0, n_keys // tk, step, 0)
    lam = lam_ref[0]
    o = acc_ref[0] / l_ref[0] - lam * (acc_ref[1] / l_ref[1])
    o = o * lax.rsqrt(jnp.mean(o * o, axis=-1, keepdims=True) + EPS)
    o_ref[...] = (o * g_ref[...] * out_scale).astype(o_ref.dtype)


def diff_attention(q, k, v, lam, subln_g, out_buf, *, n_q, q_blk0, n_keys, key_blk, tq, tk, out_scale, v_col0=0):
    B = q.shape[0]
    W = 2 * DIFF_DIM
    voff = v_col0 // W
    return pl.pallas_call(
        functools.partial(_diff_attn_kernel, n_keys=n_keys, tk=tk, out_scale=out_scale),
        grid=(B, DIFF_HEADS, n_q // tq),
        in_specs=[pl.BlockSpec(memory_space=pltpu.SMEM),
                  pl.BlockSpec((None, tq, W), lambda b, h, i: (b, q_blk0 + i, h)),
                  pl.BlockSpec((None, n_keys, W), lambda b, h, i: (b, key_blk, h)),
                  pl.BlockSpec((None, n_keys, W), lambda b, h, i: (b, key_blk, voff + h)),
                  pl.BlockSpec((1, W), lambda b, h, i: (0, 0)),
                  pl.BlockSpec(memory_space=pl.ANY)],
        out_specs=pl.BlockSpec((None, tq, W), lambda b, h, i: (b, q_blk0 + i, h)),
        scratch_shapes=[pltpu.VMEM((2, tq, 1), jnp.float32),
                        pltpu.VMEM((2, tq, 1), jnp.float32),
                        pltpu.VMEM((2, tq, W), jnp.float32)],
        out_shape=jax.ShapeDtypeStruct(out_buf.shape, out_buf.dtype),
        input_output_aliases={5: 0},
        compiler_params=_cparams(("parallel", "parallel", "arbitrary")),
        name="diff_attention",
    )(lam.reshape(1).astype(jnp.float32), q, k, v, subln_g.reshape(1, W).astype(jnp.float32), out_buf)


def _moe_up_kernel(te_ref, tv_ref, x_ref, w1_ref, w3_ref, o_ref):
    i = pl.program_id(1)

    @pl.when(tv_ref[i] > 0)
    def _():
        x = x_ref[...]
        h1 = jnp.dot(x, w1_ref[...], preferred_element_type=jnp.float32)
        h3 = jnp.dot(x, w3_ref[...], preferred_element_type=jnp.float32)
        o_ref[...] = (h1 * jax.nn.sigmoid(h1) * h3).astype(o_ref.dtype)

    @pl.when(tv_ref[i] == 0)
    def _():
        o_ref[...] = jnp.zeros(o_ref.shape, o_ref.dtype)


def _moe_down_kernel(te_ref, tv_ref, g_ref, w2_ref, rg_ref, o_ref):
    i = pl.program_id(1)

    @pl.when(tv_ref[i] > 0)
    def _():
        y = jnp.dot(g_ref[...], w2_ref[...], preferred_element_type=jnp.float32)
        o_ref[...] = (rg_ref[...] * y).astype(o_ref.dtype)

    @pl.when(tv_ref[i] == 0)
    def _():
        o_ref[...] = jnp.zeros(o_ref.shape, o_ref.dtype)


def moe_experts(xs, w1, w3, w2, tile_expert, tile_valid, row_gate, *, tn_up=768, tn_down=1024):
    P, D = xs.shape
    F = w1.shape[2]
    tm = MOE_TILE
    up = pl.pallas_call(
        _moe_up_kernel,
        grid_spec=pltpu.PrefetchScalarGridSpec(
            num_scalar_prefetch=2,
            grid=(F // tn_up, P // tm),
            in_specs=[pl.BlockSpec((tm, D), lambda j, i, te, tv: (i, 0)),
                      pl.BlockSpec((None, D, tn_up), lambda j, i, te, tv: (te[i], 0, j)),
                      pl.BlockSpec((None, D, tn_up), lambda j, i, te, tv: (te[i], 0, j))],
            out_specs=pl.BlockSpec((tm, tn_up), lambda j, i, te, tv: (i, j)),
        ),
        out_shape=jax.ShapeDtypeStruct((P, F), jnp.bfloat16),
        compiler_params=_cparams(("parallel", "arbitrary")),
        name="moe_up",
    )(tile_expert, tile_valid, xs, w1, w3)
    return pl.pallas_call(
        _moe_down_kernel,
        grid_spec=pltpu.PrefetchScalarGridSpec(
            num_scalar_prefetch=2,
            grid=(D // tn_down, P // tm),
            in_specs=[pl.BlockSpec((tm, F), lambda j, i, te, tv: (i, 0)),
                      pl.BlockSpec((None, F, tn_down), lambda j, i, te, tv: (te[i], 0, j)),
                      pl.BlockSpec((tm, 1), lambda j, i, te, tv: (i, 0))],
            out_specs=pl.BlockSpec((tm, tn_down), lambda j, i, te, tv: (i, j)),
        ),
        out_shape=jax.ShapeDtypeStruct((P, D), jnp.bfloat16),
        compiler_params=_cparams(("parallel", "arbitrary")),
        name="moe_down",
    )(tile_expert, tile_valid, up, w2, row_gate)


def moe_layer(h2, router_w, router_b, w1, w3, w2):
    N, D = h2.shape
    tm = MOE_TILE
    logits = jnp.dot(h2.astype(jnp.float32), router_w, precision=HI) + router_b
    top_v, top_i = lax.top_k(logits, TOP_K)
    gates = jax.nn.softmax(top_v, axis=-1)
    A = N * TOP_K
    e_flat = top_i.reshape(A).astype(jnp.int32)
    order = jnp.argsort(e_flat, stable=True)
    e_sorted = e_flat[order]
    counts = jnp.sum(e_flat[:, None] == jnp.arange(N_EXPERTS, dtype=jnp.int32)[None, :], axis=0).astype(jnp.int32)
    padded = ((counts + tm - 1) // tm) * tm
    start_unpadded = jnp.cumsum(counts) - counts
    start_padded = jnp.cumsum(padded) - padded
    dest_sorted = start_padded[e_sorted] + (jnp.arange(A, dtype=jnp.int32) - start_unpadded[e_sorted])
    P = A + N_EXPERTS * tm
    tile_start = jnp.arange(P // tm, dtype=jnp.int32) * tm
    ends = jnp.cumsum(padded)
    tile_expert = jnp.minimum(jnp.sum(tile_start[:, None] >= ends[None, :], axis=1), N_EXPERTS - 1).astype(jnp.int32)
    tile_valid = (tile_start < ends[-1]).astype(jnp.int32)
    rows = jnp.arange(P, dtype=jnp.int32)
    row_e = tile_expert[rows // tm]
    row_idx = rows - start_padded[row_e]
    row_ok = (row_idx < counts[row_e]) & (rows < ends[-1])
    src = order[jnp.clip(start_unpadded[row_e] + row_idx, 0, A - 1)]
    row_token = jnp.where(row_ok, src // TOP_K, 0).astype(jnp.int32)
    row_gate = jnp.where(row_ok, gates.reshape(A)[src], 0.0)
    pos = dest_sorted[jnp.argsort(order)].reshape(N, TOP_K)
    take = lambda a, idx: a.at[idx].get(mode="promise_in_bounds")
    xs = take(h2, row_token)
    y = moe_experts(xs, w1, w3, w2, tile_expert, tile_valid, row_gate.reshape(P, 1))
    return take(y, pos[:, 0]), take(y, pos[:, 1])


FFT_R = 128
FFT_K1 = 72
Z_PITCH = FFT_R + 8
S1_PITCH = 2 * FFT_K1 + 8
S2_PITCH = 2 * FFT_R + 8
HY_LANES = 128
FFT_UNROLL = 32


def _dft_tables():
    R, K1 = FFT_R, FFT_K1
    N = R * R
    i32 = jnp.int32
    b = jnp.arange(R, dtype=i32)[:, None, None]
    k1 = jnp.arange(K1, dtype=i32)[None, :, None]
    a = jnp.arange(R, dtype=i32)[None, None, :]
    th = (2.0 * math.pi / N) * ((k1 * (R * a + b)) % N).astype(jnp.float32)
    f1 = jnp.concatenate([jnp.cos(th), -jnp.sin(th)], axis=1)
    w = jnp.where(jnp.arange(K1) > R // 2, 0.0, jnp.where((jnp.arange(K1) % (R // 2)) == 0, 1.0, 2.0)) / N
    the = jnp.swapaxes(th[:, :, :R // 2], 1, 2)
    e = jnp.concatenate([w * jnp.cos(the), -w * jnp.sin(the)], axis=2)
    k2 = jnp.arange(R, dtype=i32)
    ph = (2.0 * math.pi / R) * ((k2[:, None] * k2[None, :]) % R).astype(jnp.float32)
    c, s = jnp.cos(ph), jnp.sin(ph)
    g = jnp.block([[c, s], [-s, c]])
    ginv = jnp.block([[c, -s], [s, c]])
    bf = jnp.bfloat16
    return f1.astype(bf), e.astype(bf), g.astype(bf), ginv.astype(bf)


def _fft_stage1(src_ref, f1_ref, s1_ref, n_a):
    def body(b, carry):
        zb = src_ref[pl.ds(b, n_a, stride=Z_PITCH), :]
        s1_ref[pl.ds(pl.multiple_of(b * S1_PITCH, 8), 2 * FFT_K1), :] = jnp.dot(
            f1_ref[b], zb.astype(jnp.bfloat16), preferred_element_type=jnp.float32)
        return carry
    lax.fori_loop(0, FFT_R, body, 0, unroll=FFT_UNROLL)


def _fft_stage2(s1_ref, g_ref, k1):
    are = s1_ref[pl.ds(k1, FFT_R, stride=S1_PITCH), :]
    aim = s1_ref[pl.ds(FFT_K1 + k1, FFT_R, stride=S1_PITCH), :]
    r = jnp.concatenate([are, aim], axis=0).astype(jnp.bfloat16)
    return jnp.dot(g_ref[...], r, preferred_element_type=jnp.float32)


def _filter_fft_kernel(f_ref, f1_ref, g_ref, h_ref, s1_ref):
    _fft_stage1(f_ref, f1_ref, s1_ref, FFT_R)

    def body(k1, carry):
        h_ref[k1] = _fft_stage2(s1_ref, g_ref, k1).astype(h_ref.dtype)
        return carry
    lax.fori_loop(0, FFT_K1, body, 0, unroll=FFT_UNROLL)


def filter_spectrum(filt_padded, f1, g):
    rows, width = filt_padded.shape
    nt = width // HY_LANES
    once = pl.Buffered(1)
    return pl.pallas_call(
        _filter_fft_kernel,
        grid=(nt,),
        in_specs=[pl.BlockSpec((rows, HY_LANES), lambda c: (0, c)),
                  pl.BlockSpec(f1.shape, lambda c: (0, 0, 0), pipeline_mode=once),
                  pl.BlockSpec(g.shape, lambda c: (0, 0), pipeline_mode=once)],
        out_specs=pl.BlockSpec((None, FFT_K1, 2 * FFT_R, HY_LANES), lambda c: (c, 0, 0, 0)),
        out_shape=jax.ShapeDtypeStruct((nt, FFT_K1, 2 * FFT_R, HY_LANES), jnp.bfloat16),
        scratch_shapes=[pltpu.VMEM((FFT_R * S1_PITCH, HY_LANES), jnp.float32)],
        compiler_params=_cparams(("arbitrary",)),
        name="hyena_filter_fft",
    )(filt_padded, f1, g)


def _short_conv_chunks(x_ref, w_ref, b_ref, dst_ref, n_chunks, pitch=Z_PITCH):
    R = FFT_R
    w = w_ref[...]
    bias = b_ref[...]
    row = lax.broadcasted_iota(jnp.int32, (R, HY_LANES), 0)
    for a in range(n_chunks):
        xc = x_ref[a * R:(a + 1) * R, :].astype(jnp.float32)
        prev = pltpu.roll(xc, 1, axis=0)
        nxt = pltpu.roll(xc, R - 1, axis=0)
        if a > 0:
            last = x_ref[a * R - 16:a * R, :].astype(jnp.float32)[15:16, :]
        else:
            last = jnp.zeros((1, HY_LANES), jnp.float32)
        if a < n_chunks - 1:
            first = x_ref[(a + 1) * R:(a + 1) * R + 16, :].astype(jnp.float32)[0:1, :]
        else:
            first = jnp.zeros((1, HY_LANES), jnp.float32)
        prev = jnp.where(row == 0, last, prev)
        nxt = jnp.where(row == R - 1, first, nxt)
        dst_ref[a * pitch:a * pitch + R, :] = bias + prev * w[0:1, :] + xc * w[1:2, :] + nxt * w[2:3, :]


def _hyena_conv_kernel(v_ref, x_ref, wv_ref, bv_ref, wx_ref, bx_ref, skip_ref, h_ref, f1_ref, e_ref, g_ref, gi_ref,
                       buf_ref, o_ref, z_ref, gate_ref, s1_ref, s2_ref, *, n_chunks):
    del buf_ref
    n = pl.program_id(2)
    R, K1 = FFT_R, FFT_K1

    @pl.when(n == 0)
    def _():
        _short_conv_chunks(v_ref, wv_ref, bv_ref, z_ref, n_chunks)

    _short_conv_chunks(x_ref, wx_ref, bx_ref, gate_ref, n_chunks)

    _fft_stage1(z_ref, f1_ref, s1_ref, n_chunks)

    def mid(k1, carry):
        x = _fft_stage2(s1_ref, g_ref, k1)
        h = h_ref[k1].astype(jnp.float32)
        xr, xi, hr, hi = x[:R], x[R:], h[:R], h[R:]
        p = jnp.concatenate([xr * hr - xi * hi, xr * hi + xi * hr], axis=0).astype(jnp.bfloat16)
        s2_ref[pl.ds(pl.multiple_of(k1 * S2_PITCH, 8), 2 * R), :] = jnp.dot(
            gi_ref[...], p, preferred_element_type=jnp.float32)
        return carry
    lax.fori_loop(0, K1, mid, 0, unroll=FFT_UNROLL)

    skip = skip_ref[...]

    def last(b, carry):
        qre = s2_ref[pl.ds(b, K1, stride=S2_PITCH), :]
        qim = s2_ref[pl.ds(R + b, K1, stride=S2_PITCH), :]
        r = jnp.concatenate([qre, qim], axis=0).astype(jnp.bfloat16)
        conv = jnp.dot(e_ref[b], r, preferred_element_type=jnp.float32)
        zold = z_ref[pl.ds(b, n_chunks, stride=Z_PITCH), :]
        gate = gate_ref[pl.ds(b, n_chunks, stride=Z_PITCH), :]
        z_ref[pl.ds(b, n_chunks, stride=Z_PITCH), :] = gate * (conv + zold * skip)
        return carry
    lax.fori_loop(0, R, last, 0, unroll=FFT_UNROLL)

    @pl.when(n == HY_ORDER - 1)
    def _():
        for a in range(n_chunks):
            o_ref[a * R:(a + 1) * R, :] = z_ref[a * Z_PITCH:a * Z_PITCH + R, :].astype(o_ref.dtype)


def hyena_conv(proj, conv_w, conv_b, skip, spec, tables, out_buf, *, n_lat):
    f1, e, g, ginv = tables
    B = proj.shape[0]
    n_chunks = n_lat // FFT_R
    nt = HY_CH // HY_LANES
    once = pl.Buffered(1)
    grp = lambda n: (1 + n) * nt
    f1h = f1[:, :, :n_chunks]
    return pl.pallas_call(
        functools.partial(_hyena_conv_kernel, n_chunks=n_chunks),
        grid=(B, nt, HY_ORDER),
        in_specs=[pl.BlockSpec((None, n_lat, HY_LANES), lambda b, c, n: (b, 0, c)),
                  pl.BlockSpec((None, n_lat, HY_LANES), lambda b, c, n: (b, 0, grp(n) + c)),
                  pl.BlockSpec((HY_SHORT, HY_LANES), lambda b, c, n: (0, c)),
                  pl.BlockSpec((1, HY_LANES), lambda b, c, n: (0, c)),
                  pl.BlockSpec((HY_SHORT, HY_LANES), lambda b, c, n: (0, grp(n) + c)),
                  pl.BlockSpec((1, HY_LANES), lambda b, c, n: (0, grp(n) + c)),
                  pl.BlockSpec((None, 1, HY_LANES), lambda b, c, n: (n, 0, c)),
                  pl.BlockSpec((None, FFT_K1, 2 * FFT_R, HY_LANES), lambda b, c, n: (n * nt + c, 0, 0, 0),
                               pipeline_mode=once),
                  pl.BlockSpec(f1h.shape, lambda b, c, n: (0, 0, 0), pipeline_mode=once),
                  pl.BlockSpec(e.shape, lambda b, c, n: (0, 0, 0), pipeline_mode=once),
                  pl.BlockSpec(g.shape, lambda b, c, n: (0, 0), pipeline_mode=once),
                  pl.BlockSpec(ginv.shape, lambda b, c, n: (0, 0), pipeline_mode=once),
                  pl.BlockSpec(memory_space=pl.ANY)],
        out_specs=pl.BlockSpec((None, n_lat, HY_LANES), lambda b, c, n: (b, 0, c)),
        out_shape=jax.ShapeDtypeStruct(out_buf.shape, out_buf.dtype),
        input_output_aliases={12: 0},
        scratch_shapes=[pltpu.VMEM((n_chunks * Z_PITCH, HY_LANES), jnp.float32),
                        pltpu.VMEM((n_chunks * Z_PITCH, HY_LANES), jnp.float32),
                        pltpu.VMEM((FFT_R * S1_PITCH, HY_LANES), jnp.float32),
                        pltpu.VMEM((FFT_K1 * S2_PITCH, HY_LANES), jnp.float32)],
        compiler_params=_cparams(("parallel", "parallel", "arbitrary")),
        name="hyena_conv",
    )(proj, proj, conv_w, conv_b.reshape(1, -1), conv_w, conv_b.reshape(1, -1),
      skip.reshape(HY_ORDER, 1, HY_CH), spec, f1h, e, g, ginv, out_buf)


def _ctx_dft_tables(n_ctx):
    N = 2 * n_ctx
    nk = -(-(n_ctx + 1) // 16) * 16
    k = jnp.arange(nk, dtype=jnp.int32)[:, None]
    t = jnp.arange(N, dtype=jnp.int32)[None, :]
    th = (2.0 * math.pi / N) * ((k * t) % N).astype(jnp.float32)
    fwd = jnp.concatenate([jnp.cos(th), -jnp.sin(th)], axis=0)
    w = jnp.where(k > n_ctx, 0.0, jnp.where((k % n_ctx) == 0, 1.0, 2.0)) / N
    inv = jnp.concatenate([(w * jnp.cos(th[:, :n_ctx])).T, (-w * jnp.sin(th[:, :n_ctx])).T], axis=1)
    return fwd.astype(jnp.bfloat16), inv.astype(jnp.bfloat16)


def _hyena_ctx_kernel(v_ref, x1_ref, x2_ref, wv_ref, bv_ref, w1_ref, b1_ref, w2_ref, b2_ref, skip_ref,
                      f0_ref, f1_ref, fwd_ref, inv_ref, buf_ref, o_ref, z_ref, g1_ref, g2_ref, *, n_ctx):
    del buf_ref
    nc = n_ctx // FFT_R
    _short_conv_chunks(v_ref, wv_ref, bv_ref, z_ref, nc, pitch=FFT_R)
    _short_conv_chunks(x1_ref, w1_ref, b1_ref, g1_ref, nc, pitch=FFT_R)
    _short_conv_chunks(x2_ref, w2_ref, b2_ref, g2_ref, nc, pitch=FFT_R)
    nk = fwd_ref.shape[0] // 2
    z = z_ref[...]
    for n, (filt_ref, gate_ref) in enumerate(((f0_ref, g1_ref), (f1_ref, g2_ref))):
        h = jnp.dot(fwd_ref[...], filt_ref[...].astype(jnp.bfloat16), preferred_element_type=jnp.float32)
        x = jnp.dot(fwd_ref[:, :n_ctx], z.astype(jnp.bfloat16), preferred_element_type=jnp.float32)
        xr, xi, hr, hi = x[:nk], x[nk:], h[:nk], h[nk:]
        p = jnp.concatenate([xr * hr - xi * hi, xr * hi + xi * hr], axis=0).astype(jnp.bfloat16)
        conv = jnp.dot(inv_ref[...], p, preferred_element_type=jnp.float32)
        z = gate_ref[...] * (conv + z * skip_ref[n:n + 1, :])
    o_ref[...] = z.astype(o_ref.dtype)


def hyena_ctx(proj, conv_w, conv_b, skip, filt, tables, out_buf, *, n_lat, n_ctx):
    fwd, inv = tables
    B = proj.shape[0]
    nt = HY_CH // HY_LANES
    rb = n_lat // n_ctx
    cb2 = conv_b.reshape(1, -1)
    row = lambda g: pl.BlockSpec((None, n_ctx, HY_LANES), lambda b, c: (b, rb, g * nt + c))
    wsp = lambda g: pl.BlockSpec((HY_SHORT, HY_LANES), lambda b, c: (0, g * nt + c))
    bsp = lambda g: pl.BlockSpec((1, HY_LANES), lambda b, c: (0, g * nt + c))
    fsp = lambda n: pl.BlockSpec((2 * n_ctx, HY_LANES), lambda b, c: (0, n * nt + c))
    scr = pltpu.VMEM((n_ctx, HY_LANES), jnp.float32)
    return pl.pallas_call(
        functools.partial(_hyena_ctx_kernel, n_ctx=n_ctx),
        grid=(B, nt),
        in_specs=[row(0), row(1), row(2), wsp(0), bsp(0), wsp(1), bsp(1), wsp(2), bsp(2),
                  pl.BlockSpec((HY_ORDER, HY_LANES), lambda b, c: (0, c)),
                  fsp(0), fsp(1),
                  pl.BlockSpec(fwd.shape, lambda b, c: (0, 0)),
                  pl.BlockSpec(inv.shape, lambda b, c: (0, 0)),
                  pl.BlockSpec(memory_space=pl.ANY)],
        out_specs=pl.BlockSpec((None, n_ctx, HY_LANES), lambda b, c: (b, rb, c)),
        out_shape=jax.ShapeDtypeStruct(out_buf.shape, out_buf.dtype),
        input_output_aliases={14: 0},
        scratch_shapes=[scr, scr, scr],
        compiler_params=_cparams(("parallel", "parallel")),
        name="hyena_ctx",
    )(proj, proj, proj, conv_w, cb2, conv_w, cb2, conv_w, cb2, skip, filt, filt, fwd, inv, out_buf)


def _norm_rope_kernel(x_ref, g_ref, cos_ref, sin_ref, o_ref, *, scale, nh):
    cos = cos_ref[...]
    sin = sin_ref[...]
    g = g_ref[...]
    lane = lax.broadcasted_iota(jnp.int32, cos.shape, 1)
    lower = (lane % (HEAD_DIM // 2)) < (HEAD_DIM // 4)
    for h in range(nh):
        x = x_ref[:, h * HEAD_DIM:(h + 1) * HEAD_DIM].astype(jnp.float32)
        y = x * lax.rsqrt(jnp.mean(x * x, axis=-1, keepdims=True) + EPS) * g
        rot = jnp.where(lower, -pltpu.roll(y, HEAD_DIM - HEAD_DIM // 4, axis=1), pltpu.roll(y, HEAD_DIM // 4, axis=1))
        o_ref[:, h * HEAD_DIM:(h + 1) * HEAD_DIM] = ((y * cos + rot * sin) * scale).astype(o_ref.dtype)


def head_norm_rope(t, col0, width, g, cos, sin, scale, *, tr=ROW_TILE, nh=4):
    B, T, _ = t.shape
    wb = nh * HEAD_DIM
    c0 = col0 // wb
    return pl.pallas_call(
        functools.partial(_norm_rope_kernel, scale=scale, nh=nh),
        grid=(B, T // tr, width // wb),
        in_specs=[pl.BlockSpec((None, tr, wb), lambda b, r, j: (b, r, c0 + j)),
                  pl.BlockSpec((1, HEAD_DIM), lambda b, r, j: (0, 0)),
                  pl.BlockSpec((tr, HEAD_DIM), lambda b, r, j: (r, 0)),
                  pl.BlockSpec((tr, HEAD_DIM), lambda b, r, j: (r, 0))],
        out_specs=pl.BlockSpec((None, tr, wb), lambda b, r, j: (b, r, j)),
        out_shape=jax.ShapeDtypeStruct((B, T, width), jnp.bfloat16),
        compiler_params=_cparams(("parallel", "parallel", "parallel")),
        name="head_norm_rope",
    )(t, g.reshape(1, HEAD_DIM).astype(jnp.float32), cos, sin)


def _rope_tables(L, n_ctx):
    rows = jnp.repeat(jnp.arange(L // GRID_W), GRID_W)
    cols = jnp.tile(jnp.arange(GRID_W), L // GRID_W)
    quarter = HEAD_DIM // 4
    inv = ROPE_BASE ** (-jnp.arange(quarter, dtype=jnp.float32) / quarter)
    ar = rows.astype(jnp.float32)[:, None] * inv
    ac = cols.astype(jnp.float32)[:, None] * inv
    ang = jnp.concatenate([ar, ar, ac, ac], axis=-1)
    cos = jnp.concatenate([jnp.cos(ang), jnp.ones((n_ctx, HEAD_DIM), jnp.float32)], axis=0)
    sin = jnp.concatenate([jnp.sin(ang), jnp.zeros((n_ctx, HEAD_DIM), jnp.float32)], axis=0)
    return cos, sin


def _implicit_filters(L, w_in, w_hid, b, freq, w_out, pitch=FFT_R):
    f32 = jnp.float32
    t = jnp.linspace(0.0, 1.0, L, dtype=f32)[:, None]
    w = (2.0 * math.pi / L) * jnp.arange(L, dtype=f32)[:, None]
    f = jnp.linspace(1e-4, HY_BANDS - 1, HY_BANDS, dtype=f32)[None, :]
    z = jnp.concatenate([t, jnp.cos(f * w), -jnp.sin(f * w)], axis=-1)
    h = jnp.sin(freq[0] * (jnp.dot(z, w_in, precision=HI) + b[0]))
    for n in range(HY_FILTER_HIDDEN_LAYERS):
        h = jnp.sin(freq[n + 1] * (jnp.dot(h, w_hid[n], precision=HI) + b[n + 1]))
    width = HY_ORDER * HY_CH
    max_decay = math.log(HY_DECAY_TARGET) / HY_FAST_DECAY
    min_decay = math.log(HY_DECAY_TARGET) / HY_SLOW_DECAY
    deltas = jnp.abs(jnp.linspace(min_decay, max_decay, width, dtype=f32))[None, :]
    hb = jnp.concatenate([jnp.zeros((1, h.shape[1]), f32), h[:0:-1]], axis=0)
    tb = jnp.concatenate([jnp.zeros((1, 1), f32), t[:0:-1]], axis=0)
    chunk = lambda a: jnp.pad(a.reshape(2, L // FFT_R, FFT_R, a.shape[-1]),
                              ((0, 0), (0, 0), (0, pitch - FFT_R), (0, 0))).reshape(2, -1, a.shape[-1])
    hh = chunk(jnp.stack([h, hb]))
    tt = chunk(jnp.stack([t, tb]))
    w2 = jnp.stack([w_out[:, :width], w_out[:, width:]])
    filt = jnp.einsum('hrk,hkw->hrw', hh, w2, precision=HI) * jnp.exp(-tt * deltas)
    return filt.reshape(-1, width)


def _ada_modulation(cv, down, up, b):
    m = jnp.dot(jnp.dot(jax.nn.silu(cv), down, precision=HI), up, precision=HI) + b
    return m.reshape(m.shape[:-1] + (N_MOD, m.shape[-1] // N_MOD))


def kernel(x, c, ctx, c_ctx, norm_g, ada_down, ada_up, ada_b, ev_w_in, ev_conv_w, ev_conv_b, ev_filt_w_in, ev_filt_w_hid, ev_filt_b, ev_filt_freq, ev_filt_w_out, ev_hy_skip, ev_qk_g, ev_sink, ev_w_out, ev_ffn_w1, ev_ffn_w3, ev_ffn_w2, od_w_qkv, od_qk_g, od_lambda, od_subln_g, od_w_out, od_router_w, od_router_b, od_moe_w1, od_moe_w3, od_moe_w2):
    B, L, D = x.shape
    Lc = ctx.shape[1]
    T = L + Lc
    bf16 = jnp.bfloat16
    cos, sin = _rope_tables(L, Lc)
    tables = _dft_tables()
    ctx_tables = _ctx_dft_tables(Lc)
    X = jnp.concatenate([x, ctx], axis=1)
    qk_scale = HEAD_DIM ** -0.5

    for i in range(DEPTH):
        j = i // 2
        m_l = _ada_modulation(c, ada_down[i], ada_up[i], ada_b[i])
        m_c = _ada_modulation(c_ctx, ada_down[i], ada_up[i], ada_b[i])
        mods = jnp.stack([m_l, jnp.broadcast_to(m_c[None], (B, N_MOD, D))], axis=1)

        h = norm_mod(X, norm_g[i, 0], mods[:, :, 0], mods[:, :, 1])
        hf = h.reshape(B * T, D)
        if i % 2 == 0:
            proj = matmul(hf, cast_layer(ev_w_in, j), tn=1024).reshape(B, T, -1)
            v_col0 = HY_WIDTH + WIN_Q + WIN_KV
            q = head_norm_rope(proj, HY_WIDTH, WIN_Q, ev_qk_g[j, 0], cos, sin, qk_scale)
            k = head_norm_rope(proj, HY_WIDTH + WIN_Q, WIN_KV, ev_qk_g[j, 1], cos, sin, 1.0)
            filt_args = (ev_filt_w_in[j], ev_filt_w_hid[j], ev_filt_b[j], ev_filt_freq[j], ev_filt_w_out[j])
            spec = filter_spectrum(_implicit_filters(L, *filt_args, pitch=Z_PITCH), tables[0], tables[2])
            hy = jnp.zeros((B, T, HY_CH), bf16)
            hy = hyena_conv(proj, ev_conv_w[j], ev_conv_b[j], ev_hy_skip[j], spec, tables, hy, n_lat=L)
            hy = hyena_ctx(proj, ev_conv_w[j], ev_conv_b[j], ev_hy_skip[j], _implicit_filters(Lc, *filt_args),
                           ctx_tables, hy, n_lat=L, n_ctx=Lc)
            att = window_attention(q, k, proj, ev_sink[j], n_ctx=Lc, v_col0=v_col0)
            X = matmul2_resgate(hy.reshape(B * T, HY_CH), att.reshape(B * T, WIN_Q), cast_layer(ev_w_out, j),
                                X, mods[:, :, 2], n_lat=L)
        else:
            lam_init = 0.8 - 0.6 * math.exp(-0.3 * i)
            qkv = matmul(hf, cast_layer(od_w_qkv, j), tn=1024).reshape(B, T, -1)
            q = head_norm_rope(qkv, 0, DIFF_Q, od_qk_g[j, 0], cos, sin, DIFF_DIM ** -0.5 * math.log2(math.e), nh=8)
            k = head_norm_rope(qkv, DIFF_Q, DIFF_Q, od_qk_g[j, 1], cos, sin, 1.0, nh=8)
            lp = od_lambda[j].astype(jnp.float32)
            lam = jnp.exp(jnp.sum(lp[0] * lp[1])) - jnp.exp(jnp.sum(lp[2] * lp[3])) + lam_init
            attn = functools.partial(diff_attention, q, k, qkv, lam, od_subln_g[j], v_col0=2 * DIFF_Q,
                                     out_scale=1.0 - lam_init)
            o = attn(h, n_q=L, q_blk0=0, n_keys=T, key_blk=0, tq=512, tk=T // 3)
            o = attn(o, n_q=Lc, q_blk0=L // Lc, n_keys=Lc, key_blk=L // Lc, tq=Lc, tk=Lc)
            X = matmul_resgate(o.reshape(B * T, D), cast_layer(od_w_out, j), X, mods[:, :, 2], n_lat=L)
        h2 = norm_mod(X, norm_g[i, 1], mods[:, :, 3], mods[:, :, 4]).reshape(B * T, D)
        if i % 2 == 0:
            gact = matmul_swiglu(h2, cast_layer(ev_ffn_w1, j), cast_layer(ev_ffn_w3, j))
            X = matmul_resgate(gact, cast_layer(ev_ffn_w2, j), X, mods[:, :, 5], n_lat=L)
        else:
            y0, y1 = moe_layer(h2, od_router_w[j], od_router_b[j], cast_layer(od_moe_w1, j),
                               cast_layer(od_moe_w3, j), cast_layer(od_moe_w2, j))
            X = moe_combine(X, y0.reshape(B, T, D), y1.reshape(B, T, D), mods[:, :, 5],
                            n_rows=L if i == DEPTH - 1 else T)
    return X
```

```python
import functools
import math

import jax
import jax.numpy as jnp
from jax import lax
from jax.experimental import pallas as pl
from jax.experimental.pallas import tpu as pltpu

D_MODEL = 4096
DEPTH = 4
GRID_W = 64
N_MOD = 6
EPS = 1e-6
NEG_INF = -1e30
HEAD_DIM = 128
ROPE_BASE = 10000.0
BLOCK = 128
WINDOW = 128
HY_CH = D_MODEL // 2
HY_ORDER = 2
HY_SHORT = 3
HY_EMB = 33
HY_BANDS = (HY_EMB - 1) // 2
HY_FILTER_HIDDEN_LAYERS = 2
HY_DIRS = 2
HY_FAST_DECAY = 0.3
HY_SLOW_DECAY = 1.5
HY_DECAY_TARGET = 1e-2
HY_WIDTH = (HY_ORDER + 1) * HY_CH
WIN_HEADS = (D_MODEL - HY_CH) // HEAD_DIM
WIN_KV_HEADS = WIN_HEADS // 4
WIN_GROUP = WIN_HEADS // WIN_KV_HEADS
WIN_Q = WIN_HEADS * HEAD_DIM
WIN_KV = WIN_KV_HEADS * HEAD_DIM
DIFF_DIM = 128
DIFF_HEADS = D_MODEL // (2 * DIFF_DIM)
DIFF_Q = DIFF_HEADS * 2 * DIFF_DIM
N_EXPERTS = 8
TOP_K = 2
assert DEPTH % 2 == 0, "the last layer is assumed to be an (odd) differential-attention / expert layer"

VMEM_LIMIT_BYTES = 56 * 1024 * 1024
ROW_TILE = 768
NORM_TILE = 256
MOE_TILE = 512

HI = lax.Precision.HIGHEST


def _cparams(sem):
    return pltpu.CompilerParams(dimension_semantics=sem, vmem_limit_bytes=VMEM_LIMIT_BYTES)


def _cast_kernel(w_ref, o_ref):
    o_ref[...] = w_ref[...].astype(o_ref.dtype)


def cast_layer(w, j, *, tr=1024):
    lead = w.shape[1:-2]
    R, W = w.shape[-2:]
    rows = math.prod(lead) * R
    w3 = w.reshape(w.shape[0], rows, W)
    tw = next(t for t in (2048, 1536, 1024, 512) if W % t == 0)
    out = pl.pallas_call(
        _cast_kernel,
        grid=(rows // tr, W // tw),
        in_specs=[pl.BlockSpec((None, tr, tw), lambda r, c: (j, r, c))],
        out_specs=pl.BlockSpec((tr, tw), lambda r, c: (r, c)),
        out_shape=jax.ShapeDtypeStruct((rows, W), jnp.bfloat16),
        compiler_params=_cparams(("parallel", "parallel")),
        name="cast_layer",
    )(w3)
    return out.reshape(lead + (R, W))


def _norm_mod_kernel(x_ref, g_ref, shift_ref, scale_ref, o_ref):
    x = x_ref[...]
    y = x * lax.rsqrt(jnp.mean(x * x, axis=-1, keepdims=True) + EPS)
    y = y * g_ref[...]
    o_ref[...] = (y * (1.0 + scale_ref[...]) + shift_ref[...]).astype(o_ref.dtype)


def norm_mod(x, g, shift, scale):
    B, T, D = x.shape
    nt = T // NORM_TILE
    mod_spec = pl.BlockSpec((None, None, 1, D), lambda b, t: (b, t // (nt - 1), 0, 0))
    return pl.pallas_call(
        _norm_mod_kernel,
        grid=(B, nt),
        in_specs=[
            pl.BlockSpec((None, NORM_TILE, D), lambda b, t: (b, t, 0)),
            pl.BlockSpec((1, D), lambda b, t: (0, 0)),
            mod_spec, mod_spec,
        ],
        out_specs=pl.BlockSpec((None, NORM_TILE, D), lambda b, t: (b, t, 0)),
        out_shape=jax.ShapeDtypeStruct((B, T, D), jnp.bfloat16),
        compiler_params=_cparams(("parallel", "parallel")),
        name="norm_mod",
    )(x, g.reshape(1, D), shift.reshape(B, 2, 1, D), scale.reshape(B, 2, 1, D))


def _mm_kernel(a_ref, b_ref, o_ref, acc_ref, *, nk):
    k = pl.program_id(2)
    part = jnp.dot(a_ref[...], b_ref[...], preferred_element_type=jnp.float32)
    if nk == 1:
        o_ref[...] = part.astype(o_ref.dtype)
        return

    @pl.when(k == 0)
    def _():
        acc_ref[...] = part

    @pl.when(k > 0)
    def _():
        acc_ref[...] += part

    @pl.when(k == nk - 1)
    def _():
        o_ref[...] = acc_ref[...].astype(o_ref.dtype)


def matmul(a, b, *, out_dtype=jnp.bfloat16, tm=ROW_TILE, tn=512, tk=None):
    M, K = a.shape
    _, N = b.shape
    tk = K if tk is None else tk
    nk = K // tk
    return pl.pallas_call(
        functools.partial(_mm_kernel, nk=nk),
        grid=(M // tm, N // tn, nk),
        in_specs=[pl.BlockSpec((tm, tk), lambda i, j, k: (i, k)),
                  pl.BlockSpec((tk, tn), lambda i, j, k: (k, j))],
        out_specs=pl.BlockSpec((tm, tn), lambda i, j, k: (i, j)),
        out_shape=jax.ShapeDtypeStruct((M, N), out_dtype),
        scratch_shapes=[pltpu.VMEM((tm, tn), jnp.float32)],
        compiler_params=_cparams(("parallel", "parallel", "arbitrary")),
        name="matmul",
    )(a, b)


def _mm_swiglu_kernel(a_ref, w1_ref, w3_ref, o_ref):
    a = a_ref[...]
    h1 = jnp.dot(a, w1_ref[...], preferred_element_type=jnp.float32)
    h3 = jnp.dot(a, w3_ref[...], preferred_element_type=jnp.float32)
    o_ref[...] = (h1 * jax.nn.sigmoid(h1) * h3).astype(o_ref.dtype)


def matmul_swiglu(a, w1, w3, *, tm=ROW_TILE, tn=512):
    M, K = a.shape
    _, N = w1.shape
    return pl.pallas_call(
        _mm_swiglu_kernel,
        grid=(M // tm, N // tn),
        in_specs=[pl.BlockSpec((tm, K), lambda i, j: (i, 0)),
                  pl.BlockSpec((K, tn), lambda i, j: (0, j)),
                  pl.BlockSpec((K, tn), lambda i, j: (0, j))],
        out_specs=pl.BlockSpec((tm, tn), lambda i, j: (i, j)),
        out_shape=jax.ShapeDtypeStruct((M, N), jnp.bfloat16),
        compiler_params=_cparams(("parallel", "parallel")),
        name="matmul_swiglu",
    )(a, w1, w3)


def _mm_resgate_kernel(a_ref, b_ref, res_ref, gate_ref, o_ref, acc_ref, *, nk, tm, tiles_per_batch, n_lat):
    i = pl.program_id(0)
    k = pl.program_id(2)
    part = jnp.dot(a_ref[...], b_ref[...], preferred_element_type=jnp.float32)

    def finish(acc):
        row = (i % tiles_per_batch) * tm + lax.broadcasted_iota(jnp.int32, (tm, 1), 0)
        gate = jnp.where(row < n_lat, gate_ref[0:1, :], gate_ref[1:2, :])
        o_ref[...] = res_ref[...] + gate * acc

    if nk == 1:
        finish(part)
        return

    @pl.when(k == 0)
    def _():
        acc_ref[...] = part

    @pl.when(k > 0)
    def _():
        acc_ref[...] += part

    @pl.when(k == nk - 1)
    def _():
        finish(acc_ref[...])


def matmul_resgate(a, b, res, gate, *, n_lat, tm=ROW_TILE, tn=512, tk=None):
    B, T, N = res.shape
    M, K = a.shape
    tk = K if tk is None else tk
    nk = K // tk
    tpb = T // tm
    kern = functools.partial(_mm_resgate_kernel, nk=nk, tm=tm, tiles_per_batch=tpb, n_lat=n_lat)
    out = pl.pallas_call(
        kern,
        grid=(M // tm, N // tn, nk),
        in_specs=[pl.BlockSpec((tm, tk), lambda i, j, k: (i, k)),
                  pl.BlockSpec((tk, tn), lambda i, j, k: (k, j)),
                  pl.BlockSpec((tm, tn), lambda i, j, k: (i, j)),
                  pl.BlockSpec((None, 2, tn), lambda i, j, k: (i // tpb, 0, j))],
        out_specs=pl.BlockSpec((tm, tn), lambda i, j, k: (i, j)),
        out_shape=jax.ShapeDtypeStruct((M, N), jnp.float32),
        scratch_shapes=[pltpu.VMEM((tm, tn), jnp.float32)],
        compiler_params=_cparams(("parallel", "parallel", "arbitrary")),
        name="matmul_resgate",
    )(a, b, res.reshape(M, N), gate)
    return out.reshape(B, T, N)


def _mm2_resgate_kernel(a1_ref, a2_ref, b_ref, res_ref, gate_ref, o_ref, *, tm, tiles_per_batch, n_lat):
    i = pl.program_id(0)
    k1 = a1_ref.shape[1]
    acc = (jnp.dot(a1_ref[...], b_ref[:k1, :], preferred_element_type=jnp.float32)
           + jnp.dot(a2_ref[...], b_ref[k1:, :], preferred_element_type=jnp.float32))
    row = (i % tiles_per_batch) * tm + lax.broadcasted_iota(jnp.int32, (tm, 1), 0)
    gate = jnp.where(row < n_lat, gate_ref[0:1, :], gate_ref[1:2, :])
    o_ref[...] = res_ref[...] + gate * acc


def matmul2_resgate(a1, a2, b, res, gate, *, n_lat, tm=ROW_TILE, tn=512):
    B, T, N = res.shape
    M, K1 = a1.shape
    K2 = a2.shape[1]
    tpb = T // tm
    out = pl.pallas_call(
        functools.partial(_mm2_resgate_kernel, tm=tm, tiles_per_batch=tpb, n_lat=n_lat),
        grid=(M // tm, N // tn),
        in_specs=[pl.BlockSpec((tm, K1), lambda i, j: (i, 0)),
                  pl.BlockSpec((tm, K2), lambda i, j: (i, 0)),
                  pl.BlockSpec((K1 + K2, tn), lambda i, j: (0, j)),
                  pl.BlockSpec((tm, tn), lambda i, j: (i, j)),
                  pl.BlockSpec((None, 2, tn), lambda i, j: (i // tpb, 0, j))],
        out_specs=pl.BlockSpec((tm, tn), lambda i, j: (i, j)),
        out_shape=jax.ShapeDtypeStruct((M, N), jnp.float32),
        compiler_params=_cparams(("parallel", "parallel")),
        name="matmul2_resgate",
    )(a1, a2, b, res.reshape(M, N), gate)
    return out.reshape(B, T, N)


def _moe_combine_kernel(x_ref, y0_ref, y1_ref, gate_ref, o_ref):
    y = y0_ref[...].astype(jnp.float32) + y1_ref[...].astype(jnp.float32)
    o_ref[...] = x_ref[...] + gate_ref[...] * y


def moe_combine(x, y0, y1, gate, *, n_rows):
    B, T, D = x.shape
    nt = T // NORM_TILE
    row = pl.BlockSpec((None, NORM_TILE, D), lambda b, t: (b, t, 0))
    return pl.pallas_call(
        _moe_combine_kernel,
        grid=(B, n_rows // NORM_TILE),
        in_specs=[row, row, row, pl.BlockSpec((None, None, 1, D), lambda b, t: (b, t // (nt - 1), 0, 0))],
        out_specs=row,
        out_shape=jax.ShapeDtypeStruct((B, n_rows, D), jnp.float32),
        compiler_params=_cparams(("parallel", "parallel")),
        name="moe_combine",
    )(x, y0, y1, gate.reshape(B, 2, 1, D))


def _window_attn_kernel(sink_ref, q_ref, kc_ref, kp_ref, ko_ref, kn_ref, vc_ref, vp_ref, vo_ref, vn_ref, o_ref,
                        *, n_lat, n_ctx):
    n = pl.program_id(1)
    h = pl.program_id(2)
    G = WIN_GROUP
    q = q_ref[...]
    qs = jnp.concatenate([q[:, g * HEAD_DIM:(g + 1) * HEAD_DIM] for g in range(G)], axis=0)
    k = jnp.concatenate([kc_ref[...], kp_ref[...], ko_ref[...], kn_ref[...]], axis=0)
    v = jnp.concatenate([vc_ref[...], vp_ref[...], vo_ref[...], vn_ref[...]], axis=0)
    s = lax.dot_general(qs, k, (((1,), (1,)), ((), ())), preferred_element_type=jnp.float32)
    nk = n_ctx + 3 * BLOCK
    col = lax.broadcasted_iota(jnp.int32, (G * BLOCK, nk), 1)
    qpos = n * BLOCK + lax.broadcasted_iota(jnp.int32, (G * BLOCK, nk), 0) % BLOCK
    kpos = (n - 1) * BLOCK + (col - n_ctx)
    valid = (col < n_ctx) | ((jnp.abs(qpos - kpos) <= WINDOW) & (kpos >= 0) & (kpos < n_lat) & (qpos < n_lat))
    s = jnp.where(valid, s, NEG_INF)
    sink = jnp.concatenate(
        [jnp.full((BLOCK, 1), sink_ref[h * G + g], jnp.float32) for g in range(G)], axis=0)
    m = jnp.maximum(jnp.max(s, axis=-1, keepdims=True), sink)
    e = jnp.exp(s - m)
    denom = jnp.sum(e, axis=-1, keepdims=True) + jnp.exp(sink - m)
    p = (e / denom).astype(v.dtype)
    o = jnp.dot(p, v, preferred_element_type=jnp.float32)
    for g in range(G):
        o_ref[:, g * HEAD_DIM:(g + 1) * HEAD_DIM] = o[g * BLOCK:(g + 1) * BLOCK, :].astype(o_ref.dtype)


def window_attention(q, k, v, sink, *, n_ctx, v_col0=0):
    B, T, _ = q.shape
    L = T - n_ctx
    nb = L // BLOCK
    kv_blk = lambda f: pl.BlockSpec((None, BLOCK, HEAD_DIM), f)
    ctx_blk = lambda off: pl.BlockSpec((None, n_ctx, HEAD_DIM), lambda b, n, h: (b, L // n_ctx, off + h))
    prev_blk = lambda off: kv_blk(lambda b, n, h: (b, jnp.maximum(n - 1, 0), off + h))
    own_blk = lambda off: kv_blk(lambda b, n, h: (b, n, off + h))
    next_blk = lambda off: kv_blk(lambda b, n, h: (b, jnp.minimum(n + 1, nb - 1), off + h))
    voff = v_col0 // HEAD_DIM
    return pl.pallas_call(
        functools.partial(_window_attn_kernel, n_lat=L, n_ctx=n_ctx),
        grid=(B, T // BLOCK, WIN_KV_HEADS),
        in_specs=[pl.BlockSpec(memory_space=pltpu.SMEM),
                  pl.BlockSpec((None, BLOCK, WIN_GROUP * HEAD_DIM), lambda b, n, h: (b, n, h)),
                  ctx_blk(0), prev_blk(0), own_blk(0), next_blk(0),
                  ctx_blk(voff), prev_blk(voff), own_blk(voff), next_blk(voff)],
        out_specs=pl.BlockSpec((None, BLOCK, WIN_GROUP * HEAD_DIM), lambda b, n, h: (b, n, h)),
        out_shape=jax.ShapeDtypeStruct((B, T, WIN_Q), jnp.bfloat16),
        compiler_params=_cparams(("parallel", "parallel", "parallel")),
        name="window_attention",
    )(sink.astype(jnp.float32), q, k, k, k, k, v, v, v, v)


def _diff_attn_kernel(lam_ref, q_ref, k_ref, v_ref, g_ref, buf_ref, o_ref, m_ref, l_ref, acc_ref,
                      *, n_keys, tk, out_scale):
    del buf_ref
    m_ref[...] = jnp.full(m_ref.shape, NEG_INF, jnp.float32)
    l_ref[...] = jnp.zeros(l_ref.shape, jnp.float32)
    acc_ref[...] = jnp.zeros(acc_ref.shape, jnp.float32)

    def step(j, carry):
        ks = pl.multiple_of(j * tk, tk)
        v = v_ref[pl.ds(ks, tk), :]
        scores = []
        for sub in range(2):
            q = q_ref[:, sub * DIFF_DIM:(sub + 1) * DIFF_DIM]
            k = k_ref[pl.ds(ks, tk), sub * DIFF_DIM:(sub + 1) * DIFF_DIM]
            scores.append(lax.dot_general(q, k, (((1,), (1,)), ((), ())), preferred_element_type=jnp.float32))
        for sub in range(2):
            s = scores[sub]
            m_old = m_ref[sub]
            m_new = jnp.maximum(m_old, jnp.max(s, axis=-1, keepdims=True))
            alpha = jnp.exp2(m_old - m_new)
            p = jnp.exp2(s - m_new)
            l_ref[sub] = alpha * l_ref[sub] + jnp.sum(p, axis=-1, keepdims=True)
            acc_ref[sub] = alpha * acc_ref[sub] + jnp.dot(p.astype(v.dtype), v, preferred_element_type=jnp.float32)
            m_ref[sub] = m_new
        return carry

    lax.fori_loop(0, n_keys // tk, step, 0)
    lam = lam_ref[0]
    o = acc_ref[0] / l_ref[0] - lam * (acc_ref[1] / l_ref[1])
    o = o * lax.rsqrt(jnp.mean(o * o, axis=-1, keepdims=True) + EPS)
    o_ref[...] = (o * g_ref[...] * out_scale).astype(o_ref.dtype)


def diff_attention(q, k, v, lam, subln_g, out_buf, *, n_q, q_blk0, n_keys, key_blk, tq, tk, out_scale, v_col0=0):
    B = q.shape[0]
    W = 2 * DIFF_DIM
    voff = v_col0 // W
    return pl.pallas_call(
        functools.partial(_diff_attn_kernel, n_keys=n_keys, tk=tk, out_scale=out_scale),
        grid=(B, DIFF_HEADS, n_q // tq),
        in_specs=[pl.BlockSpec(memory_space=pltpu.SMEM),
                  pl.BlockSpec((None, tq, W), lambda b, h, i: (b, q_blk0 + i, h)),
                  pl.BlockSpec((None, n_keys, W), lambda b, h, i: (b, key_blk, h)),
                  pl.BlockSpec((None, n_keys, W), lambda b, h, i: (b, key_blk, voff + h)),
                  pl.BlockSpec((1, W), lambda b, h, i: (0, 0)),
                  pl.BlockSpec(memory_space=pl.ANY)],
        out_specs=pl.BlockSpec((None, tq, W), lambda b, h, i: (b, q_blk0 + i, h)),
        scratch_shapes=[pltpu.VMEM((2, tq, 1), jnp.float32),
                        pltpu.VMEM((2, tq, 1), jnp.float32),
                        pltpu.VMEM((2, tq, W), jnp.float32)],
        out_shape=jax.ShapeDtypeStruct(out_buf.shape, out_buf.dtype),
        input_output_aliases={5: 0},
        compiler_params=_cparams(("parallel", "parallel", "arbitrary")),
        name="diff_attention",
    )(lam.reshape(1).astype(jnp.float32), q, k, v, subln_g.reshape(1, W).astype(jnp.float32), out_buf)


def _moe_up_kernel(te_ref, tv_ref, x_ref, w1_ref, w3_ref, o_ref):
    i = pl.program_id(1)

    @pl.when(tv_ref[i] > 0)
    def _():
        x = x_ref[...]
        h1 = jnp.dot(x, w1_ref[...], preferred_element_type=jnp.float32)
        h3 = jnp.dot(x, w3_ref[...], preferred_element_type=jnp.float32)
        o_ref[...] = (h1 * jax.nn.sigmoid(h1) * h3).astype(o_ref.dtype)

    @pl.when(tv_ref[i] == 0)
    def _():
        o_ref[...] = jnp.zeros(o_ref.shape, o_ref.dtype)


def _moe_down_kernel(te_ref, tv_ref, g_ref, w2_ref, rg_ref, o_ref):
    i = pl.program_id(1)

    @pl.when(tv_ref[i] > 0)
    def _():
        y = jnp.dot(g_ref[...], w2_ref[...], preferred_element_type=jnp.float32)
        o_ref[...] = (rg_ref[...] * y).astype(o_ref.dtype)

    @pl.when(tv_ref[i] == 0)
    def _():
        o_ref[...] = jnp.zeros(o_ref.shape, o_ref.dtype)


def moe_experts(xs, w1, w3, w2, tile_expert, tile_valid, row_gate, *, tn_up=768, tn_down=1024):
    P, D = xs.shape
    F = w1.shape[2]
    tm = MOE_TILE
    up = pl.pallas_call(
        _moe_up_kernel,
        grid_spec=pltpu.PrefetchScalarGridSpec(
            num_scalar_prefetch=2,
            grid=(F // tn_up, P // tm),
            in_specs=[pl.BlockSpec((tm, D), lambda j, i, te, tv: (i, 0)),
                      pl.BlockSpec((None, D, tn_up), lambda j, i, te, tv: (te[i], 0, j)),
                      pl.BlockSpec((None, D, tn_up), lambda j, i, te, tv: (te[i], 0, j))],
            out_specs=pl.BlockSpec((tm, tn_up), lambda j, i, te, tv: (i, j)),
        ),
        out_shape=jax.ShapeDtypeStruct((P, F), jnp.bfloat16),
        compiler_params=_cparams(("parallel", "arbitrary")),
        name="moe_up",
    )(tile_expert, tile_valid, xs, w1, w3)
    return pl.pallas_call(
        _moe_down_kernel,
        grid_spec=pltpu.PrefetchScalarGridSpec(
            num_scalar_prefetch=2,
            grid=(D // tn_down, P // tm),
            in_specs=[pl.BlockSpec((tm, F), lambda j, i, te, tv: (i, 0)),
                      pl.BlockSpec((None, F, tn_down), lambda j, i, te, tv: (te[i], 0, j)),
                      pl.BlockSpec((tm, 1), lambda j, i, te, tv: (i, 0))],
            out_specs=pl.BlockSpec((tm, tn_down), lambda j, i, te, tv: (i, j)),
        ),
        out_shape=jax.ShapeDtypeStruct((P, D), jnp.bfloat16),
        compiler_params=_cparams(("parallel", "arbitrary")),
        name="moe_down",
    )(tile_expert, tile_valid, up, w2, row_gate)


def moe_layer(h2, router_w, router_b, w1, w3, w2):
    N, D = h2.shape
    tm = MOE_TILE
    logits = jnp.dot(h2.astype(jnp.float32), router_w, precision=HI) + router_b
    top_v, top_i = lax.top_k(logits, TOP_K)
    gates = jax.nn.softmax(top_v, axis=-1)
    A = N * TOP_K
    e_flat = top_i.reshape(A).astype(jnp.int32)
    order = jnp.argsort(e_flat, stable=True)
    e_sorted = e_flat[order]
    counts = jnp.sum(e_flat[:, None] == jnp.arange(N_EXPERTS, dtype=jnp.int32)[None, :], axis=0).astype(jnp.int32)
    padded = ((counts + tm - 1) // tm) * tm
    start_unpadded = jnp.cumsum(counts) - counts
    start_padded = jnp.cumsum(padded) - padded
    dest_sorted = start_padded[e_sorted] + (jnp.arange(A, dtype=jnp.int32) - start_unpadded[e_sorted])
    P = A + N_EXPERTS * tm
    tile_start = jnp.arange(P // tm, dtype=jnp.int32) * tm
    ends = jnp.cumsum(padded)
    tile_expert = jnp.minimum(jnp.sum(tile_start[:, None] >= ends[None, :], axis=1), N_EXPERTS - 1).astype(jnp.int32)
    tile_valid = (tile_start < ends[-1]).astype(jnp.int32)
    rows = jnp.arange(P, dtype=jnp.int32)
    row_e = tile_expert[rows // tm]
    row_idx = rows - start_padded[row_e]
    row_ok = (row_idx < counts[row_e]) & (rows < ends[-1])
    src = order[jnp.clip(start_unpadded[row_e] + row_idx, 0, A - 1)]
    row_token = jnp.where(row_ok, src // TOP_K, 0).astype(jnp.int32)
    row_gate = jnp.where(row_ok, gates.reshape(A)[src], 0.0)
    pos = dest_sorted[jnp.argsort(order)].reshape(N, TOP_K)
    take = lambda a, idx: a.at[idx].get(mode="promise_in_bounds")
    xs = take(h2, row_token)
    y = moe_experts(xs, w1, w3, w2, tile_expert, tile_valid, row_gate.reshape(P, 1))
    return take(y, pos[:, 0]), take(y, pos[:, 1])


FFT_R = 128
FFT_K1 = 72
Z_PITCH = FFT_R + 8
S1_PITCH = 2 * FFT_K1 + 8
S2_PITCH = 2 * FFT_R + 8
HY_LANES = 128
FFT_UNROLL = 32


def _dft_tables():
    R, K1 = FFT_R, FFT_K1
    N = R * R
    i32 = jnp.int32
    b = jnp.arange(R, dtype=i32)[:, None, None]
    k1 = jnp.arange(K1, dtype=i32)[None, :, None]
    a = jnp.arange(R, dtype=i32)[None, None, :]
    th = (2.0 * math.pi / N) * ((k1 * (R * a + b)) % N).astype(jnp.float32)
    f1 = jnp.concatenate([jnp.cos(th), -jnp.sin(th)], axis=1)
    w = jnp.where(jnp.arange(K1) > R // 2, 0.0, jnp.where((jnp.arange(K1) % (R // 2)) == 0, 1.0, 2.0)) / N
    the = jnp.swapaxes(th[:, :, :R // 2], 1, 2)
    e = jnp.concatenate([w * jnp.cos(the), -w * jnp.sin(the)], axis=2)
    k2 = jnp.arange(R, dtype=i32)
    ph = (2.0 * math.pi / R) * ((k2[:, None] * k2[None, :]) % R).astype(jnp.float32)
    c, s = jnp.cos(ph), jnp.sin(ph)
    g = jnp.block([[c, s], [-s, c]])
    ginv = jnp.block([[c, -s], [s, c]])
    bf = jnp.bfloat16
    return f1.astype(bf), e.astype(bf), g.astype(bf), ginv.astype(bf)


def _fft_stage1(src_ref, f1_ref, s1_ref, n_a):
    def body(b, carry):
        zb = src_ref[pl.ds(b, n_a, stride=Z_PITCH), :]
        s1_ref[pl.ds(pl.multiple_of(b * S1_PITCH, 8), 2 * FFT_K1), :] = jnp.dot(
            f1_ref[b], zb.astype(jnp.bfloat16), preferred_element_type=jnp.float32)
        return carry
    lax.fori_loop(0, FFT_R, body, 0, unroll=FFT_UNROLL)


def _fft_stage2(s1_ref, g_ref, k1):
    are = s1_ref[pl.ds(k1, FFT_R, stride=S1_PITCH), :]
    aim = s1_ref[pl.ds(FFT_K1 + k1, FFT_R, stride=S1_PITCH), :]
    r = jnp.concatenate([are, aim], axis=0).astype(jnp.bfloat16)
    return jnp.dot(g_ref[...], r, preferred_element_type=jnp.float32)


def _filter_fft_kernel(f_ref, f1_ref, g_ref, h_ref, s1_ref):
    _fft_stage1(f_ref, f1_ref, s1_ref, FFT_R)

    def body(k1, carry):
        h_ref[k1] = _fft_stage2(s1_ref, g_ref, k1).astype(h_ref.dtype)
        return carry
    lax.fori_loop(0, FFT_K1, body, 0, unroll=FFT_UNROLL)


def filter_spectrum(filt_padded, f1, g):
    rows, width = filt_padded.shape
    nt = width // HY_LANES
    once = pl.Buffered(1)
    return pl.pallas_call(
        _filter_fft_kernel,
        grid=(nt,),
        in_specs=[pl.BlockSpec((rows, HY_LANES), lambda c: (0, c)),
                  pl.BlockSpec(f1.shape, lambda c: (0, 0, 0), pipeline_mode=once),
                  pl.BlockSpec(g.shape, lambda c: (0, 0), pipeline_mode=once)],
        out_specs=pl.BlockSpec((None, FFT_K1, 2 * FFT_R, HY_LANES), lambda c: (c, 0, 0, 0)),
        out_shape=jax.ShapeDtypeStruct((nt, FFT_K1, 2 * FFT_R, HY_LANES), jnp.bfloat16),
        scratch_shapes=[pltpu.VMEM((FFT_R * S1_PITCH, HY_LANES), jnp.float32)],
        compiler_params=_cparams(("arbitrary",)),
        name="hyena_filter_fft",
    )(filt_padded, f1, g)


def _short_conv_chunks(x_ref, w_ref, b_ref, dst_ref, n_chunks, pitch=Z_PITCH):
    R = FFT_R
    w = w_ref[...]
    bias = b_ref[...]
    row = lax.broadcasted_iota(jnp.int32, (R, HY_LANES), 0)
    for a in range(n_chunks):
        xc = x_ref[a * R:(a + 1) * R, :].astype(jnp.float32)
        prev = pltpu.roll(xc, 1, axis=0)
        nxt = pltpu.roll(xc, R - 1, axis=0)
        if a > 0:
            last = x_ref[a * R - 16:a * R, :].astype(jnp.float32)[15:16, :]
        else:
            last = jnp.zeros((1, HY_LANES), jnp.float32)
        if a < n_chunks - 1:
            first = x_ref[(a + 1) * R:(a + 1) * R + 16, :].astype(jnp.float32)[0:1, :]
        else:
            first = jnp.zeros((1, HY_LANES), jnp.float32)
        prev = jnp.where(row == 0, last, prev)
        nxt = jnp.where(row == R - 1, first, nxt)
        dst_ref[a * pitch:a * pitch + R, :] = bias + prev * w[0:1, :] + xc * w[1:2, :] + nxt * w[2:3, :]


def _hyena_conv_kernel(v_ref, x_ref, wv_ref, bv_ref, wx_ref, bx_ref, skip_ref, h_ref, f1_ref, e_ref, g_ref, gi_ref,
                       buf_ref, o_ref, z_ref, gate_ref, s1_ref, s2_ref, *, n_chunks):
    del buf_ref
    n = pl.program_id(2)
    R, K1 = FFT_R, FFT_K1

    @pl.when(n == 0)
    def _():
        _short_conv_chunks(v_ref, wv_ref, bv_ref, z_ref, n_chunks)

    _short_conv_chunks(x_ref, wx_ref, bx_ref, gate_ref, n_chunks)

    _fft_stage1(z_ref, f1_ref, s1_ref, n_chunks)

    def mid(k1, carry):
        x = _fft_stage2(s1_ref, g_ref, k1)
        h = h_ref[k1].astype(jnp.float32)
        xr, xi, hr, hi = x[:R], x[R:], h[:R], h[R:]
        p = jnp.concatenate([xr * hr - xi * hi, xr * hi + xi * hr], axis=0).astype(jnp.bfloat16)
        s2_ref[pl.ds(pl.multiple_of(k1 * S2_PITCH, 8), 2 * R), :] = jnp.dot(
            gi_ref[...], p, preferred_element_type=jnp.float32)
        return carry
    lax.fori_loop(0, K1, mid, 0, unroll=FFT_UNROLL)

    skip = skip_ref[...]

    def last(b, carry):
        qre = s2_ref[pl.ds(b, K1, stride=S2_PITCH), :]
        qim = s2_ref[pl.ds(R + b, K1, stride=S2_PITCH), :]
        r = jnp.concatenate([qre, qim], axis=0).astype(jnp.bfloat16)
        conv = jnp.dot(e_ref[b], r, preferred_element_type=jnp.float32)
        zold = z_ref[pl.ds(b, n_chunks, stride=Z_PITCH), :]
        gate = gate_ref[pl.ds(b, n_chunks, stride=Z_PITCH), :]
        z_ref[pl.ds(b, n_chunks, stride=Z_PITCH), :] = gate * (conv + zold * skip)
        return carry
    lax.fori_loop(0, R, last, 0, unroll=FFT_UNROLL)

    @pl.when(n == HY_ORDER - 1)
    def _():
        for a in range(n_chunks):
            o_ref[a * R:(a + 1) * R, :] = z_ref[a * Z_PITCH:a * Z_PITCH + R, :].astype(o_ref.dtype)


def hyena_conv(proj, conv_w, conv_b, skip, spec, tables, out_buf, *, n_lat):
    f1, e, g, ginv = tables
    B = proj.shape[0]
    n_chunks = n_lat // FFT_R
    nt = HY_CH // HY_LANES
    once = pl.Buffered(1)
    grp = lambda n: (1 + n) * nt
    f1h = f1[:, :, :n_chunks]
    return pl.pallas_call(
        functools.partial(_hyena_conv_kernel, n_chunks=n_chunks),
        grid=(B, nt, HY_ORDER),
        in_specs=[pl.BlockSpec((None, n_lat, HY_LANES), lambda b, c, n: (b, 0, c)),
                  pl.BlockSpec((None, n_lat, HY_LANES), lambda b, c, n: (b, 0, grp(n) + c)),
                  pl.BlockSpec((HY_SHORT, HY_LANES), lambda b, c, n: (0, c)),
                  pl.BlockSpec((1, HY_LANES), lambda b, c, n: (0, c)),
                  pl.BlockSpec((HY_SHORT, HY_LANES), lambda b, c, n: (0, grp(n) + c)),
                  pl.BlockSpec((1, HY_LANES), lambda b, c, n: (0, grp(n) + c)),
                  pl.BlockSpec((None, 1, HY_LANES), lambda b, c, n: (n, 0, c)),
                  pl.BlockSpec((None, FFT_K1, 2 * FFT_R, HY_LANES), lambda b, c, n: (n * nt + c, 0, 0, 0),
                               pipeline_mode=once),
                  pl.BlockSpec(f1h.shape, lambda b, c, n: (0, 0, 0), pipeline_mode=once),
                  pl.BlockSpec(e.shape, lambda b, c, n: (0, 0, 0), pipeline_mode=once),
                  pl.BlockSpec(g.shape, lambda b, c, n: (0, 0), pipeline_mode=once),
                  pl.BlockSpec(ginv.shape, lambda b, c, n: (0, 0), pipeline_mode=once),
                  pl.BlockSpec(memory_space=pl.ANY)],
        out_specs=pl.BlockSpec((None, n_lat, HY_LANES), lambda b, c, n: (b, 0, c)),
        out_shape=jax.ShapeDtypeStruct(out_buf.shape, out_buf.dtype),
        input_output_aliases={12: 0},
        scratch_shapes=[pltpu.VMEM((n_chunks * Z_PITCH, HY_LANES), jnp.float32),
                        pltpu.VMEM((n_chunks * Z_PITCH, HY_LANES), jnp.float32),
                        pltpu.VMEM((FFT_R * S1_PITCH, HY_LANES), jnp.float32),
                        pltpu.VMEM((FFT_K1 * S2_PITCH, HY_LANES), jnp.float32)],
        compiler_params=_cparams(("parallel", "parallel", "arbitrary")),
        name="hyena_conv",
    )(proj, proj, conv_w, conv_b.reshape(1, -1), conv_w, conv_b.reshape(1, -1),
      skip.reshape(HY_ORDER, 1, HY_CH), spec, f1h, e, g, ginv, out_buf)


def _ctx_dft_tables(n_ctx):
    N = 2 * n_ctx
    nk = -(-(n_ctx + 1) // 16) * 16
    k = jnp.arange(nk, dtype=jnp.int32)[:, None]
    t = jnp.arange(N, dtype=jnp.int32)[None, :]
    th = (2.0 * math.pi / N) * ((k * t) % N).astype(jnp.float32)
    fwd = jnp.concatenate([jnp.cos(th), -jnp.sin(th)], axis=0)
    w = jnp.where(k > n_ctx, 0.0, jnp.where((k % n_ctx) == 0, 1.0, 2.0)) / N
    inv = jnp.concatenate([(w * jnp.cos(th[:, :n_ctx])).T, (-w * jnp.sin(th[:, :n_ctx])).T], axis=1)
    return fwd.astype(jnp.bfloat16), inv.astype(jnp.bfloat16)


def _hyena_ctx_kernel(v_ref, x1_ref, x2_ref, wv_ref, bv_ref, w1_ref, b1_ref, w2_ref, b2_ref, skip_ref,
                      f0_ref, f1_ref, fwd_ref, inv_ref, buf_ref, o_ref, z_ref, g1_ref, g2_ref, *, n_ctx):
    del buf_ref
    nc = n_ctx // FFT_R
    _short_conv_chunks(v_ref, wv_ref, bv_ref, z_ref, nc, pitch=FFT_R)
    _short_conv_chunks(x1_ref, w1_ref, b1_ref, g1_ref, nc, pitch=FFT_R)
    _short_conv_chunks(x2_ref, w2_ref, b2_ref, g2_ref, nc, pitch=FFT_R)
    nk = fwd_ref.shape[0] // 2
    z = z_ref[...]
    for n, (filt_ref, gate_ref) in enumerate(((f0_ref, g1_ref), (f1_ref, g2_ref))):
        h = jnp.dot(fwd_ref[...], filt_ref[...].astype(jnp.bfloat16), preferred_element_type=jnp.float32)
        x = jnp.dot(fwd_ref[:, :n_ctx], z.astype(jnp.bfloat16), preferred_element_type=jnp.float32)
        xr, xi, hr, hi = x[:nk], x[nk:], h[:nk], h[nk:]
        p = jnp.concatenate([xr * hr - xi * hi, xr * hi + xi * hr], axis=0).astype(jnp.bfloat16)
        conv = jnp.dot(inv_ref[...], p, preferred_element_type=jnp.float32)
        z = gate_ref[...] * (conv + z * skip_ref[n:n + 1, :])
    o_ref[...] = z.astype(o_ref.dtype)


def hyena_ctx(proj, conv_w, conv_b, skip, filt, tables, out_buf, *, n_lat, n_ctx):
    fwd, inv = tables
    B = proj.shape[0]
    nt = HY_CH // HY_LANES
    rb = n_lat // n_ctx
    cb2 = conv_b.reshape(1, -1)
    row = lambda g: pl.BlockSpec((None, n_ctx, HY_LANES), lambda b, c: (b, rb, g * nt + c))
    wsp = lambda g: pl.BlockSpec((HY_SHORT, HY_LANES), lambda b, c: (0, g * nt + c))
    bsp = lambda g: pl.BlockSpec((1, HY_LANES), lambda b, c: (0, g * nt + c))
    fsp = lambda n: pl.BlockSpec((2 * n_ctx, HY_LANES), lambda b, c: (0, n * nt + c))
    scr = pltpu.VMEM((n_ctx, HY_LANES), jnp.float32)
    return pl.pallas_call(
        functools.partial(_hyena_ctx_kernel, n_ctx=n_ctx),
        grid=(B, nt),
        in_specs=[row(0), row(1), row(2), wsp(0), bsp(0), wsp(1), bsp(1), wsp(2), bsp(2),
                  pl.BlockSpec((HY_ORDER, HY_LANES), lambda b, c: (0, c)),
                  fsp(0), fsp(1),
                  pl.BlockSpec(fwd.shape, lambda b, c: (0, 0)),
                  pl.BlockSpec(inv.shape, lambda b, c: (0, 0)),
                  pl.BlockSpec(memory_space=pl.ANY)],
        out_specs=pl.BlockSpec((None, n_ctx, HY_LANES), lambda b, c: (b, rb, c)),
        out_shape=jax.ShapeDtypeStruct(out_buf.shape, out_buf.dtype),
        input_output_aliases={14: 0},
        scratch_shapes=[scr, scr, scr],
        compiler_params=_cparams(("parallel", "parallel")),
        name="hyena_ctx",
    )(proj, proj, proj, conv_w, cb2, conv_w, cb2, conv_w, cb2, skip, filt, filt, fwd, inv, out_buf)


def _norm_rope_kernel(x_ref, g_ref, cos_ref, sin_ref, o_ref, *, scale, nh):
    cos = cos_ref[...]
    sin = sin_ref[...]
    g = g_ref[...]
    lane = lax.broadcasted_iota(jnp.int32, cos.shape, 1)
    lower = (lane % (HEAD_DIM // 2)) < (HEAD_DIM // 4)
    for h in range(nh):
        x = x_ref[:, h * HEAD_DIM:(h + 1) * HEAD_DIM].astype(jnp.float32)
        y = x * lax.rsqrt(jnp.mean(x * x, axis=-1, keepdims=True) + EPS) * g
        rot = jnp.where(lower, -pltpu.roll(y, HEAD_DIM - HEAD_DIM // 4, axis=1), pltpu.roll(y, HEAD_DIM // 4, axis=1))
        o_ref[:, h * HEAD_DIM:(h + 1) * HEAD_DIM] = ((y * cos + rot * sin) * scale).astype(o_ref.dtype)


def head_norm_rope(t, col0, width, g, cos, sin, scale, *, tr=ROW_TILE, nh=4):
    B, T, _ = t.shape
    wb = nh * HEAD_DIM
    c0 = col0 // wb
    return pl.pallas_call(
        functools.partial(_norm_rope_kernel, scale=scale, nh=nh),
        grid=(B, T // tr, width // wb),
        in_specs=[pl.BlockSpec((None, tr, wb), lambda b, r, j: (b, r, c0 + j)),
                  pl.BlockSpec((1, HEAD_DIM), lambda b, r, j: (0, 0)),
                  pl.BlockSpec((tr, HEAD_DIM), lambda b, r, j: (r, 0)),
                  pl.BlockSpec((tr, HEAD_DIM), lambda b, r, j: (r, 0))],
        out_specs=pl.BlockSpec((None, tr, wb), lambda b, r, j: (b, r, j)),
        out_shape=jax.ShapeDtypeStruct((B, T, width), jnp.bfloat16),
        compiler_params=_cparams(("parallel", "parallel", "parallel")),
        name="head_norm_rope",
    )(t, g.reshape(1, HEAD_DIM).astype(jnp.float32), cos, sin)


def _rope_tables(L, n_ctx):
    rows = jnp.repeat(jnp.arange(L // GRID_W), GRID_W)
    cols = jnp.tile(jnp.arange(GRID_W), L // GRID_W)
    quarter = HEAD_DIM // 4
    inv = ROPE_BASE ** (-jnp.arange(quarter, dtype=jnp.float32) / quarter)
    ar = rows.astype(jnp.float32)[:, None] * inv
    ac = cols.astype(jnp.float32)[:, None] * inv
    ang = jnp.concatenate([ar, ar, ac, ac], axis=-1)
    cos = jnp.concatenate([jnp.cos(ang), jnp.ones((n_ctx, HEAD_DIM), jnp.float32)], axis=0)
    sin = jnp.concatenate([jnp.sin(ang), jnp.zeros((n_ctx, HEAD_DIM), jnp.float32)], axis=0)
    return cos, sin


def _implicit_filters(L, w_in, w_hid, b, freq, w_out, pitch=FFT_R):
    f32 = jnp.float32
    t = jnp.linspace(0.0, 1.0, L, dtype=f32)[:, None]
    w = (2.0 * math.pi / L) * jnp.arange(L, dtype=f32)[:, None]
    f = jnp.linspace(1e-4, HY_BANDS - 1, HY_BANDS, dtype=f32)[None, :]
    z = jnp.concatenate([t, jnp.cos(f * w), -jnp.sin(f * w)], axis=-1)
    h = jnp.sin(freq[0] * (jnp.dot(z, w_in, precision=HI) + b[0]))
    for n in range(HY_FILTER_HIDDEN_LAYERS):
        h = jnp.sin(freq[n + 1] * (jnp.dot(h, w_hid[n], precision=HI) + b[n + 1]))
    width = HY_ORDER * HY_CH
    max_decay = math.log(HY_DECAY_TARGET) / HY_FAST_DECAY
    min_decay = math.log(HY_DECAY_TARGET) / HY_SLOW_DECAY
    deltas = jnp.abs(jnp.linspace(min_decay, max_decay, width, dtype=f32))[None, :]
    hb = jnp.concatenate([jnp.zeros((1, h.shape[1]), f32), h[:0:-1]], axis=0)
    tb = jnp.concatenate([jnp.zeros((1, 1), f32), t[:0:-1]], axis=0)
    chunk = lambda a: jnp.pad(a.reshape(2, L // FFT_R, FFT_R, a.shape[-1]),
                              ((0, 0), (0, 0), (0, pitch - FFT_R), (0, 0))).reshape(2, -1, a.shape[-1])
    hh = chunk(jnp.stack([h, hb]))
    tt = chunk(jnp.stack([t, tb]))
    w2 = jnp.stack([w_out[:, :width], w_out[:, width:]])
    filt = jnp.einsum('hrk,hkw->hrw', hh, w2, precision=HI) * jnp.exp(-tt * deltas)
    return filt.reshape(-1, width)


def _ada_modulation(cv, down, up, b):
    m = jnp.dot(jnp.dot(jax.nn.silu(cv), down, precision=HI), up, precision=HI) + b
    return m.reshape(m.shape[:-1] + (N_MOD, m.shape[-1] // N_MOD))


def kernel(x, c, ctx, c_ctx, norm_g, ada_down, ada_up, ada_b, ev_w_in, ev_conv_w, ev_conv_b, ev_filt_w_in, ev_filt_w_hid, ev_filt_b, ev_filt_freq, ev_filt_w_out, ev_hy_skip, ev_qk_g, ev_sink, ev_w_out, ev_ffn_w1, ev_ffn_w3, ev_ffn_w2, od_w_qkv, od_qk_g, od_lambda, od_subln_g, od_w_out, od_router_w, od_router_b, od_moe_w1, od_moe_w3, od_moe_w2):
    B, L, D = x.shape
    Lc = ctx.shape[1]
    T = L + Lc
    bf16 = jnp.bfloat16
    cos, sin = _rope_tables(L, Lc)
    tables = _dft_tables()
    ctx_tables = _ctx_dft_tables(Lc)
    X = jnp.concatenate([x, ctx], axis=1)
    qk_scale = HEAD_DIM ** -0.5

    for i in range(DEPTH):
        j = i // 2
        m_l = _ada_modulation(c, ada_down[i], ada_up[i], ada_b[i])
        m_c = _ada_modulation(c_ctx, ada_down[i], ada_up[i], ada_b[i])
        mods = jnp.stack([m_l, jnp.broadcast_to(m_c[None], (B, N_MOD, D))], axis=1)

        h = norm_mod(X, norm_g[i, 0], mods[:, :, 0], mods[:, :, 1])
        hf = h.reshape(B * T, D)
        if i % 2 == 0:
            proj = matmul(hf, cast_layer(ev_w_in, j), tn=1024).reshape(B, T, -1)
            v_col0 = HY_WIDTH + WIN_Q + WIN_KV
            q = head_norm_rope(proj, HY_WIDTH, WIN_Q, ev_qk_g[j, 0], cos, sin, qk_scale)
            k = head_norm_rope(proj, HY_WIDTH + WIN_Q, WIN_KV, ev_qk_g[j, 1], cos, sin, 1.0)
            filt_args = (ev_filt_w_in[j], ev_filt_w_hid[j], ev_filt_b[j], ev_filt_freq[j], ev_filt_w_out[j])
            spec = filter_spectrum(_implicit_filters(L, *filt_args, pitch=Z_PITCH), tables[0], tables[2])
            hy = jnp.zeros((B, T, HY_CH), bf16)
            hy = hyena_conv(proj, ev_conv_w[j], ev_conv_b[j], ev_hy_skip[j], spec, tables, hy, n_lat=L)
            hy = hyena_ctx(proj, ev_conv_w[j], ev_conv_b[j], ev_hy_skip[j], _implicit_filters(Lc, *filt_args),
                           ctx_tables, hy, n_lat=L, n_ctx=Lc)
            att = window_attention(q, k, proj, ev_sink[j], n_ctx=Lc, v_col0=v_col0)
            X = matmul2_resgate(hy.reshape(B * T, HY_CH), att.reshape(B * T, WIN_Q), cast_layer(ev_w_out, j),
                                X, mods[:, :, 2], n_lat=L)
        else:
            lam_init = 0.8 - 0.6 * math.exp(-0.3 * i)
            qkv = matmul(hf, cast_layer(od_w_qkv, j), tn=1024).reshape(B, T, -1)
            q = head_norm_rope(qkv, 0, DIFF_Q, od_qk_g[j, 0], cos, sin, DIFF_DIM ** -0.5 * math.log2(math.e), nh=8)
            k = head_norm_rope(qkv, DIFF_Q, DIFF_Q, od_qk_g[j, 1], cos, sin, 1.0, nh=8)
            lp = od_lambda[j].astype(jnp.float32)
            lam = jnp.exp(jnp.sum(lp[0] * lp[1])) - jnp.exp(jnp.sum(lp[2] * lp[3])) + lam_init
            attn = functools.partial(diff_attention, q, k, qkv, lam, od_subln_g[j], v_col0=2 * DIFF_Q,
                                     out_scale=1.0 - lam_init)
            o = attn(h, n_q=L, q_blk0=0, n_keys=T, key_blk=0, tq=512, tk=T // 2)
            o = attn(o, n_q=Lc, q_blk0=L // Lc, n_keys=Lc, key_blk=L // Lc, tq=Lc, tk=Lc)
            X = matmul_resgate(o.reshape(B * T, D), cast_layer(od_w_out, j), X, mods[:, :, 2], n_lat=L)
        h2 = norm_mod(X, norm_g[i, 1], mods[:, :, 3], mods[:, :, 4]).reshape(B * T, D)
        if i % 2 == 0:
            gact = matmul_swiglu(h2, cast_layer(ev_ffn_w1, j), cast_layer(ev_ffn_w3, j))
            X = matmul_resgate(gact, cast_layer(ev_ffn_w2, j), X, mods[:, :, 5], n_lat=L)
        else:
            y0, y1 = moe_layer(h2, od_router_w[j], od_router_b[j], cast_layer(od_moe_w1, j),
                               cast_layer(od_moe_w3, j), cast_layer(od_moe_w2, j))
            X = moe_combine(X, y0.reshape(B, T, D), y1.reshape(B, T, D), mods[:, :, 5],
                            n_rows=L if i == DEPTH - 1 else T)
    return X
```

```python
import functools
import math

import jax
import jax.numpy as jnp
from jax import lax
from jax.experimental import pallas as pl
from jax.experimental.pallas import tpu as pltpu

D_MODEL = 4096
DEPTH = 4
GRID_W = 64
N_MOD = 6
EPS = 1e-6
NEG_INF = -1e30
HEAD_DIM = 128
ROPE_BASE = 10000.0
BLOCK = 128
WINDOW = 128
HY_CH = D_MODEL // 2
HY_ORDER = 2
HY_SHORT = 3
HY_EMB = 33
HY_BANDS = (HY_EMB - 1) // 2
HY_FILTER_HIDDEN_LAYERS = 2
HY_DIRS = 2
HY_FAST_DECAY = 0.3
HY_SLOW_DECAY = 1.5
HY_DECAY_TARGET = 1e-2
HY_WIDTH = (HY_ORDER + 1) * HY_CH
WIN_HEADS = (D_MODEL - HY_CH) // HEAD_DIM
WIN_KV_HEADS = WIN_HEADS // 4
WIN_GROUP = WIN_HEADS // WIN_KV_HEADS
WIN_Q = WIN_HEADS * HEAD_DIM
WIN_KV = WIN_KV_HEADS * HEAD_DIM
DIFF_DIM = 128
DIFF_HEADS = D_MODEL // (2 * DIFF_DIM)
DIFF_Q = DIFF_HEADS * 2 * DIFF_DIM
N_EXPERTS = 8
TOP_K = 2
assert DEPTH % 2 == 0, "the last layer is assumed to be an (odd) differential-attention / expert layer"

VMEM_LIMIT_BYTES = 56 * 1024 * 1024
ROW_TILE = 768
NORM_TILE = 256
MOE_TILE = 512

HI = lax.Precision.HIGHEST


def _cparams(sem):
    return pltpu.CompilerParams(dimension_semantics=sem, vmem_limit_bytes=VMEM_LIMIT_BYTES)


def _cast_kernel(w_ref, o_ref):
    o_ref[...] = w_ref[...].astype(o_ref.dtype)


def cast_layer(w, j, *, tr=1024):
    lead = w.shape[1:-2]
    R, W = w.shape[-2:]
    rows = math.prod(lead) * R
    w3 = w.reshape(w.shape[0], rows, W)
    tw = next(t for t in (2048, 1536, 1024, 512) if W % t == 0)
    out = pl.pallas_call(
        _cast_kernel,
        grid=(rows // tr, W // tw),
        in_specs=[pl.BlockSpec((None, tr, tw), lambda r, c: (j, r, c))],
        out_specs=pl.BlockSpec((tr, tw), lambda r, c: (r, c)),
        out_shape=jax.ShapeDtypeStruct((rows, W), jnp.bfloat16),
        compiler_params=_cparams(("parallel", "parallel")),
        name="cast_layer",
    )(w3)
    return out.reshape(lead + (R, W))


def _norm_mod_kernel(x_ref, g_ref, shift_ref, scale_ref, o_ref):
    x = x_ref[...]
    y = x * lax.rsqrt(jnp.mean(x * x, axis=-1, keepdims=True) + EPS)
    y = y * g_ref[...]
    o_ref[...] = (y * (1.0 + scale_ref[...]) + shift_ref[...]).astype(o_ref.dtype)


def norm_mod(x, g, shift, scale):
    B, T, D = x.shape
    nt = T // NORM_TILE
    mod_spec = pl.BlockSpec((None, None, 1, D), lambda b, t: (b, t // (nt - 1), 0, 0))
    return pl.pallas_call(
        _norm_mod_kernel,
        grid=(B, nt),
        in_specs=[
            pl.BlockSpec((None, NORM_TILE, D), lambda b, t: (b, t, 0)),
            pl.BlockSpec((1, D), lambda b, t: (0, 0)),
            mod_spec, mod_spec,
        ],
        out_specs=pl.BlockSpec((None, NORM_TILE, D), lambda b, t: (b, t, 0)),
        out_shape=jax.ShapeDtypeStruct((B, T, D), jnp.bfloat16),
        compiler_params=_cparams(("parallel", "parallel")),
        name="norm_mod",
    )(x, g.reshape(1, D), shift.reshape(B, 2, 1, D), scale.reshape(B, 2, 1, D))


def _mm_kernel(a_ref, b_ref, o_ref, acc_ref, *, nk):
    k = pl.program_id(2)
    part = jnp.dot(a_ref[...], b_ref[...], preferred_element_type=jnp.float32)
    if nk == 1:
        o_ref[...] = part.astype(o_ref.dtype)
        return

    @pl.when(k == 0)
    def _():
        acc_ref[...] = part

    @pl.when(k > 0)
    def _():
        acc_ref[...] += part

    @pl.when(k == nk - 1)
    def _():
        o_ref[...] = acc_ref[...].astype(o_ref.dtype)


def matmul(a, b, *, out_dtype=jnp.bfloat16, tm=ROW_TILE, tn=512, tk=None):
    M, K = a.shape
    _, N = b.shape
    tk = K if tk is None else tk
    nk = K // tk
    return pl.pallas_call(
        functools.partial(_mm_kernel, nk=nk),
        grid=(M // tm, N // tn, nk),
        in_specs=[pl.BlockSpec((tm, tk), lambda i, j, k: (i, k)),
                  pl.BlockSpec((tk, tn), lambda i, j, k: (k, j))],
        out_specs=pl.BlockSpec((tm, tn), lambda i, j, k: (i, j)),
        out_shape=jax.ShapeDtypeStruct((M, N), out_dtype),
        scratch_shapes=[pltpu.VMEM((tm, tn), jnp.float32)],
        compiler_params=_cparams(("parallel", "parallel", "arbitrary")),
        name="matmul",
    )(a, b)


def _mm_swiglu_kernel(a_ref, w1_ref, w3_ref, o_ref):
    a = a_ref[...]
    h1 = jnp.dot(a, w1_ref[...], preferred_element_type=jnp.float32)
    h3 = jnp.dot(a, w3_ref[...], preferred_element_type=jnp.float32)
    o_ref[...] = (h1 * jax.nn.sigmoid(h1) * h3).astype(o_ref.dtype)


def matmul_swiglu(a, w1, w3, *, tm=ROW_TILE, tn=512):
    M, K = a.shape
    _, N = w1.shape
    return pl.pallas_call(
        _mm_swiglu_kernel,
        grid=(M // tm, N // tn),
        in_specs=[pl.BlockSpec((tm, K), lambda i, j: (i, 0)),
                  pl.BlockSpec((K, tn), lambda i, j: (0, j)),
                  pl.BlockSpec((K, tn), lambda i, j: (0, j))],
        out_specs=pl.BlockSpec((tm, tn), lambda i, j: (i, j)),
        out_shape=jax.ShapeDtypeStruct((M, N), jnp.bfloat16),
        compiler_params=_cparams(("parallel", "parallel")),
        name="matmul_swiglu",
    )(a, w1, w3)


def _mm_resgate_kernel(a_ref, b_ref, res_ref, gate_ref, o_ref, acc_ref, *, nk, tm, tiles_per_batch, n_lat):
    i = pl.program_id(0)
    k = pl.program_id(2)
    part = jnp.dot(a_ref[...], b_ref[...], preferred_element_type=jnp.float32)

    def finish(acc):
        row = (i % tiles_per_batch) * tm + lax.broadcasted_iota(jnp.int32, (tm, 1), 0)
        gate = jnp.where(row < n_lat, gate_ref[0:1, :], gate_ref[1:2, :])
        o_ref[...] = res_ref[...] + gate * acc

    if nk == 1:
        finish(part)
        return

    @pl.when(k == 0)
    def _():
        acc_ref[...] = part

    @pl.when(k > 0)
    def _():
        acc_ref[...] += part

    @pl.when(k == nk - 1)
    def _():
        finish(acc_ref[...])


def matmul_resgate(a, b, res, gate, *, n_lat, tm=ROW_TILE, tn=512, tk=None):
    B, T, N = res.shape
    M, K = a.shape
    tk = K if tk is None else tk
    nk = K // tk
    tpb = T // tm
    kern = functools.partial(_mm_resgate_kernel, nk=nk, tm=tm, tiles_per_batch=tpb, n_lat=n_lat)
    out = pl.pallas_call(
        kern,
        grid=(M // tm, N // tn, nk),
        in_specs=[pl.BlockSpec((tm, tk), lambda i, j, k: (i, k)),
                  pl.BlockSpec((tk, tn), lambda i, j, k: (k, j)),
                  pl.BlockSpec((tm, tn), lambda i, j, k: (i, j)),
                  pl.BlockSpec((None, 2, tn), lambda i, j, k: (i // tpb, 0, j))],
        out_specs=pl.BlockSpec((tm, tn), lambda i, j, k: (i, j)),
        out_shape=jax.ShapeDtypeStruct((M, N), jnp.float32),
        scratch_shapes=[pltpu.VMEM((tm, tn), jnp.float32)],
        compiler_params=_cparams(("parallel", "parallel", "arbitrary")),
        name="matmul_resgate",
    )(a, b, res.reshape(M, N), gate)
    return out.reshape(B, T, N)


def _mm2_resgate_kernel(a1_ref, a2_ref, b_ref, res_ref, gate_ref, o_ref, *, tm, tiles_per_batch, n_lat):
    i = pl.program_id(0)
    k1 = a1_ref.shape[1]
    acc = (jnp.dot(a1_ref[...], b_ref[:k1, :], preferred_element_type=jnp.float32)
           + jnp.dot(a2_ref[...], b_ref[k1:, :], preferred_element_type=jnp.float32))
    row = (i % tiles_per_batch) * tm + lax.broadcasted_iota(jnp.int32, (tm, 1), 0)
    gate = jnp.where(row < n_lat, gate_ref[0:1, :], gate_ref[1:2, :])
    o_ref[...] = res_ref[...] + gate * acc


def matmul2_resgate(a1, a2, b, res, gate, *, n_lat, tm=ROW_TILE, tn=512):
    B, T, N = res.shape
    M, K1 = a1.shape
    K2 = a2.shape[1]
    tpb = T // tm
    out = pl.pallas_call(
        functools.partial(_mm2_resgate_kernel, tm=tm, tiles_per_batch=tpb, n_lat=n_lat),
        grid=(M // tm, N // tn),
        in_specs=[pl.BlockSpec((tm, K1), lambda i, j: (i, 0)),
                  pl.BlockSpec((tm, K2), lambda i, j: (i, 0)),
                  pl.BlockSpec((K1 + K2, tn), lambda i, j: (0, j)),
                  pl.BlockSpec((tm, tn), lambda i, j: (i, j)),
                  pl.BlockSpec((None, 2, tn), lambda i, j: (i // tpb, 0, j))],
        out_specs=pl.BlockSpec((tm, tn), lambda i, j: (i, j)),
        out_shape=jax.ShapeDtypeStruct((M, N), jnp.float32),
        compiler_params=_cparams(("parallel", "parallel")),
        name="matmul2_resgate",
    )(a1, a2, b, res.reshape(M, N), gate)
    return out.reshape(B, T, N)


def _moe_combine_kernel(x_ref, y0_ref, y1_ref, gate_ref, o_ref):
    y = y0_ref[...].astype(jnp.float32) + y1_ref[...].astype(jnp.float32)
    o_ref[...] = x_ref[...] + gate_ref[...] * y


def moe_combine(x, y0, y1, gate, *, n_rows):
    B, T, D = x.shape
    nt = T // NORM_TILE
    row = pl.BlockSpec((None, NORM_TILE, D), lambda b, t: (b, t, 0))
    return pl.pallas_call(
        _moe_combine_kernel,
        grid=(B, n_rows // NORM_TILE),
        in_specs=[row, row, row, pl.BlockSpec((None, None, 1, D), lambda b, t: (b, t // (nt - 1), 0, 0))],
        out_specs=row,
        out_shape=jax.ShapeDtypeStruct((B, n_rows, D), jnp.float32),
        compiler_params=_cparams(("parallel", "parallel")),
        name="moe_combine",
    )(x, y0, y1, gate.reshape(B, 2, 1, D))


def _window_attn_kernel(sink_ref, q_ref, kc_ref, kp_ref, ko_ref, kn_ref, vc_ref, vp_ref, vo_ref, vn_ref, o_ref,
                        *, n_lat, n_ctx):
    n = pl.program_id(1)
    h = pl.program_id(2)
    G = WIN_GROUP
    q = q_ref[...]
    qs = jnp.concatenate([q[:, g * HEAD_DIM:(g + 1) * HEAD_DIM] for g in range(G)], axis=0)
    k = jnp.concatenate([kc_ref[...], kp_ref[...], ko_ref[...], kn_ref[...]], axis=0)
    v = jnp.concatenate([vc_ref[...], vp_ref[...], vo_ref[...], vn_ref[...]], axis=0)
    s = lax.dot_general(qs, k, (((1,), (1,)), ((), ())), preferred_element_type=jnp.float32)
    nk = n_ctx + 3 * BLOCK
    col = lax.broadcasted_iota(jnp.int32, (G * BLOCK, nk), 1)
    qpos = n * BLOCK + lax.broadcasted_iota(jnp.int32, (G * BLOCK, nk), 0) % BLOCK
    kpos = (n - 1) * BLOCK + (col - n_ctx)
    valid = (col < n_ctx) | ((jnp.abs(qpos - kpos) <= WINDOW) & (kpos >= 0) & (kpos < n_lat) & (qpos < n_lat))
    s = jnp.where(valid, s, NEG_INF)
    sink = jnp.concatenate(
        [jnp.full((BLOCK, 1), sink_ref[h * G + g], jnp.float32) for g in range(G)], axis=0)
    m = jnp.maximum(jnp.max(s, axis=-1, keepdims=True), sink)
    e = jnp.exp(s - m)
    denom = jnp.sum(e, axis=-1, keepdims=True) + jnp.exp(sink - m)
    p = (e / denom).astype(v.dtype)
    o = jnp.dot(p, v, preferred_element_type=jnp.float32)
    for g in range(G):
        o_ref[:, g * HEAD_DIM:(g + 1) * HEAD_DIM] = o[g * BLOCK:(g + 1) * BLOCK, :].astype(o_ref.dtype)


def window_attention(q, k, v, sink, *, n_ctx, v_col0=0):
    B, T, _ = q.shape
    L = T - n_ctx
    nb = L // BLOCK
    kv_blk = lambda f: pl.BlockSpec((None, BLOCK, HEAD_DIM), f)
    ctx_blk = lambda off: pl.BlockSpec((None, n_ctx, HEAD_DIM), lambda b, n, h: (b, L // n_ctx, off + h))
    prev_blk = lambda off: kv_blk(lambda b, n, h: (b, jnp.maximum(n - 1, 0), off + h))
    own_blk = lambda off: kv_blk(lambda b, n, h: (b, n, off + h))
    next_blk = lambda off: kv_blk(lambda b, n, h: (b, jnp.minimum(n + 1, nb - 1), off + h))
    voff = v_col0 // HEAD_DIM
    return pl.pallas_call(
        functools.partial(_window_attn_kernel, n_lat=L, n_ctx=n_ctx),
        grid=(B, T // BLOCK, WIN_KV_HEADS),
        in_specs=[pl.BlockSpec(memory_space=pltpu.SMEM),
                  pl.BlockSpec((None, BLOCK, WIN_GROUP * HEAD_DIM), lambda b, n, h: (b, n, h)),
                  ctx_blk(0), prev_blk(0), own_blk(0), next_blk(0),
                  ctx_blk(voff), prev_blk(voff), own_blk(voff), next_blk(voff)],
        out_specs=pl.BlockSpec((None, BLOCK, WIN_GROUP * HEAD_DIM), lambda b, n, h: (b, n, h)),
        out_shape=jax.ShapeDtypeStruct((B, T, WIN_Q), jnp.bfloat16),
        compiler_params=_cparams(("parallel", "parallel", "parallel")),
        name="window_attention",
    )(sink.astype(jnp.float32), q, k, k, k, k, v, v, v, v)


def _diff_attn_kernel(lam_ref, q_ref, k_ref, v_ref, g_ref, buf_ref, o_ref, m_ref, l_ref, acc_ref,
                      *, n_keys, tk, out_scale):
    del buf_ref
    m_ref[...] = jnp.full(m_ref.shape, NEG_INF, jnp.float32)
    l_ref[...] = jnp.zeros(l_ref.shape, jnp.float32)
    acc_ref[...] = jnp.zeros(acc_ref.shape, jnp.float32)

    def step(j, carry):
        ks = pl.multiple_of(j * tk, tk)
        v = v_ref[pl.ds(ks, tk), :]
        scores = []
        for sub in range(2):
            q = q_ref[:, sub * DIFF_DIM:(sub + 1) * DIFF_DIM]
            k = k_ref[pl.ds(ks, tk), sub * DIFF_DIM:(sub + 1) * DIFF_DIM]
            scores.append(lax.dot_general(q, k, (((1,), (1,)), ((), ())), preferred_element_type=jnp.float32))
        for sub in range(2):
            s = scores[sub]
            m_old = m_ref[sub]
            m_new = jnp.maximum(m_old, jnp.max(s, axis=-1, keepdims=True))
            alpha = jnp.exp2(m_old - m_new)
            p = jnp.exp2(s - m_new)
            l_ref[sub] = alpha * l_ref[sub] + jnp.sum(p, axis=-1, keepdims=True)
            acc_ref[sub] = alpha * acc_ref[sub] + jnp.dot(p.astype(v.dtype), v, preferred_element_type=jnp.float32)
            m_ref[sub] = m_new
        return carry

    lax.fori_loop(0, n_keys // tk, step, 0)
    lam = lam_ref[0]
    o = acc_ref[0] / l_ref[0] - lam * (acc_ref[1] / l_ref[1])
    o = o * lax.rsqrt(jnp.mean(o * o, axis=-1, keepdims=True) + EPS)
    o_ref[...] = (o * g_ref[...] * out_scale).astype(o_ref.dtype)


def diff_attention(q, k, v, lam, subln_g, out_buf, *, n_q, q_blk0, n_keys, key_blk, tq, tk, out_scale, v_col0=0):
    B = q.shape[0]
    W = 2 * DIFF_DIM
    voff = v_col0 // W
    return pl.pallas_call(
        functools.partial(_diff_attn_kernel, n_keys=n_keys, tk=tk, out_scale=out_scale),
        grid=(B, DIFF_HEADS, n_q // tq),
        in_specs=[pl.BlockSpec(memory_space=pltpu.SMEM),
                  pl.BlockSpec((None, tq, W), lambda b, h, i: (b, q_blk0 + i, h)),
                  pl.BlockSpec((None, n_keys, W), lambda b, h, i: (b, key_blk, h)),
                  pl.BlockSpec((None, n_keys, W), lambda b, h, i: (b, key_blk, voff + h)),
                  pl.BlockSpec((1, W), lambda b, h, i: (0, 0)),
                  pl.BlockSpec(memory_space=pl.ANY)],
        out_specs=pl.BlockSpec((None, tq, W), lambda b, h, i: (b, q_blk0 + i, h)),
        scratch_shapes=[pltpu.VMEM((2, tq, 1), jnp.float32),
                        pltpu.VMEM((2, tq, 1), jnp.float32),
                        pltpu.VMEM((2, tq, W), jnp.float32)],
        out_shape=jax.ShapeDtypeStruct(out_buf.shape, out_buf.dtype),
        input_output_aliases={5: 0},
        compiler_params=_cparams(("parallel", "parallel", "arbitrary")),
        name="diff_attention",
    )(lam.reshape(1).astype(jnp.float32), q, k, v, subln_g.reshape(1, W).astype(jnp.float32), out_buf)


def _moe_up_kernel(te_ref, tv_ref, x_ref, w1_ref, w3_ref, o_ref):
    i = pl.program_id(1)

    @pl.when(tv_ref[i] > 0)
    def _():
        x = x_ref[...]
        h1 = jnp.dot(x, w1_ref[...], preferred_element_type=jnp.float32)
        h3 = jnp.dot(x, w3_ref[...], preferred_element_type=jnp.float32)
        o_ref[...] = (h1 * jax.nn.sigmoid(h1) * h3).astype(o_ref.dtype)

    @pl.when(tv_ref[i] == 0)
    def _():
        o_ref[...] = jnp.zeros(o_ref.shape, o_ref.dtype)


def _moe_down_kernel(te_ref, tv_ref, g_ref, w2_ref, rg_ref, o_ref):
    i = pl.program_id(1)

    @pl.when(tv_ref[i] > 0)
    def _():
        y = jnp.dot(g_ref[...], w2_ref[...], preferred_element_type=jnp.float32)
        o_ref[...] = (rg_ref[...] * y).astype(o_ref.dtype)

    @pl.when(tv_ref[i] == 0)
    def _():
        o_ref[...] = jnp.zeros(o_ref.shape, o_ref.dtype)


def moe_experts(xs, w1, w3, w2, tile_expert, tile_valid, row_gate, *, tn_up=768, tn_down=1024):
    P, D = xs.shape
    F = w1.shape[2]
    tm = MOE_TILE
    up = pl.pallas_call(
        _moe_up_kernel,
        grid_spec=pltpu.PrefetchScalarGridSpec(
            num_scalar_prefetch=2,
            grid=(F // tn_up, P // tm),
            in_specs=[pl.BlockSpec((tm, D), lambda j, i, te, tv: (i, 0)),
                      pl.BlockSpec((None, D, tn_up), lambda j, i, te, tv: (te[i], 0, j)),
                      pl.BlockSpec((None, D, tn_up), lambda j, i, te, tv: (te[i], 0, j))],
            out_specs=pl.BlockSpec((tm, tn_up), lambda j, i, te, tv: (i, j)),
        ),
        out_shape=jax.ShapeDtypeStruct((P, F), jnp.bfloat16),
        compiler_params=_cparams(("parallel", "arbitrary")),
        name="moe_up",
    )(tile_expert, tile_valid, xs, w1, w3)
    return pl.pallas_call(
        _moe_down_kernel,
        grid_spec=pltpu.PrefetchScalarGridSpec(
            num_scalar_prefetch=2,
            grid=(D // tn_down, P // tm),
            in_specs=[pl.BlockSpec((tm, F), lambda j, i, te, tv: (i, 0)),
                      pl.BlockSpec((None, F, tn_down), lambda j, i, te, tv: (te[i], 0, j)),
                      pl.BlockSpec((tm, 1), lambda j, i, te, tv: (i, 0))],
            out_specs=pl.BlockSpec((tm, tn_down), lambda j, i, te, tv: (i, j)),
        ),
        out_shape=jax.ShapeDtypeStruct((P, D), jnp.bfloat16),
        compiler_params=_cparams(("parallel", "arbitrary")),
        name="moe_down",
    )(tile_expert, tile_valid, up, w2, row_gate)


def moe_layer(h2, router_w, router_b, w1, w3, w2):
    N, D = h2.shape
    tm = MOE_TILE
    logits = jnp.dot(h2.astype(jnp.float32), router_w, precision=HI) + router_b
    top_v, top_i = lax.top_k(logits, TOP_K)
    gates = jax.nn.softmax(top_v, axis=-1)
    A = N * TOP_K
    e_flat = top_i.reshape(A).astype(jnp.int32)
    order = jnp.argsort(e_flat, stable=True)
    e_sorted = e_flat[order]
    counts = jnp.sum(e_flat[:, None] == jnp.arange(N_EXPERTS, dtype=jnp.int32)[None, :], axis=0).astype(jnp.int32)
    padded = ((counts + tm - 1) // tm) * tm
    start_unpadded = jnp.cumsum(counts) - counts
    start_padded = jnp.cumsum(padded) - padded
    dest_sorted = start_padded[e_sorted] + (jnp.arange(A, dtype=jnp.int32) - start_unpadded[e_sorted])
    P = A + N_EXPERTS * tm
    tile_start = jnp.arange(P // tm, dtype=jnp.int32) * tm
    ends = jnp.cumsum(padded)
    tile_expert = jnp.minimum(jnp.sum(tile_start[:, None] >= ends[None, :], axis=1), N_EXPERTS - 1).astype(jnp.int32)
    tile_valid = (tile_start < ends[-1]).astype(jnp.int32)
    rows = jnp.arange(P, dtype=jnp.int32)
    row_e = tile_expert[rows // tm]
    row_idx = rows - start_padded[row_e]
    row_ok = (row_idx < counts[row_e]) & (rows < ends[-1])
    src = order[jnp.clip(start_unpadded[row_e] + row_idx, 0, A - 1)]
    row_token = jnp.where(row_ok, src // TOP_K, 0).astype(jnp.int32)
    row_gate = jnp.where(row_ok, gates.reshape(A)[src], 0.0)
    pos = dest_sorted[jnp.argsort(order)].reshape(N, TOP_K)
    take = lambda a, idx: a.at[idx].get(mode="promise_in_bounds")
    xs = take(h2, row_token)
    y = moe_experts(xs, w1, w3, w2, tile_expert, tile_valid, row_gate.reshape(P, 1))
    return take(y, pos[:, 0]), take(y, pos[:, 1])


FFT_R = 128
FFT_K1 = 72
Z_PITCH = FFT_R + 8
S1_PITCH = 2 * FFT_K1 + 8
S2_PITCH = 2 * FFT_R + 8
HY_LANES = 128
FFT_UNROLL = 32


def _dft_tables():
    R, K1 = FFT_R, FFT_K1
    N = R * R
    i32 = jnp.int32
    b = jnp.arange(R, dtype=i32)[:, None, None]
    k1 = jnp.arange(K1, dtype=i32)[None, :, None]
    a = jnp.arange(R, dtype=i32)[None, None, :]
    th = (2.0 * math.pi / N) * ((k1 * (R * a + b)) % N).astype(jnp.float32)
    f1 = jnp.concatenate([jnp.cos(th), -jnp.sin(th)], axis=1)
    w = jnp.where(jnp.arange(K1) > R // 2, 0.0, jnp.where((jnp.arange(K1) % (R // 2)) == 0, 1.0, 2.0)) / N
    the = jnp.swapaxes(th[:, :, :R // 2], 1, 2)
    e = jnp.concatenate([w * jnp.cos(the), -w * jnp.sin(the)], axis=2)
    k2 = jnp.arange(R, dtype=i32)
    ph = (2.0 * math.pi / R) * ((k2[:, None] * k2[None, :]) % R).astype(jnp.float32)
    c, s = jnp.cos(ph), jnp.sin(ph)
    g = jnp.block([[c, s], [-s, c]])
    ginv = jnp.block([[c, -s], [s, c]])
    bf = jnp.bfloat16
    return f1.astype(bf), e.astype(bf), g.astype(bf), ginv.astype(bf)


def _fft_stage1(src_ref, f1_ref, s1_ref, n_a):
    def body(b, carry):
        zb = src_ref[pl.ds(b, n_a, stride=Z_PITCH), :]
        s1_ref[pl.ds(pl.multiple_of(b * S1_PITCH, 8), 2 * FFT_K1), :] = jnp.dot(
            f1_ref[b], zb.astype(jnp.bfloat16), preferred_element_type=jnp.float32)
        return carry
    lax.fori_loop(0, FFT_R, body, 0, unroll=FFT_UNROLL)


def _fft_stage2(s1_ref, g_ref, k1):
    are = s1_ref[pl.ds(k1, FFT_R, stride=S1_PITCH), :]
    aim = s1_ref[pl.ds(FFT_K1 + k1, FFT_R, stride=S1_PITCH), :]
    r = jnp.concatenate([are, aim], axis=0).astype(jnp.bfloat16)
    return jnp.dot(g_ref[...], r, preferred_element_type=jnp.float32)


def _filter_fft_kernel(f_ref, f1_ref, g_ref, h_ref, s1_ref):
    _fft_stage1(f_ref, f1_ref, s1_ref, FFT_R)

    def body(k1, carry):
        h_ref[k1] = _fft_stage2(s1_ref, g_ref, k1).astype(h_ref.dtype)
        return carry
    lax.fori_loop(0, FFT_K1, body, 0, unroll=FFT_UNROLL)


def filter_spectrum(filt_padded, f1, g):
    rows, width = filt_padded.shape
    nt = width // HY_LANES
    once = pl.Buffered(1)
    return pl.pallas_call(
        _filter_fft_kernel,
        grid=(nt,),
        in_specs=[pl.BlockSpec((rows, HY_LANES), lambda c: (0, c)),
                  pl.BlockSpec(f1.shape, lambda c: (0, 0, 0), pipeline_mode=once),
                  pl.BlockSpec(g.shape, lambda c: (0, 0), pipeline_mode=once)],
        out_specs=pl.BlockSpec((None, FFT_K1, 2 * FFT_R, HY_LANES), lambda c: (c, 0, 0, 0)),
        out_shape=jax.ShapeDtypeStruct((nt, FFT_K1, 2 * FFT_R, HY_LANES), jnp.bfloat16),
        scratch_shapes=[pltpu.VMEM((FFT_R * S1_PITCH, HY_LANES), jnp.float32)],
        compiler_params=_cparams(("arbitrary",)),
        name="hyena_filter_fft",
    )(filt_padded, f1, g)


def _short_conv_chunks(x_ref, w_ref, b_ref, dst_ref, n_chunks, pitch=Z_PITCH):
    R = FFT_R
    w = w_ref[...]
    bias = b_ref[...]
    row = lax.broadcasted_iota(jnp.int32, (R, HY_LANES), 0)
    for a in range(n_chunks):
        xc = x_ref[a * R:(a + 1) * R, :].astype(jnp.float32)
        prev = pltpu.roll(xc, 1, axis=0)
        nxt = pltpu.roll(xc, R - 1, axis=0)
        if a > 0:
            last = x_ref[a * R - 16:a * R, :].astype(jnp.float32)[15:16, :]
        else:
            last = jnp.zeros((1, HY_LANES), jnp.float32)
        if a < n_chunks - 1:
            first = x_ref[(a + 1) * R:(a + 1) * R + 16, :].astype(jnp.float32)[0:1, :]
        else:
            first = jnp.zeros((1, HY_LANES), jnp.float32)
        prev = jnp.where(row == 0, last, prev)
        nxt = jnp.where(row == R - 1, first, nxt)
        dst_ref[a * pitch:a * pitch + R, :] = bias + prev * w[0:1, :] + xc * w[1:2, :] + nxt * w[2:3, :]


def _hyena_conv_kernel(v_ref, x_ref, wv_ref, bv_ref, wx_ref, bx_ref, skip_ref, h_ref, f1_ref, e_ref, g_ref, gi_ref,
                       buf_ref, o_ref, z_ref, gate_ref, s1_ref, s2_ref, *, n_chunks):
    del buf_ref
    n = pl.program_id(2)
    R, K1 = FFT_R, FFT_K1

    @pl.when(n == 0)
    def _():
        _short_conv_chunks(v_ref, wv_ref, bv_ref, z_ref, n_chunks)

    _short_conv_chunks(x_ref, wx_ref, bx_ref, gate_ref, n_chunks)

    _fft_stage1(z_ref, f1_ref, s1_ref, n_chunks)

    def mid(k1, carry):
        x = _fft_stage2(s1_ref, g_ref, k1)
        h = h_ref[k1].astype(jnp.float32)
        xr, xi, hr, hi = x[:R], x[R:], h[:R], h[R:]
        p = jnp.concatenate([xr * hr - xi * hi, xr * hi + xi * hr], axis=0).astype(jnp.bfloat16)
        s2_ref[pl.ds(pl.multiple_of(k1 * S2_PITCH, 8), 2 * R), :] = jnp.dot(
            gi_ref[...], p, preferred_element_type=jnp.float32)
        return carry
    lax.fori_loop(0, K1, mid, 0, unroll=FFT_UNROLL)

    skip = skip_ref[...]

    def last(b, carry):
        qre = s2_ref[pl.ds(b, K1, stride=S2_PITCH), :]
        qim = s2_ref[pl.ds(R + b, K1, stride=S2_PITCH), :]
        r = jnp.concatenate([qre, qim], axis=0).astype(jnp.bfloat16)
        conv = jnp.dot(e_ref[b], r, preferred_element_type=jnp.float32)
        zold = z_ref[pl.ds(b, n_chunks, stride=Z_PITCH), :]
        gate = gate_ref[pl.ds(b, n_chunks, stride=Z_PITCH), :]
        z_ref[pl.ds(b, n_chunks, stride=Z_PITCH), :] = gate * (conv + zold * skip)
        return carry
    lax.fori_loop(0, R, last, 0, unroll=FFT_UNROLL)

    @pl.when(n == HY_ORDER - 1)
    def _():
        for a in range(n_chunks):
            o_ref[a * R:(a + 1) * R, :] = z_ref[a * Z_PITCH:a * Z_PITCH + R, :].astype(o_ref.dtype)


def hyena_conv(proj, conv_w, conv_b, skip, spec, tables, out_buf, *, n_lat):
    f1, e, g, ginv = tables
    B = proj.shape[0]
    n_chunks = n_lat // FFT_R
    nt = HY_CH // HY_LANES
    once = pl.Buffered(1)
    grp = lambda n: (1 + n) * nt
    f1h = f1[:, :, :n_chunks]
    return pl.pallas_call(
        functools.partial(_hyena_conv_kernel, n_chunks=n_chunks),
        grid=(B, nt, HY_ORDER),
        in_specs=[pl.BlockSpec((None, n_lat, HY_LANES), lambda b, c, n: (b, 0, c)),
                  pl.BlockSpec((None, n_lat, HY_LANES), lambda b, c, n: (b, 0, grp(n) + c)),
                  pl.BlockSpec((HY_SHORT, HY_LANES), lambda b, c, n: (0, c)),
                  pl.BlockSpec((1, HY_LANES), lambda b, c, n: (0, c)),
                  pl.BlockSpec((HY_SHORT, HY_LANES), lambda b, c, n: (0, grp(n) + c)),
                  pl.BlockSpec((1, HY_LANES), lambda b, c, n: (0, grp(n) + c)),
                  pl.BlockSpec((None, 1, HY_LANES), lambda b, c, n: (n, 0, c)),
                  pl.BlockSpec((None, FFT_K1, 2 * FFT_R, HY_LANES), lambda b, c, n: (n * nt + c, 0, 0, 0),
                               pipeline_mode=once),
                  pl.BlockSpec(f1h.shape, lambda b, c, n: (0, 0, 0), pipeline_mode=once),
                  pl.BlockSpec(e.shape, lambda b, c, n: (0, 0, 0), pipeline_mode=once),
                  pl.BlockSpec(g.shape, lambda b, c, n: (0, 0), pipeline_mode=once),
                  pl.BlockSpec(ginv.shape, lambda b, c, n: (0, 0), pipeline_mode=once),
                  pl.BlockSpec(memory_space=pl.ANY)],
        out_specs=pl.BlockSpec((None, n_lat, HY_LANES), lambda b, c, n: (b, 0, c)),
        out_shape=jax.ShapeDtypeStruct(out_buf.shape, out_buf.dtype),
        input_output_aliases={12: 0},
        scratch_shapes=[pltpu.VMEM((n_chunks * Z_PITCH, HY_LANES), jnp.float32),
                        pltpu.VMEM((n_chunks * Z_PITCH, HY_LANES), jnp.float32),
                        pltpu.VMEM((FFT_R * S1_PITCH, HY_LANES), jnp.float32),
                        pltpu.VMEM((FFT_K1 * S2_PITCH, HY_LANES), jnp.float32)],
        compiler_params=_cparams(("parallel", "parallel", "arbitrary")),
        name="hyena_conv",
    )(proj, proj, conv_w, conv_b.reshape(1, -1), conv_w, conv_b.reshape(1, -1),
      skip.reshape(HY_ORDER, 1, HY_CH), spec, f1h, e, g, ginv, out_buf)


def _ctx_dft_tables(n_ctx):
    N = 2 * n_ctx
    nk = -(-(n_ctx + 1) // 16) * 16
    k = jnp.arange(nk, dtype=jnp.int32)[:, None]
    t = jnp.arange(N, dtype=jnp.int32)[None, :]
    th = (2.0 * math.pi / N) * ((k * t) % N).astype(jnp.float32)
    fwd = jnp.concatenate([jnp.cos(th), -jnp.sin(th)], axis=0)
    w = jnp.where(k > n_ctx, 0.0, jnp.where((k % n_ctx) == 0, 1.0, 2.0)) / N
    inv = jnp.concatenate([(w * jnp.cos(th[:, :n_ctx])).T, (-w * jnp.sin(th[:, :n_ctx])).T], axis=1)
    return fwd.astype(jnp.bfloat16), inv.astype(jnp.bfloat16)


def _hyena_ctx_kernel(v_ref, x1_ref, x2_ref, wv_ref, bv_ref, w1_ref, b1_ref, w2_ref, b2_ref, skip_ref,
                      f0_ref, f1_ref, fwd_ref, inv_ref, buf_ref, o_ref, z_ref, g1_ref, g2_ref, *, n_ctx):
    del buf_ref
    nc = n_ctx // FFT_R
    _short_conv_chunks(v_ref, wv_ref, bv_ref, z_ref, nc, pitch=FFT_R)
    _short_conv_chunks(x1_ref, w1_ref, b1_ref, g1_ref, nc, pitch=FFT_R)
    _short_conv_chunks(x2_ref, w2_ref, b2_ref, g2_ref, nc, pitch=FFT_R)
    nk = fwd_ref.shape[0] // 2
    z = z_ref[...]
    for n, (filt_ref, gate_ref) in enumerate(((f0_ref, g1_ref), (f1_ref, g2_ref))):
        h = jnp.dot(fwd_ref[...], filt_ref[...].astype(jnp.bfloat16), preferred_element_type=jnp.float32)
        x = jnp.dot(fwd_ref[:, :n_ctx], z.astype(jnp.bfloat16), preferred_element_type=jnp.float32)
        xr, xi, hr, hi = x[:nk], x[nk:], h[:nk], h[nk:]
        p = jnp.concatenate([xr * hr - xi * hi, xr * hi + xi * hr], axis=0).astype(jnp.bfloat16)
        conv = jnp.dot(inv_ref[...], p, preferred_element_type=jnp.float32)
        z = gate_ref[...] * (conv + z * skip_ref[n:n + 1, :])
    o_ref[...] = z.astype(o_ref.dtype)


def hyena_ctx(proj, conv_w, conv_b, skip, filt, tables, out_buf, *, n_lat, n_ctx):
    fwd, inv = tables
    B = proj.shape[0]
    nt = HY_CH // HY_LANES
    rb = n_lat // n_ctx
    cb2 = conv_b.reshape(1, -1)
    row = lambda g: pl.BlockSpec((None, n_ctx, HY_LANES), lambda b, c: (b, rb, g * nt + c))
    wsp = lambda g: pl.BlockSpec((HY_SHORT, HY_LANES), lambda b, c: (0, g * nt + c))
    bsp = lambda g: pl.BlockSpec((1, HY_LANES), lambda b, c: (0, g * nt + c))
    fsp = lambda n: pl.BlockSpec((2 * n_ctx, HY_LANES), lambda b, c: (0, n * nt + c))
    scr = pltpu.VMEM((n_ctx, HY_LANES), jnp.float32)
    return pl.pallas_call(
        functools.partial(_hyena_ctx_kernel, n_ctx=n_ctx),
        grid=(B, nt),
        in_specs=[row(0), row(1), row(2), wsp(0), bsp(0), wsp(1), bsp(1), wsp(2), bsp(2),
                  pl.BlockSpec((HY_ORDER, HY_LANES), lambda b, c: (0, c)),
                  fsp(0), fsp(1),
                  pl.BlockSpec(fwd.shape, lambda b, c: (0, 0)),
                  pl.BlockSpec(inv.shape, lambda b, c: (0, 0)),
                  pl.BlockSpec(memory_space=pl.ANY)],
        out_specs=pl.BlockSpec((None, n_ctx, HY_LANES), lambda b, c: (b, rb, c)),
        out_shape=jax.ShapeDtypeStruct(out_buf.shape, out_buf.dtype),
        input_output_aliases={14: 0},
        scratch_shapes=[scr, scr, scr],
        compiler_params=_cparams(("parallel", "parallel")),
        name="hyena_ctx",
    )(proj, proj, proj, conv_w, cb2, conv_w, cb2, conv_w, cb2, skip, filt, filt, fwd, inv, out_buf)


def _norm_rope_kernel(x_ref, g_ref, cos_ref, sin_ref, o_ref, *, scale, nh):
    cos = cos_ref[...]
    sin = sin_ref[...]
    g = g_ref[...]
    lane = lax.broadcasted_iota(jnp.int32, cos.shape, 1)
    lower = (lane % (HEAD_DIM // 2)) < (HEAD_DIM // 4)
    for h in range(nh):
        x = x_ref[:, h * HEAD_DIM:(h + 1) * HEAD_DIM].astype(jnp.float32)
        y = x * lax.rsqrt(jnp.mean(x * x, axis=-1, keepdims=True) + EPS) * g
        rot = jnp.where(lower, -pltpu.roll(y, HEAD_DIM - HEAD_DIM // 4, axis=1), pltpu.roll(y, HEAD_DIM // 4, axis=1))
        o_ref[:, h * HEAD_DIM:(h + 1) * HEAD_DIM] = ((y * cos + rot * sin) * scale).astype(o_ref.dtype)


def head_norm_rope(t, col0, width, g, cos, sin, scale, *, tr=ROW_TILE, nh=4):
    B, T, _ = t.shape
    wb = nh * HEAD_DIM
    c0 = col0 // wb
    return pl.pallas_call(
        functools.partial(_norm_rope_kernel, scale=scale, nh=nh),
        grid=(B, T // tr, width // wb),
        in_specs=[pl.BlockSpec((None, tr, wb), lambda b, r, j: (b, r, c0 + j)),
                  pl.BlockSpec((1, HEAD_DIM), lambda b, r, j: (0, 0)),
                  pl.BlockSpec((tr, HEAD_DIM), lambda b, r, j: (r, 0)),
                  pl.BlockSpec((tr, HEAD_DIM), lambda b, r, j: (r, 0))],
        out_specs=pl.BlockSpec((None, tr, wb), lambda b, r, j: (b, r, j)),
        out_shape=jax.ShapeDtypeStruct((B, T, width), jnp.bfloat16),
        compiler_params=_cparams(("parallel", "parallel", "parallel")),
        name="head_norm_rope",
    )(t, g.reshape(1, HEAD_DIM).astype(jnp.float32), cos, sin)


def _rope_tables(L, n_ctx):
    rows = jnp.repeat(jnp.arange(L // GRID_W), GRID_W)
    cols = jnp.tile(jnp.arange(GRID_W), L // GRID_W)
    quarter = HEAD_DIM // 4
    inv = ROPE_BASE ** (-jnp.arange(quarter, dtype=jnp.float32) / quarter)
    ar = rows.astype(jnp.float32)[:, None] * inv
    ac = cols.astype(jnp.float32)[:, None] * inv
    ang = jnp.concatenate([ar, ar, ac, ac], axis=-1)
    cos = jnp.concatenate([jnp.cos(ang), jnp.ones((n_ctx, HEAD_DIM), jnp.float32)], axis=0)
    sin = jnp.concatenate([jnp.sin(ang), jnp.zeros((n_ctx, HEAD_DIM), jnp.float32)], axis=0)
    return cos, sin


def _implicit_filters(L, w_in, w_hid, b, freq, w_out, pitch=FFT_R):
    f32 = jnp.float32
    t = jnp.linspace(0.0, 1.0, L, dtype=f32)[:, None]
    w = (2.0 * math.pi / L) * jnp.arange(L, dtype=f32)[:, None]
    f = jnp.linspace(1e-4, HY_BANDS - 1, HY_BANDS, dtype=f32)[None, :]
    z = jnp.concatenate([t, jnp.cos(f * w), -jnp.sin(f * w)], axis=-1)
    h = jnp.sin(freq[0] * (jnp.dot(z, w_in, precision=HI) + b[0]))
    for n in range(HY_FILTER_HIDDEN_LAYERS):
        h = jnp.sin(freq[n + 1] * (jnp.dot(h, w_hid[n], precision=HI) + b[n + 1]))
    width = HY_ORDER * HY_CH
    max_decay = math.log(HY_DECAY_TARGET) / HY_FAST_DECAY
    min_decay = math.log(HY_DECAY_TARGET) / HY_SLOW_DECAY
    deltas = jnp.abs(jnp.linspace(min_decay, max_decay, width, dtype=f32))[None, :]
    hb = jnp.concatenate([jnp.zeros((1, h.shape[1]), f32), h[:0:-1]], axis=0)
    tb = jnp.concatenate([jnp.zeros((1, 1), f32), t[:0:-1]], axis=0)
    chunk = lambda a: jnp.pad(a.reshape(2, L // FFT_R, FFT_R, a.shape[-1]),
                              ((0, 0), (0, 0), (0, pitch - FFT_R), (0, 0))).reshape(2, -1, a.shape[-1])
    hh = chunk(jnp.stack([h, hb]))
    tt = chunk(jnp.stack([t, tb]))
    w2 = jnp.stack([w_out[:, :width], w_out[:, width:]])
    filt = jnp.einsum('hrk,hkw->hrw', hh, w2, precision=HI) * jnp.exp(-tt * deltas)
    return filt.reshape(-1, width)


def _ada_modulation(cv, down, up, b):
    m = jnp.dot(jnp.dot(jax.nn.silu(cv), down, precision=HI), up, precision=HI) + b
    return m.reshape(m.shape[:-1] + (N_MOD, m.shape[-1] // N_MOD))


def kernel(x, c, ctx, c_ctx, norm_g, ada_down, ada_up, ada_b, ev_w_in, ev_conv_w, ev_conv_b, ev_filt_w_in, ev_filt_w_hid, ev_filt_b, ev_filt_freq, ev_filt_w_out, ev_hy_skip, ev_qk_g, ev_sink, ev_w_out, ev_ffn_w1, ev_ffn_w3, ev_ffn_w2, od_w_qkv, od_qk_g, od_lambda, od_subln_g, od_w_out, od_router_w, od_router_b, od_moe_w1, od_moe_w3, od_moe_w2):
    B, L, D = x.shape
    Lc = ctx.shape[1]
    T = L + Lc
    bf16 = jnp.bfloat16
    cos, sin = _rope_tables(L, Lc)
    tables = _dft_tables()
    ctx_tables = _ctx_dft_tables(Lc)
    X = jnp.concatenate([x, ctx], axis=1)
    qk_scale = HEAD_DIM ** -0.5

    for i in range(DEPTH):
        j = i // 2
        m_l = _ada_modulation(c, ada_down[i], ada_up[i], ada_b[i])
        m_c = _ada_modulation(c_ctx, ada_down[i], ada_up[i], ada_b[i])
        mods = jnp.stack([m_l, jnp.broadcast_to(m_c[None], (B, N_MOD, D))], axis=1)

        h = norm_mod(X, norm_g[i, 0], mods[:, :, 0], mods[:, :, 1])
        hf = h.reshape(B * T, D)
        if i % 2 == 0:
            proj = matmul(hf, cast_layer(ev_w_in, j), tn=1024).reshape(B, T, -1)
            v_col0 = HY_WIDTH + WIN_Q + WIN_KV
            q = head_norm_rope(proj, HY_WIDTH, WIN_Q, ev_qk_g[j, 0], cos, sin, qk_scale)
            k = head_norm_rope(proj, HY_WIDTH + WIN_Q, WIN_KV, ev_qk_g[j, 1], cos, sin, 1.0)
            filt_args = (ev_filt_w_in[j], ev_filt_w_hid[j], ev_filt_b[j], ev_filt_freq[j], ev_filt_w_out[j])
            spec = filter_spectrum(_implicit_filters(L, *filt_args, pitch=Z_PITCH), tables[0], tables[2])
            hy = jnp.zeros((B, T, HY_CH), bf16)
            hy = hyena_conv(proj, ev_conv_w[j], ev_conv_b[j], ev_hy_skip[j], spec, tables, hy, n_lat=L)
            hy = hyena_ctx(proj, ev_conv_w[j], ev_conv_b[j], ev_hy_skip[j], _implicit_filters(Lc, *filt_args),
                           ctx_tables, hy, n_lat=L, n_ctx=Lc)
            att = window_attention(q, k, proj, ev_sink[j], n_ctx=Lc, v_col0=v_col0)
            X = matmul2_resgate(hy.reshape(B * T, HY_CH), att.reshape(B * T, WIN_Q), cast_layer(ev_w_out, j),
                                X, mods[:, :, 2], n_lat=L)
        else:
            lam_init = 0.8 - 0.6 * math.exp(-0.3 * i)
            qkv = matmul(hf, cast_layer(od_w_qkv, j), tn=1024).reshape(B, T, -1)
            q = head_norm_rope(qkv, 0, DIFF_Q, od_qk_g[j, 0], cos, sin, DIFF_DIM ** -0.5 * math.log2(math.e), nh=8)
            k = head_norm_rope(qkv, DIFF_Q, DIFF_Q, od_qk_g[j, 1], cos, sin, 1.0, nh=8)
            lp = od_lambda[j].astype(jnp.float32)
            lam = jnp.exp(jnp.sum(lp[0] * lp[1])) - jnp.exp(jnp.sum(lp[2] * lp[3])) + lam_init
            attn = functools.partial(diff_attention, q, k, qkv, lam, od_subln_g[j], v_col0=2 * DIFF_Q,
                                     out_scale=1.0 - lam_init)
            o = attn(h, n_q=L, q_blk0=0, n_keys=T, key_blk=0, tq=1024, tk=T // 3)
            o = attn(o, n_q=Lc, q_blk0=L // Lc, n_keys=Lc, key_blk=L // Lc, tq=Lc, tk=Lc)
            X = matmul_resgate(o.reshape(B * T, D), cast_layer(od_w_out, j), X, mods[:, :, 2], n_lat=L)
        h2 = norm_mod(X, norm_g[i, 1], mods[:, :, 3], mods[:, :, 4]).reshape(B * T, D)
        if i % 2 == 0:
            gact = matmul_swiglu(h2, cast_layer(ev_ffn_w1, j), cast_layer(ev_ffn_w3, j))
            X = matmul_resgate(gact, cast_layer(ev_ffn_w2, j), X, mods[:, :, 5], n_lat=L)
        else:
            y0, y1 = moe_layer(h2, od_router_w[j], od_router_b[j], cast_layer(od_moe_w1, j),
                               cast_layer(od_moe_w3, j), cast_layer(od_moe_w2, j))
            X = moe_combine(X, y0.reshape(B, T, D), y1.reshape(B, T, D), mods[:, :, 5],
                            n_rows=L if i == DEPTH - 1 else T)
    return X
```
